```python
import math
import jax
import jax.numpy as jnp
from jax import lax
import numpy as np

D_MODEL = 1024
BATCH = 4
SEQ = 4096
DEPTH = 2
DEC_BATCH = 128
DEC_SEQ = 4
PAST_LEN = 16384
PAGE_SIZE = 128

SSM_GROUP = 16
N_GROUPS = D_MODEL // SSM_GROUP
SSM_STATE = 64
HEAD_DIM = 64
N_HEADS = D_MODEL // HEAD_DIM
N_KV_HEADS = 4
Q_PER_KV = N_HEADS // N_KV_HEADS
WINDOW = 128
NUM_BUCKETS = 32
MAX_DISTANCE = 128
D_FF = ((8 * D_MODEL // 3 + 127) // 128) * 128
CONV_WIDTH = 3
PLE_DIM = 256
N_A_LAYERS = DEPTH // 2
N_B_LAYERS = DEPTH - N_A_LAYERS
EPS = 1e-6

kernel_name = 'yoco_s5_swa_sink_convffn_step'


def _rmsnorm(x, g):
    xf = x.astype(jnp.float32)
    y = xf * lax.rsqrt(jnp.mean(xf * xf, axis=-1, keepdims=True) + EPS)
    return (y * g.astype(jnp.float32)).astype(x.dtype)


def _complex_affine_combine(e1, e2):
    a1r, a1i, b1r, b1i = e1
    a2r, a2i, b2r, b2i = e2
    return (a1r * a2r - a1i * a2i,
            a1r * a2i + a1i * a2r,
            a2r * b1r - a2i * b1i + b2r,
            a2r * b1i + a2i * b1r + b2i)


def _s5_ssm(u, h0_re, h0_im, lam_re, lam_im, log_dt, b_re, b_im, c_re, c_im, d_skip):
    f32 = jnp.float32
    n, l, _ = u.shape
    uf = u.astype(f32)
    ug = uf.reshape(n, l, N_GROUPS, SSM_GROUP)
    lr = lam_re.astype(f32)
    li = lam_im.astype(f32)
    dt = jnp.exp(log_dt.astype(f32))[:, None]
    mag = jnp.exp(lr * dt)
    ang = li * dt
    ab_re = mag * jnp.cos(ang)
    ab_im = mag * jnp.sin(ang)
    den = lr * lr + li * li
    nr = ab_re - 1.0
    f_re = (nr * lr + ab_im * li) / den
    f_im = (ab_im * lr - nr * li) / den
    br = b_re.astype(f32)
    bi = b_im.astype(f32)
    bb_re = f_re[..., None] * br - f_im[..., None] * bi
    bb_im = f_re[..., None] * bi + f_im[..., None] * br
    bu_re = jnp.einsum('nlgc,gpc->nlgp', ug, bb_re)
    bu_im = jnp.einsum('nlgc,gpc->nlgp', ug, bb_im)
    h0r = h0_re.astype(f32)
    h0i = h0_im.astype(f32)
    bu_re = bu_re.at[:, 0].add(ab_re * h0r - ab_im * h0i)
    bu_im = bu_im.at[:, 0].add(ab_re * h0i + ab_im * h0r)
    a_re = jnp.broadcast_to(ab_re, (1, l) + ab_re.shape)
    a_im = jnp.broadcast_to(ab_im, (1, l) + ab_im.shape)
    _, _, h_re, h_im = lax.associative_scan(_complex_affine_combine, (a_re, a_im, bu_re, bu_im), axis=1)
    y = (jnp.einsum('nlgp,gcp->nlgc', h_re, c_re.astype(f32))
         - jnp.einsum('nlgp,gcp->nlgc', h_im, c_im.astype(f32)))
    y = y.reshape(n, l, D_MODEL) + d_skip.astype(f32) * uf
    return y, h_re[:, -1], h_im[:, -1]


def _conv_ffn(xn, buf, w_up, conv_w, conv_b, w_down):
    up = xn @ w_up
    l = up.shape[1]
    ext = jnp.concatenate([buf.astype(up.dtype), up], axis=1)
    c = conv_b + ext[:, 0:l] * conv_w[0]
    for k in range(1, CONV_WIDTH):
        c = c + ext[:, k:k + l] * conv_w[k]
    gate, val = jnp.split(c, 2, axis=-1)
    return (jax.nn.silu(gate) * val) @ w_down, ext[:, l:]


def _t5_bucket(d):
    max_exact = NUM_BUCKETS // 2
    df = jnp.maximum(d, 1).astype(jnp.float32)
    large = max_exact + (jnp.log(df / max_exact) / math.log(MAX_DISTANCE / max_exact)
                         * (NUM_BUCKETS - max_exact)).astype(jnp.int32)
    return jnp.where(d < max_exact, d, jnp.minimum(large, NUM_BUCKETS - 1))


def _shared_kv(h, kbuf, vbuf, buf_valid, g_kv, w_k, w_v, k_norm, rel_bias):
    n, l, _ = h.shape
    s = _rmsnorm(h, g_kv)
    k = _rmsnorm((s @ w_k).reshape(n, l, N_KV_HEADS, HEAD_DIM), k_norm)
    v = (s @ w_v).reshape(n, l, N_KV_HEADS, HEAD_DIM)
    k_full = jnp.concatenate([kbuf.astype(k.dtype), k], axis=1)
    v_full = jnp.concatenate([vbuf.astype(v.dtype), v], axis=1)
    lb = kbuf.shape[1]
    qb = WINDOW if l > WINDOW else l
    nb = l // qb
    lk = lb + qb
    idx = (jnp.arange(nb) * qb)[:, None] + jnp.arange(lk)[None, :]
    d = lb + jnp.arange(qb)[:, None] - jnp.arange(lk)[None, :]
    bias = jnp.transpose(rel_bias[_t5_bucket(jnp.maximum(d, 0))], (2, 0, 1))
    key_ok = jnp.logical_or(idx >= lb, buf_valid)
    mask = ((d >= 0) & (d <= WINDOW))[None] & key_ok[:, None, :]
    return (k_full[:, idx], v_full[:, idx], bias, mask, k_full[:, -lb:], v_full[:, -lb:])


def _window_attention(xn, k_blk, v_blk, bias, mask, w_q, q_norm, sinks, w_o):
    f32 = jnp.float32
    n, l, _ = xn.shape
    nb, lk = k_blk.shape[1], k_blk.shape[2]
    qb = l // nb
    q = _rmsnorm((xn @ w_q).reshape(n, nb, qb, N_KV_HEADS, Q_PER_KV, HEAD_DIM), q_norm)
    s = jnp.einsum('nbtkrd,nbskd->nbkrts', q.astype(f32), k_blk.astype(f32)) * (HEAD_DIM ** -0.5)
    s = s + bias.reshape(N_KV_HEADS, Q_PER_KV, qb, lk).astype(f32)
    s = jnp.where(mask[None, :, None, None], s, -jnp.inf)
    sink = sinks.astype(f32).reshape(N_KV_HEADS, Q_PER_KV, 1, 1)
    m = jnp.maximum(jnp.max(s, axis=-1, keepdims=True), sink)
    pr = jnp.exp(s - m)
    w = pr / (jnp.sum(pr, axis=-1, keepdims=True) + jnp.exp(sink - m))
    o = jnp.einsum('nbkrts,nbskd->nbtkrd', w, v_blk.astype(f32))
    o = o.reshape(n, l, N_HEADS * HEAD_DIM).astype(xn.dtype)
    return o @ w_o


def _trunk(x, p, ssm_re0, ssm_im0, conv0, kbuf, vbuf, buf_valid, prm):
    h = x
    ssm_re, ssm_im, conv = [], [], []
    kv = None
    for i in range(DEPTH):
        xn = _rmsnorm(h, prm['g_mix'][i])
        if i < N_A_LAYERS:
            y, hr, hi = _s5_ssm(xn, ssm_re0[i], ssm_im0[i], prm['ssm_lam_re'][i], prm['ssm_lam_im'][i],
                                prm['ssm_log_dt'][i], prm['ssm_b_re'][i], prm['ssm_b_im'][i],
                                prm['ssm_c_re'][i], prm['ssm_c_im'][i], prm['ssm_d'][i])
            ssm_re.append(hr)
            ssm_im.append(hi)
            z = jax.nn.gelu(y)
            gl = z @ prm['w_glu'][i] + prm['b_glu'][i]
            h = h + (gl[..., :D_MODEL] * jax.nn.sigmoid(gl[..., D_MODEL:])).astype(h.dtype)
        else:
            j = i - N_A_LAYERS
            if kv is None:
                kv = _shared_kv(h, kbuf, vbuf, buf_valid, prm['g_kv'], prm['w_k'], prm['w_v'],
                                prm['k_norm'], prm['rel_bias'])
            h = h + _window_attention(xn, kv[0], kv[1], kv[2], kv[3], prm['w_q'][j], prm['q_norm'][j],
                                      prm['sinks'][j], prm['w_o'][j])
        f, cb = _conv_ffn(_rmsnorm(h, prm['g_ffn'][i]), conv0[i], prm['w_up'][i], prm['conv_w'][i],
                          prm['conv_b'][i], prm['w_down'][i])
        conv.append(cb)
        h = h + f
        gate = jax.nn.sigmoid(_rmsnorm(h, prm['g_ple'][i]) @ prm['w_ple_gate'][i])
        h = h + (p[i] @ prm['w_ple_in'][i]) * gate
    return h, jnp.stack(ssm_re), jnp.stack(ssm_im), jnp.stack(conv), kv[4], kv[5]


def setup_inputs(seed: int = 0) -> dict:
    key = jax.random.key(seed)
    ks = iter(jax.random.split(key, 48))
    f32 = jnp.float32

    def nrm(shape, scale):
        return scale * jax.random.normal(next(ks), shape, f32)

    win_buf = min(WINDOW, PAST_LEN)
    hk = N_KV_HEADS * HEAD_DIM
    hq = N_HEADS * HEAD_DIM
    return {
        'x_prompt': nrm((BATCH, SEQ, D_MODEL), 1.0),
        'x_sample': nrm((DEC_BATCH, DEC_SEQ, D_MODEL), 1.0),
        'state_ssm_re': nrm((N_A_LAYERS, DEC_BATCH, N_GROUPS, SSM_STATE), 0.5),
        'state_ssm_im': nrm((N_A_LAYERS, DEC_BATCH, N_GROUPS, SSM_STATE), 0.5),
        'state_ffn_conv': nrm((DEPTH, DEC_BATCH, CONV_WIDTH - 1, 2 * D_FF), 1.0),
        'cache_k_win': nrm((DEC_BATCH, win_buf, N_KV_HEADS, HEAD_DIM), 1.0),
        'cache_v_win': nrm((DEC_BATCH, win_buf, N_KV_HEADS, HEAD_DIM), 1.0),
        'p_prompt': nrm((DEPTH, BATCH, SEQ, PLE_DIM), 1.0),
        'p_sample': nrm((DEPTH, DEC_BATCH, DEC_SEQ, PLE_DIM), 1.0),
        'g_mix': 1.0 + nrm((DEPTH, D_MODEL), 0.05),
        'g_ffn': 1.0 + nrm((DEPTH, D_MODEL), 0.05),
        'g_ple': 1.0 + nrm((DEPTH, D_MODEL), 0.05),
        'ssm_lam_re': -0.5 + nrm((N_A_LAYERS, N_GROUPS, SSM_STATE), 0.01),
        'ssm_lam_im': math.pi * jnp.arange(SSM_STATE, dtype=f32) + nrm((N_A_LAYERS, N_GROUPS, SSM_STATE), 0.01),
        'ssm_log_dt': jax.random.uniform(next(ks), (N_A_LAYERS, N_GROUPS), f32, math.log(1e-3), math.log(1e-1)),
        'ssm_b_re': nrm((N_A_LAYERS, N_GROUPS, SSM_STATE, SSM_GROUP), (2 * SSM_GROUP) ** -0.5),
        'ssm_b_im': nrm((N_A_LAYERS, N_GROUPS, SSM_STATE, SSM_GROUP), (2 * SSM_GROUP) ** -0.5),
        'ssm_c_re': nrm((N_A_LAYERS, N_GROUPS, SSM_GROUP, SSM_STATE), SSM_STATE ** -0.5),
        'ssm_c_im': nrm((N_A_LAYERS, N_GROUPS, SSM_GROUP, SSM_STATE), SSM_STATE ** -0.5),
        'ssm_d': nrm((N_A_LAYERS, D_MODEL), 1.0),
        'w_glu': nrm((N_A_LAYERS, D_MODEL, 2 * D_MODEL), D_MODEL ** -0.5),
        'b_glu': nrm((N_A_LAYERS, 2 * D_MODEL), 0.01),
        'g_kv': 1.0 + nrm((D_MODEL,), 0.05),
        'w_k': nrm((D_MODEL, hk), D_MODEL ** -0.5),
        'w_v': nrm((D_MODEL, hk), D_MODEL ** -0.5),
        'k_norm': 1.0 + nrm((HEAD_DIM,), 0.05),
        'w_q': nrm((N_B_LAYERS, D_MODEL, hq), D_MODEL ** -0.5),
        'q_norm': 1.0 + nrm((N_B_LAYERS, HEAD_DIM), 0.05),
        'sinks': nrm((N_B_LAYERS, N_HEADS), 0.5),
        'w_o': nrm((N_B_LAYERS, hq, D_MODEL), hq ** -0.5),
        'rel_bias': nrm((NUM_BUCKETS, N_HEADS), 0.1),
        'w_up': nrm((DEPTH, D_MODEL, 2 * D_FF), D_MODEL ** -0.5),
        'conv_w': nrm((DEPTH, CONV_WIDTH, 2 * D_FF), CONV_WIDTH ** -0.5),
        'conv_b': nrm((DEPTH, 2 * D_FF), 0.01),
        'w_down': nrm((DEPTH, D_FF, D_MODEL), D_FF ** -0.5),
        'w_ple_in': nrm((DEPTH, PLE_DIM, D_MODEL), PLE_DIM ** -0.5),
        'w_ple_gate': nrm((DEPTH, D_MODEL, D_MODEL), D_MODEL ** -0.5),
    }


def reference(x_prompt, x_sample, state_ssm_re, state_ssm_im, state_ffn_conv, cache_k_win, cache_v_win,
              p_prompt, p_sample, g_mix, g_ffn, g_ple, ssm_lam_re, ssm_lam_im, ssm_log_dt, ssm_b_re,
              ssm_b_im, ssm_c_re, ssm_c_im, ssm_d, w_glu, b_glu, g_kv, w_k, w_v, k_norm, w_q, q_norm,
              sinks, w_o, rel_bias, w_up, conv_w, conv_b, w_down, w_ple_in, w_ple_gate):
    prm = dict(g_mix=g_mix, g_ffn=g_ffn, g_ple=g_ple, ssm_lam_re=ssm_lam_re, ssm_lam_im=ssm_lam_im,
               ssm_log_dt=ssm_log_dt, ssm_b_re=ssm_b_re, ssm_b_im=ssm_b_im, ssm_c_re=ssm_c_re,
               ssm_c_im=ssm_c_im, ssm_d=ssm_d, w_glu=w_glu, b_glu=b_glu, g_kv=g_kv, w_k=w_k, w_v=w_v,
               k_norm=k_norm, w_q=w_q, q_norm=q_norm, sinks=sinks, w_o=w_o, rel_bias=rel_bias,
               w_up=w_up, conv_w=conv_w, conv_b=conv_b, w_down=w_down, w_ple_in=w_ple_in,
               w_ple_gate=w_ple_gate)
    bp = x_prompt.shape[0]
    zero_ssm = jnp.zeros((N_A_LAYERS, bp, N_GROUPS, SSM_STATE), jnp.float32)
    zero_conv = jnp.zeros((DEPTH, bp, CONV_WIDTH - 1, 2 * D_FF), x_prompt.dtype)
    zero_win = jnp.zeros((bp, WINDOW, N_KV_HEADS, HEAD_DIM), x_prompt.dtype)
    y_prompt, ssm_re_p, ssm_im_p, conv_p, k_win_p, v_win_p = _trunk(
        x_prompt, p_prompt, zero_ssm, zero_ssm, zero_conv, zero_win, zero_win, False, prm)
    y_sample, ssm_re_s, ssm_im_s, conv_s, k_win_s, v_win_s = _trunk(
        x_sample, p_sample, state_ssm_re, state_ssm_im, state_ffn_conv, cache_k_win, cache_v_win, True, prm)
    return (y_prompt, y_sample, ssm_re_p, ssm_im_p, ssm_re_s, ssm_im_s, conv_p, conv_s,
            k_win_p, v_win_p, k_win_s, v_win_s)
```

```python
import functools
import math

import jax
import jax.numpy as jnp
from jax import lax
from jax.experimental import pallas as pl
from jax.experimental.pallas import tpu as pltpu

F32 = jnp.float32
BF16 = jnp.bfloat16

EPS = 1e-6
SSM_GROUP = 16
SSM_STATE = 64
HEAD_DIM = 64
N_KV_HEADS = 4
WINDOW = 128
NUM_BUCKETS = 32
MAX_DISTANCE = 128
CONV_WIDTH = 3

LANES = 128
SUBLANES = 8
MXU_DIM = 256
VMEM_LIMIT = 56 * 1024 * 1024

SCAN_COLS = 512
FF_CHUNK = 256


def _dot(a, b):
    return jnp.dot(a, b, preferred_element_type=F32)


def _dot_nt(a, b):
    return lax.dot_general(a, b, (((1,), (1,)), ((), ())), preferred_element_type=F32)


def _rms(x, g):
    ms = jnp.mean(x * x, axis=-1, keepdims=True)
    return x * lax.rsqrt(ms + EPS) * g


def _head_rms(x, gmat, g):
    x2 = x * x
    hi = x2.astype(BF16)
    lo = (x2 - hi.astype(F32)).astype(BF16)
    ms = _dot(hi, gmat) + _dot(lo, gmat)
    return x * lax.rsqrt(ms + EPS) * g


def _const_spec(shape):
    nd = len(shape)
    return pl.BlockSpec(shape, lambda *_: (0,) * nd, pipeline_mode=pl.Buffered(1))


def _b_project(ub, bre_ref, bim_ref, bure, buim):
    n_kt = bre_ref.shape[0]
    kw = bre_ref.shape[1]
    nw = bre_ref.shape[2]
    for kt in range(n_kt):
        lhs = ub[:, kt * kw:(kt + 1) * kw]
        bure[:, kt * nw:(kt + 1) * nw] = _dot(lhs, bre_ref[kt])
        buim[:, kt * nw:(kt + 1) * nw] = _dot(lhs, bim_ref[kt])


def _scan_inplace(bure, buim, are_ref, aim_ref, n_seq, n_steps, init_fn, final_fn):
    nst = bure.shape[1]
    for cb in range(nst // SCAN_COLS):
        cs = slice(cb * SCAN_COLS, (cb + 1) * SCAN_COLS)
        ar = jnp.broadcast_to(are_ref[:, cs], (SUBLANES, SCAN_COLS))
        ai = jnp.broadcast_to(aim_ref[:, cs], (SUBLANES, SCAN_COLS))

        def group(g, _, cs=cs, ar=ar, ai=ai):
            r0 = pl.multiple_of(g * SUBLANES, SUBLANES)

            def step(k, carry):
                hr, hi = carry
                row = pl.multiple_of(k * n_seq + r0, SUBLANES)
                br = bure[pl.ds(row, SUBLANES), cs]
                bi = buim[pl.ds(row, SUBLANES), cs]
                nr = ar * hr - ai * hi + br
                ni = ar * hi + ai * hr + bi
                bure[pl.ds(row, SUBLANES), cs] = nr
                buim[pl.ds(row, SUBLANES), cs] = ni
                return nr, ni

            hr, hi = lax.fori_loop(0, n_steps, step, init_fn(r0, cs), unroll=min(n_steps, 4))
            final_fn(r0, cs, hr, hi)
            return 0

        if n_seq == SUBLANES:
            group(0, 0)
        else:
            lax.fori_loop(0, n_seq // SUBLANES, group, 0)


def _c_project_glu(x, u, bure, buim, cre_ref, ncim_ref, dskip_ref, wglu_ref, bglu_ref):
    d = x.shape[1]
    n_blk = cre_ref.shape[0]
    kw = cre_ref.shape[1]
    ys = []
    for m in range(n_blk):
        hr = bure[:, m * kw:(m + 1) * kw].astype(BF16)
        hi = buim[:, m * kw:(m + 1) * kw].astype(BF16)
        ys.append(_dot(hr, cre_ref[m]) + _dot(hi, ncim_ref[m]))
    y = jnp.concatenate(ys, axis=1) + dskip_ref[...] * u
    z = jax.nn.gelu(y).astype(BF16)
    gl = _dot(z, wglu_ref[...]) + bglu_ref[...]
    return x + gl[:, :d] * jax.nn.sigmoid(gl[:, d:])


def _ssm_prompt_kernel(x_ref, gmix_ref, are_ref, aim_ref, apre_ref, apim_ref, bre_ref, bim_ref, cre_ref,
                       ncim_ref, dskip_ref, wglu_ref, bglu_ref,
                       out_ref, sre_ref, sim_ref,
                       slab, xp, bure, buim, hin_re, hin_im, car_re, car_im, *, n_steps, pitch):
    t = pl.program_id(1)
    n_slab = slab.shape[0]
    rows = SUBLANES * n_steps

    @pl.when(t == 0)
    def _():
        car_re[...] = jnp.zeros_like(car_re)
        car_im[...] = jnp.zeros_like(car_im)

    for j in range(SUBLANES):
        for c in range(n_slab):
            slab[c, j * pitch:j * pitch + n_steps, :] = x_ref[j * n_steps:(j + 1) * n_steps,
                                                             c * LANES:(c + 1) * LANES]

    def gather(k, _):
        r0 = pl.multiple_of(k * SUBLANES, SUBLANES)
        for c in range(n_slab):
            xp[pl.ds(r0, SUBLANES), c * LANES:(c + 1) * LANES] = slab[c, pl.ds(k, SUBLANES, stride=pitch), :]
        return 0

    lax.fori_loop(0, n_steps, gather, 0)

    x = xp[...]
    u = _rms(x, gmix_ref[...])
    _b_project(u.astype(BF16), bre_ref, bim_ref, bure, buim)

    def zero_init(r0, cs):
        z = jnp.zeros((SUBLANES, SCAN_COLS), F32)
        return z, z

    _scan_inplace(bure, buim, are_ref, aim_ref, SUBLANES, n_steps, zero_init, lambda *_: None)

    last = (n_steps - 1) * SUBLANES
    apr = apre_ref[n_steps - 1:n_steps, :]
    api = apim_ref[n_steps - 1:n_steps, :]
    hr = car_re[...]
    hi = car_im[...]
    for j in range(SUBLANES):
        hin_re[j:j + 1, :] = hr
        hin_im[j:j + 1, :] = hi
        er = bure[last + j:last + j + 1, :]
        ei = buim[last + j:last + j + 1, :]
        hr, hi = apr * hr - api * hi + er, apr * hi + api * hr + ei
    car_re[...] = hr
    car_im[...] = hi
    sre_ref[...] = hr
    sim_ref[...] = hi

    nst = bure.shape[1]
    for cb in range(nst // SCAN_COLS):
        cs = slice(cb * SCAN_COLS, (cb + 1) * SCAN_COLS)
        gr = hin_re[:, cs]
        gi = hin_im[:, cs]

        def fix(k, _, cs=cs, gr=gr, gi=gi):
            row = pl.multiple_of(k * SUBLANES, SUBLANES)
            pr = apre_ref[pl.ds(k, 1), cs]
            pi = apim_ref[pl.ds(k, 1), cs]
            bure[pl.ds(row, SUBLANES), cs] = bure[pl.ds(row, SUBLANES), cs] + (pr * gr - pi * gi)
            buim[pl.ds(row, SUBLANES), cs] = buim[pl.ds(row, SUBLANES), cs] + (pr * gi + pi * gr)
            return 0

        lax.fori_loop(0, n_steps, fix, 0, unroll=4)

    xp[...] = _c_project_glu(x, u, bure, buim, cre_ref, ncim_ref, dskip_ref, wglu_ref, bglu_ref)

    def scatter(k, _):
        r0 = pl.multiple_of(k * SUBLANES, SUBLANES)
        for c in range(n_slab):
            slab[c, pl.ds(k, SUBLANES, stride=pitch), :] = xp[pl.ds(r0, SUBLANES), c * LANES:(c + 1) * LANES]
        return 0

    lax.fori_loop(0, n_steps, scatter, 0)

    for j in range(SUBLANES):
        for c in range(n_slab):
            out_ref[j * n_steps:(j + 1) * n_steps, c * LANES:(c + 1) * LANES] = slab[c, j * pitch:j * pitch + n_steps, :]


def _ssm_sample_kernel(x_ref, h0re_ref, h0im_ref, gmix_ref, are_ref, aim_ref, bre_ref, bim_ref, cre_ref,
                       ncim_ref, dskip_ref, wglu_ref, bglu_ref,
                       out_ref, sre_ref, sim_ref, bure, buim, *, n_seq, n_steps):
    x = x_ref[...]
    u = _rms(x, gmix_ref[...])
    _b_project(u.astype(BF16), bre_ref, bim_ref, bure, buim)

    def init(r0, cs):
        return h0re_ref[pl.ds(r0, SUBLANES), cs], h0im_ref[pl.ds(r0, SUBLANES), cs]

    def final(r0, cs, hr, hi):
        sre_ref[pl.ds(r0, SUBLANES), cs] = hr
        sim_ref[pl.ds(r0, SUBLANES), cs] = hi

    _scan_inplace(bure, buim, are_ref, aim_ref, n_seq, n_steps, init, final)
    out_ref[...] = _c_project_glu(x, u, bure, buim, cre_ref, ncim_ref, dskip_ref, wglu_ref, bglu_ref)


def _ssm_params(lam_re, lam_im, log_dt, b_re, b_im, c_re, c_im, n_pow):
    g, p = lam_re.shape
    lr = lam_re.astype(F32)
    li = lam_im.astype(F32)
    dt = jnp.exp(log_dt.astype(F32))[:, None]
    mag = jnp.exp(lr * dt)
    ang = li * dt
    ab_re = mag * jnp.cos(ang)
    ab_im = mag * jnp.sin(ang)
    den = lr * lr + li * li
    nr = ab_re - 1.0
    f_re = (nr * lr + ab_im * li) / den
    f_im = (ab_im * lr - nr * li) / den
    br = b_re.astype(F32)
    bi = b_im.astype(F32)
    bb_re = f_re[..., None] * br - f_im[..., None] * bi
    bb_im = f_re[..., None] * bi + f_im[..., None] * br

    c = bb_re.shape[2]
    gk = MXU_DIM // c
    eye_k = jnp.eye(gk, dtype=F32)

    def b_blocks(bb):
        bt = jnp.transpose(bb, (0, 2, 1)).reshape(g // gk, gk, c, p)
        return jnp.einsum('tgcp,gh->tgchp', bt, eye_k).reshape(g // gk, gk * c, gk * p).astype(BF16)

    gc = LANES // c
    eye_c = jnp.eye(gc, dtype=F32)

    def c_blocks(cc):
        ct = jnp.transpose(cc.astype(F32), (0, 2, 1)).reshape(g // gc, gc, p, c)
        return jnp.einsum('tgpc,gh->tgphc', ct, eye_c).reshape(g // gc, gc * p, gc * c).astype(BF16)

    pw_re = ab_re.reshape(1, g * p)
    pw_im = ab_im.reshape(1, g * p)
    while pw_re.shape[0] < n_pow:
        tr = pw_re[-1:]
        ti = pw_im[-1:]
        pw_re, pw_im = (jnp.concatenate([pw_re, pw_re * tr - pw_im * ti], axis=0),
                        jnp.concatenate([pw_im, pw_re * ti + pw_im * tr], axis=0))
    return dict(a_re=ab_re.reshape(1, g * p), a_im=ab_im.reshape(1, g * p),
                ap_re=pw_re[:n_pow], ap_im=pw_im[:n_pow],
                b_re=b_blocks(bb_re), b_im=b_blocks(bb_im),
                c_re=c_blocks(c_re), nc_im=c_blocks(-c_im))


def _ssm_prompt(x, sp, gmix, dskip, wglu, bglu, n_steps):
    n, l, d = x.shape
    nst = sp['a_re'].shape[1]
    rows = SUBLANES * n_steps
    pitch = n_steps + SUBLANES
    consts = [gmix, sp['a_re'], sp['a_im'], sp['ap_re'], sp['ap_im'], sp['b_re'], sp['b_im'], sp['c_re'],
              sp['nc_im'], dskip, wglu, bglu]
    out, sre, sim = pl.pallas_call(
        functools.partial(_ssm_prompt_kernel, n_steps=n_steps, pitch=pitch),
        grid=(n, l // rows),
        in_specs=[pl.BlockSpec((None, rows, d), lambda i, t: (i, t, 0))] + [_const_spec(c.shape) for c in consts],
        out_specs=[pl.BlockSpec((None, rows, d), lambda i, t: (i, t, 0)),
                   pl.BlockSpec((None, 1, nst), lambda i, t: (i, 0, 0)),
                   pl.BlockSpec((None, 1, nst), lambda i, t: (i, 0, 0))],
        out_shape=[jax.ShapeDtypeStruct((n, l, d), F32),
                   jax.ShapeDtypeStruct((n, 1, nst), F32),
                   jax.ShapeDtypeStruct((n, 1, nst), F32)],
        scratch_shapes=[pltpu.VMEM((d // LANES, SUBLANES * pitch, LANES), F32),
                        pltpu.VMEM((rows, d), F32),
                        pltpu.VMEM((rows, nst), F32),
                        pltpu.VMEM((rows, nst), F32),
                        pltpu.VMEM((SUBLANES, nst), F32),
                        pltpu.VMEM((SUBLANES, nst), F32),
                        pltpu.VMEM((1, nst), F32),
                        pltpu.VMEM((1, nst), F32)],
        compiler_params=pltpu.CompilerParams(dimension_semantics=("arbitrary", "arbitrary"),
                                             vmem_limit_bytes=VMEM_LIMIT),
        name="ssm_prompt",
    )(x, *consts)
    return out, sre[:, 0], sim[:, 0]


def _ssm_sample(x_tm, h0re, h0im, sp, gmix, dskip, wglu, bglu, n_seq, n_steps):
    rows, d = x_tm.shape
    nst = sp['a_re'].shape[1]
    args = [x_tm, h0re, h0im, gmix, sp['a_re'], sp['a_im'], sp['b_re'], sp['b_im'], sp['c_re'], sp['nc_im'],
            dskip, wglu, bglu]
    return pl.pallas_call(
        functools.partial(_ssm_sample_kernel, n_seq=n_seq, n_steps=n_steps),
        grid=(1,),
        in_specs=[_const_spec(a.shape) for a in args],
        out_specs=[pl.BlockSpec((rows, d), lambda i: (0, 0)), pl.BlockSpec((n_seq, nst), lambda i: (0, 0)),
                   pl.BlockSpec((n_seq, nst), lambda i: (0, 0))],
        out_shape=[jax.ShapeDtypeStruct((rows, d), F32),
                   jax.ShapeDtypeStruct((n_seq, nst), F32),
                   jax.ShapeDtypeStruct((n_seq, nst), F32)],
        scratch_shapes=[pltpu.VMEM((rows, nst), F32), pltpu.VMEM((rows, nst), F32)],
        compiler_params=pltpu.CompilerParams(dimension_semantics=("arbitrary",),
                                             vmem_limit_bytes=VMEM_LIMIT),
        name="ssm_sample",
    )(*args)


def _ffn_kernel(*refs, rows, head_rows, shift, has_attn):
    if has_attn:
        h_ref, o_ref, wo_ref = refs[:3]
        refs = refs[3:]
    else:
        h_ref = refs[0]
        refs = refs[1:]
    (p_ref, head_ref, gffn_ref, wup_ref, cw_ref, cb_ref, wdown_ref, gple_ref, wgate_ref, win_ref,
     out_ref, tail_ref, acc, extg, extv) = refs
    n_chunk = wdown_ref.shape[0]

    @pl.when(pl.program_id(1) == 0)
    def _():
        tail_ref[...] = head_ref[...]

    x = h_ref[...]
    if has_attn:
        x = x + _dot(o_ref[...], wo_ref[...])
    xn = _rms(x, gffn_ref[...]).astype(BF16)
    acc[...] = jnp.zeros_like(acc)

    def conv(up, ext, c):
        ext[0:head_rows, :] = tail_ref[c]
        ext[head_rows:head_rows + rows, :] = up
        w = cw_ref[c]
        t2 = ext[head_rows - 2 * shift:head_rows - 2 * shift + rows, :]
        t1 = ext[head_rows - shift:head_rows - shift + rows, :]
        tail_ref[c] = ext[rows:rows + head_rows, :]
        r = cb_ref[c] + t2 * w[0:1, :]
        r = r + t1 * w[1:2, :]
        return r + up * w[2:3, :]

    def chunk(j, _):
        cg = conv(_dot(xn, wup_ref[j]), extg, j)
        cv = conv(_dot(xn, wup_ref[n_chunk + j]), extv, n_chunk + j)
        act = (cg * jax.nn.sigmoid(cg) * cv).astype(BF16)
        acc[...] += _dot(act, wdown_ref[j])
        return 0

    lax.fori_loop(0, n_chunk, chunk, 0)

    h2 = x + acc[...]
    gate = jax.nn.sigmoid(_dot(_rms(h2, gple_ref[...]).astype(BF16), wgate_ref[...]))
    pe = _dot(p_ref[...].astype(BF16), win_ref[...])
    out_ref[...] = h2 + pe * gate


def _ffn_weights(g_ffn, w_up, conv_w, conv_b, w_down, g_ple, w_gate, w_in):
    d, ff2 = w_up.shape
    nc2 = ff2 // FF_CHUNK
    return dict(
        g_ffn=g_ffn.reshape(1, d),
        w_up=jnp.transpose(w_up.astype(BF16).reshape(d, nc2, FF_CHUNK), (1, 0, 2)),
        conv_w=jnp.transpose(conv_w.reshape(CONV_WIDTH, nc2, FF_CHUNK), (1, 0, 2)),
        conv_b=conv_b.reshape(nc2, 1, FF_CHUNK),
        w_down=w_down.astype(BF16).reshape(nc2 // 2, FF_CHUNK, d),
        g_ple=g_ple.reshape(1, d),
        w_gate=w_gate.astype(BF16),
        w_in=w_in.astype(BF16),
    )


def _ffn(h, p, head, fw, rows, n_seq_tiles, shift, attn=None):
    total, d = h.shape
    tiles = total // rows // n_seq_tiles
    nc2, head_rows = head.shape[1], head.shape[2]
    row_map = lambda i, t: (i * tiles + t, 0)
    consts = [fw['g_ffn'], fw['w_up'], fw['conv_w'], fw['conv_b'], fw['w_down'], fw['g_ple'], fw['w_gate'],
              fw['w_in']]
    args = [h]
    in_specs = [pl.BlockSpec((rows, d), row_map)]
    if attn is not None:
        o, wo = attn
        args += [o, wo]
        in_specs += [pl.BlockSpec((rows, o.shape[1]), row_map), _const_spec(wo.shape)]
    args += [p, head] + consts
    in_specs += [pl.BlockSpec((rows, p.shape[1]), row_map),
                 pl.BlockSpec((None, nc2, head_rows, FF_CHUNK), lambda i, t: (i, 0, 0, 0))]
    in_specs += [_const_spec(c.shape) for c in consts]
    out, tail = pl.pallas_call(
        functools.partial(_ffn_kernel, rows=rows, head_rows=head_rows, shift=shift, has_attn=attn is not None),
        grid=(n_seq_tiles, tiles),
        in_specs=in_specs,
        out_specs=[pl.BlockSpec((rows, d), row_map),
                   pl.BlockSpec((None, nc2, head_rows, FF_CHUNK), lambda i, t: (i, 0, 0, 0))],
        out_shape=[jax.ShapeDtypeStruct((total, d), F32),
                   jax.ShapeDtypeStruct(head.shape, F32)],
        scratch_shapes=[pltpu.VMEM((rows, d), F32),
                        pltpu.VMEM((head_rows + rows, FF_CHUNK), F32),
                        pltpu.VMEM((head_rows + rows, FF_CHUNK), F32)],
        compiler_params=pltpu.CompilerParams(dimension_semantics=("arbitrary", "arbitrary"),
                                             vmem_limit_bytes=VMEM_LIMIT),
        name="ffn_ple",
    )(*args)
    return out, tail


def _qkv_kernel(h_ref, gkv_ref, gmix_ref, wk_ref, wv_ref, wq_ref, knorm_ref, qnorm_ref, gk_ref, gq_ref,
                q_ref, k_ref, v_ref):
    x = h_ref[...]
    s = _rms(x, gkv_ref[...]).astype(BF16)
    k_ref[...] = _head_rms(_dot(s, wk_ref[...]), gk_ref[...], knorm_ref[...])
    v_ref[...] = _dot(s, wv_ref[...])
    xn = _rms(x, gmix_ref[...]).astype(BF16)
    q = _head_rms(_dot(xn, wq_ref[...]), gq_ref[...], qnorm_ref[...])
    q_ref[...] = (q * (HEAD_DIM ** -0.5)).astype(q_ref.dtype)


def _head_mean_matrix(n):
    idx = jnp.arange(n) // HEAD_DIM
    return (idx[:, None] == idx[None, :]).astype(BF16) * (1.0 / HEAD_DIM)


def _qkv(h, qw, rows):
    total, d = h.shape
    hk = qw['w_k'].shape[1]
    hq = qw['w_q'].shape[1]
    consts = [qw['g_kv'], qw['g_mix'], qw['w_k'], qw['w_v'], qw['w_q'], qw['k_norm'], qw['q_norm'],
              qw['gk'], qw['gq']]
    row_map = lambda i: (i, 0)
    return pl.pallas_call(
        _qkv_kernel,
        grid=(total // rows,),
        in_specs=[pl.BlockSpec((rows, d), row_map)] + [_const_spec(c.shape) for c in consts],
        out_specs=[pl.BlockSpec((rows, hq), row_map), pl.BlockSpec((rows, hk), row_map),
                   pl.BlockSpec((rows, hk), row_map)],
        out_shape=[jax.ShapeDtypeStruct((total, hq), BF16),
                   jax.ShapeDtypeStruct((total, hk), F32),
                   jax.ShapeDtypeStruct((total, hk), F32)],
        compiler_params=pltpu.CompilerParams(dimension_semantics=("arbitrary",),
                                             vmem_limit_bytes=VMEM_LIMIT),
        name="qkv",
    )(h, *consts)


def _t5_bucket(dist):
    max_exact = NUM_BUCKETS // 2
    df = jnp.maximum(dist, 1).astype(F32)
    large = max_exact + (jnp.log(df / max_exact) / math.log(MAX_DISTANCE / max_exact)
                         * (NUM_BUCKETS - max_exact)).astype(jnp.int32)
    return jnp.where(dist < max_exact, dist, jnp.minimum(large, NUM_BUCKETS - 1))


def _bias_table(rel_bias, lb, qb, lk):
    dist = lb + jnp.arange(qb)[:, None] - jnp.arange(lk)[None, :]
    bias = jnp.transpose(rel_bias[_t5_bucket(jnp.maximum(dist, 0))], (2, 0, 1)).astype(F32)
    ok = (dist >= 0) & (dist <= WINDOW)
    return jnp.where(ok[None], bias, -jnp.inf)


def _softmax_pv(s, sink, v):
    m = jnp.maximum(jnp.max(s, axis=-1, keepdims=True), sink)
    pr = jnp.exp(s - m)
    den = jnp.sum(pr, axis=-1, keepdims=True) + jnp.exp(sink - m)
    return _dot(pr.astype(BF16), v) / den


def _attn_prompt_kernel(sink_ref, q_ref, kp_ref, kc_ref, vp_ref, vc_ref, bias_ref, o_ref):
    b = pl.program_id(1)
    qb = q_ref.shape[0]
    n_heads = bias_ref.shape[0]
    rep = n_heads // N_KV_HEADS
    q = q_ref[...]
    kk = jnp.concatenate([kp_ref[...], kc_ref[...]], axis=0).astype(BF16)
    vv = jnp.concatenate([vp_ref[...], vc_ref[...]], axis=0).astype(BF16)
    col = lax.broadcasted_iota(jnp.int32, (1, 2 * qb), 1)
    key_ok = jnp.logical_or(col >= qb, b > 0)
    outs = []
    for g in range(N_KV_HEADS):
        kg = kk[:, g * HEAD_DIM:(g + 1) * HEAD_DIM]
        vg = vv[:, g * HEAD_DIM:(g + 1) * HEAD_DIM]
        for r in range(rep):
            h = g * rep + r
            s = _dot_nt(q[:, h * HEAD_DIM:(h + 1) * HEAD_DIM], kg) + bias_ref[h]
            s = jnp.where(key_ok, s, -jnp.inf)
            outs.append(_softmax_pv(s, sink_ref[h], vg))
    o_ref[...] = jnp.concatenate(outs, axis=1).astype(o_ref.dtype)


def _attn_prompt(q, k, v, bias, sinks, n, l):
    hq = q.shape[1]
    hk = k.shape[1]
    qb = WINDOW
    nb = l // qb
    cur = lambda i, b, *_: (i * nb + b, 0)
    prev = lambda i, b, *_: (i * nb + jnp.maximum(b - 1, 0), 0)
    return pl.pallas_call(
        _attn_prompt_kernel,
        grid_spec=pltpu.PrefetchScalarGridSpec(
            num_scalar_prefetch=1,
            grid=(n, nb),
            in_specs=[pl.BlockSpec((qb, hq), cur),
                      pl.BlockSpec((qb, hk), prev), pl.BlockSpec((qb, hk), cur),
                      pl.BlockSpec((qb, hk), prev), pl.BlockSpec((qb, hk), cur),
                      pl.BlockSpec(bias.shape, lambda i, b, *_: (0, 0, 0), pipeline_mode=pl.Buffered(1))],
            out_specs=pl.BlockSpec((qb, hq), cur)),
        out_shape=jax.ShapeDtypeStruct((n * l, hq), BF16),
        compiler_params=pltpu.CompilerParams(dimension_semantics=("arbitrary", "arbitrary"),
                                             vmem_limit_bytes=VMEM_LIMIT),
        name="attn_prompt",
    )(sinks, q, k, k, v, v, bias)


def _attn_sample_kernel(sink_ref, q_ref, k_ref, v_ref, bias_ref, o_ref):
    for s_i in range(q_ref.shape[0]):
        kk = k_ref[s_i].astype(BF16)
        vv = v_ref[s_i].astype(BF16)
        for g in range(N_KV_HEADS):
            kg = kk[:, g * HEAD_DIM:(g + 1) * HEAD_DIM]
            vg = vv[:, g * HEAD_DIM:(g + 1) * HEAD_DIM]
            s = _dot_nt(q_ref[s_i, g].astype(BF16), kg) + bias_ref[g]
            o_ref[s_i, g] = _softmax_pv(s, sink_ref[g], vg)


def _attn_sample(q4, k_ext, v_ext, bias4, sink4, block):
    ns, kvh, rr, hd = q4.shape
    lkp, hk = k_ext.shape[1], k_ext.shape[2]
    return pl.pallas_call(
        _attn_sample_kernel,
        grid=(ns // block,),
        in_specs=[_const_spec(sink4.shape),
                  pl.BlockSpec((block, kvh, rr, hd), lambda i: (i, 0, 0, 0)),
                  pl.BlockSpec((block, lkp, hk), lambda i: (i, 0, 0)),
                  pl.BlockSpec((block, lkp, hk), lambda i: (i, 0, 0)),
                  _const_spec(bias4.shape)],
        out_specs=pl.BlockSpec((block, kvh, rr, hd), lambda i: (i, 0, 0, 0)),
        out_shape=jax.ShapeDtypeStruct((ns, kvh, rr, hd), F32),
        compiler_params=pltpu.CompilerParams(dimension_semantics=("arbitrary",),
                                             vmem_limit_bytes=VMEM_LIMIT),
        name="attn_sample",
    )(sink4, q4, k_ext, v_ext, bias4)


def _chunk_major(a, nc2):
    lead = a.shape[:-2]
    r = a.shape[-2]
    a = a.reshape(lead + (r, nc2, FF_CHUNK))
    return jnp.swapaxes(a, -3, -2)


def _row_major(a):
    a = jnp.swapaxes(a, -3, -2)
    return a.reshape(a.shape[:-2] + (a.shape[-2] * a.shape[-1],))


def kernel(x_prompt, x_sample, state_ssm_re, state_ssm_im, state_ffn_conv, cache_k_win, cache_v_win, p_prompt, p_sample, g_mix, g_ffn, g_ple, ssm_lam_re, ssm_lam_im, ssm_log_dt, ssm_b_re, ssm_b_im, ssm_c_re, ssm_c_im, ssm_d, w_glu, b_glu, g_kv, w_k, w_v, k_norm, w_q, q_norm, sinks, w_o, rel_bias, w_up, conv_w, conv_b, w_down, w_ple_in, w_ple_gate):
    n, l, d = x_prompt.shape
    ns, ls, _ = x_sample.shape
    n_groups, n_state = ssm_lam_re.shape[1:]
    nst = n_groups * n_state
    ff2 = w_up.shape[2]
    nc2 = ff2 // FF_CHUNK
    hk = w_k.shape[1]
    hq = w_q.shape[2]
    n_heads = hq // HEAD_DIM
    rep = n_heads // N_KV_HEADS
    lb = cache_k_win.shape[1]

    ssm_steps = min(32, l // SUBLANES)
    ffn_rows = min(512, l)
    head_rows_p = SUBLANES
    tm = lambda a: jnp.swapaxes(a, 0, 1)

    sp = _ssm_params(ssm_lam_re[0], ssm_lam_im[0], ssm_log_dt[0], ssm_b_re[0], ssm_b_im[0], ssm_c_re[0],
                     ssm_c_im[0], ssm_steps)
    gmix0 = g_mix[0].reshape(1, d)
    dskip = ssm_d[0].reshape(1, d)
    wglu = w_glu[0].astype(BF16)
    bglu = b_glu[0].reshape(1, 2 * d)
    hp, sre_p, sim_p = _ssm_prompt(x_prompt, sp, gmix0, dskip, wglu, bglu, ssm_steps)
    hp = hp.reshape(n * l, d)
    xs_tm = tm(x_sample).reshape(ls * ns, d)
    hs, sre_s, sim_s = _ssm_sample(xs_tm, state_ssm_re[0].reshape(ns, nst), state_ssm_im[0].reshape(ns, nst),
                                   sp, gmix0, dskip, wglu, bglu, ns, ls)

    pp = p_prompt.reshape(p_prompt.shape[0], n * l, -1)
    ps = jnp.swapaxes(p_sample, 1, 2).reshape(p_sample.shape[0], ls * ns, -1)
    zero_head = jnp.zeros((n, nc2, head_rows_p, FF_CHUNK), F32)
    conv_s_tm = jnp.swapaxes(state_ffn_conv, 1, 2).reshape(state_ffn_conv.shape[0], 1, (CONV_WIDTH - 1) * ns, ff2)

    def ffn_layer(i, hp, hs, attn_p=None, attn_s=None):
        fw = _ffn_weights(g_ffn[i], w_up[i], conv_w[i], conv_b[i], w_down[i], g_ple[i], w_ple_gate[i],
                          w_ple_in[i])
        hp, tail_p = _ffn(hp, pp[i], zero_head, fw, ffn_rows, n, 1, attn_p)
        hs, tail_s = _ffn(hs, ps[i], _chunk_major(conv_s_tm[i], nc2), fw, ls * ns, 1, ns, attn_s)
        conv_p = _row_major(tail_p)[:, head_rows_p - (CONV_WIDTH - 1):, :]
        conv_s = jnp.swapaxes(_row_major(tail_s).reshape(CONV_WIDTH - 1, ns, ff2), 0, 1)
        return hp, hs, conv_p, conv_s

    hp, hs, conv_p0, conv_s0 = ffn_layer(0, hp, hs)

    qw = dict(g_kv=g_kv.reshape(1, d), g_mix=g_mix[1].reshape(1, d), w_k=w_k.astype(BF16), w_v=w_v.astype(BF16),
              w_q=w_q[0].astype(BF16), k_norm=jnp.tile(k_norm, hk // HEAD_DIM).reshape(1, hk),
              q_norm=jnp.tile(q_norm[0], n_heads).reshape(1, hq),
              gk=_head_mean_matrix(hk), gq=_head_mean_matrix(hq))
    q_p, k_p, v_p = _qkv(hp, qw, ffn_rows)
    q_s, k_s, v_s = _qkv(hs, qw, ls * ns)

    wo = w_o[0].astype(BF16)
    bias_p = _bias_table(rel_bias, WINDOW, WINDOW, 2 * WINDOW)
    o_p = _attn_prompt(q_p, k_p, v_p, bias_p, sinks[0].astype(F32), n, l)

    lkp = 2 * WINDOW
    k_new = tm(k_s.reshape(ls, ns, hk))
    v_new = tm(v_s.reshape(ls, ns, hk))
    kc = cache_k_win.reshape(ns, lb, hk)
    vc = cache_v_win.reshape(ns, lb, hk)
    pad = jnp.zeros((ns, lkp - lb - ls, hk), F32)
    k_ext = jnp.concatenate([kc, k_new, pad], axis=1)
    v_ext = jnp.concatenate([vc, v_new, pad], axis=1)
    bias_s = _bias_table(rel_bias, lb, ls, lb + ls)
    bias_s = jnp.pad(bias_s, ((0, 0), (0, 0), (0, lkp - lb - ls)), constant_values=-jnp.inf)
    bias4 = bias_s.reshape(N_KV_HEADS, rep * ls, lkp)
    sink4 = jnp.repeat(sinks[0].astype(F32), ls).reshape(N_KV_HEADS, rep * ls, 1)
    q4 = jnp.transpose(q_s.reshape(ls, ns, N_KV_HEADS, rep, HEAD_DIM), (1, 2, 3, 0, 4))
    q4 = q4.reshape(ns, N_KV_HEADS, rep * ls, HEAD_DIM)
    o4 = _attn_sample(q4, k_ext, v_ext, bias4, sink4, 8)
    o_s = jnp.transpose(o4.reshape(ns, N_KV_HEADS, rep, ls, HEAD_DIM), (3, 0, 1, 2, 4)).reshape(ls * ns, hq)
    o_s = o_s.astype(BF16)

    hp, hs, conv_p1, conv_s1 = ffn_layer(1, hp, hs, (o_p, wo), (o_s, wo))

    y_prompt = hp.reshape(n, l, d)
    y_sample = tm(hs.reshape(ls, ns, d))
    ssm_shape = (1, -1, n_groups, n_state)
    kvh_shape = (-1, lb, N_KV_HEADS, HEAD_DIM)
    k_win_p = k_p.reshape(n, l, hk)[:, l - WINDOW:].reshape(n, WINDOW, N_KV_HEADS, HEAD_DIM)
    v_win_p = v_p.reshape(n, l, hk)[:, l - WINDOW:].reshape(n, WINDOW, N_KV_HEADS, HEAD_DIM)
    k_win_s = jnp.concatenate([kc, k_new], axis=1)[:, ls:].reshape(kvh_shape)
    v_win_s = jnp.concatenate([vc, v_new], axis=1)[:, ls:].reshape(kvh_shape)
    return (y_prompt, y_sample,
            sre_p.reshape(ssm_shape), sim_p.reshape(ssm_shape),
            sre_s.reshape(ssm_shape), sim_s.reshape(ssm_shape),
            jnp.stack([conv_p0, conv_p1]), jnp.stack([conv_s0, conv_s1]),
            k_win_p, v_win_p, k_win_s, v_win_s)
```

```python
import functools
import math

import jax
import jax.numpy as jnp
from jax import lax
from jax.experimental import pallas as pl
from jax.experimental.pallas import tpu as pltpu

F32 = jnp.float32
BF16 = jnp.bfloat16

EPS = 1e-6
SSM_GROUP = 16
SSM_STATE = 64
HEAD_DIM = 64
N_KV_HEADS = 4
WINDOW = 128
NUM_BUCKETS = 32
MAX_DISTANCE = 128
CONV_WIDTH = 3

LANES = 128
SUBLANES = 8
MXU_DIM = 256
VMEM_LIMIT = 56 * 1024 * 1024

SCAN_COLS = 1024
FF_CHUNK = 256


def _dot(a, b):
    return jnp.dot(a, b, preferred_element_type=F32)


def _dot_nt(a, b):
    return lax.dot_general(a, b, (((1,), (1,)), ((), ())), preferred_element_type=F32)


def _rms(x, g):
    ms = jnp.mean(x * x, axis=-1, keepdims=True)
    return x * lax.rsqrt(ms + EPS) * g


def _head_rms(x, gmat, g):
    x2 = x * x
    hi = x2.astype(BF16)
    lo = (x2 - hi.astype(F32)).astype(BF16)
    ms = _dot(hi, gmat) + _dot(lo, gmat)
    return x * lax.rsqrt(ms + EPS) * g


def _const_spec(shape):
    nd = len(shape)
    return pl.BlockSpec(shape, lambda *_: (0,) * nd, pipeline_mode=pl.Buffered(1))


def _b_project(ub, bre_ref, bim_ref, bure, buim):
    n_kt = bre_ref.shape[0]
    kw = bre_ref.shape[1]
    nw = bre_ref.shape[2]
    for kt in range(n_kt):
        lhs = ub[:, kt * kw:(kt + 1) * kw]
        bure[:, kt * nw:(kt + 1) * nw] = _dot(lhs, bre_ref[kt])
        buim[:, kt * nw:(kt + 1) * nw] = _dot(lhs, bim_ref[kt])


def _scan(bure, buim, are_ref, aim_ref, n_seq, n_steps, init_fn, final_fn, store):
    nst = bure.shape[1]
    for cb in range(nst // SCAN_COLS):
        cs = slice(cb * SCAN_COLS, (cb + 1) * SCAN_COLS)
        ar = jnp.broadcast_to(are_ref[:, cs], (SUBLANES, SCAN_COLS))
        ai = jnp.broadcast_to(aim_ref[:, cs], (SUBLANES, SCAN_COLS))

        def group(g, _, cs=cs, ar=ar, ai=ai):
            r0 = pl.multiple_of(g * SUBLANES, SUBLANES)

            def step(k, carry):
                hr, hi = carry
                row = pl.multiple_of(k * n_seq + r0, SUBLANES)
                br = bure[pl.ds(row, SUBLANES), cs]
                bi = buim[pl.ds(row, SUBLANES), cs]
                nr = ar * hr - ai * hi + br
                ni = ar * hi + ai * hr + bi
                if store:
                    bure[pl.ds(row, SUBLANES), cs] = nr
                    buim[pl.ds(row, SUBLANES), cs] = ni
                return nr, ni

            hr, hi = lax.fori_loop(0, n_steps, step, init_fn(r0, cs), unroll=min(n_steps, 4))
            final_fn(r0, cs, hr, hi)
            return 0

        if n_seq == SUBLANES:
            group(0, 0)
        else:
            lax.fori_loop(0, n_seq // SUBLANES, group, 0)


def _c_project_glu(x, u, bure, buim, cre_ref, ncim_ref, dskip_ref, wglu_ref, bglu_ref):
    d = x.shape[1]
    n_blk = cre_ref.shape[0]
    kw = cre_ref.shape[1]
    ys = []
    for m in range(n_blk):
        hr = bure[:, m * kw:(m + 1) * kw].astype(BF16)
        hi = buim[:, m * kw:(m + 1) * kw].astype(BF16)
        ys.append(_dot(hr, cre_ref[m]) + _dot(hi, ncim_ref[m]))
    y = jnp.concatenate(ys, axis=1) + dskip_ref[...] * u
    z = jax.nn.gelu(y).astype(BF16)
    gl = _dot(z, wglu_ref[...]) + bglu_ref[...]
    return x + gl[:, :d] * jax.nn.sigmoid(gl[:, d:])


def _ssm_prompt_kernel(x_ref, gmix_ref, are_ref, aim_ref, apw_re_ref, apw_im_ref, bre_ref, bim_ref, cre_ref,
                       ncim_ref, dskip_ref, wglu_ref, bglu_ref,
                       out_ref, sre_ref, sim_ref,
                       slab, xp, bure, buim, hin_re, hin_im, car_re, car_im, *, n_steps, pitch):
    n_slab = slab.shape[0]

    @pl.when(pl.program_id(1) == 0)
    def _():
        car_re[...] = jnp.zeros_like(car_re)
        car_im[...] = jnp.zeros_like(car_im)

    for j in range(SUBLANES):
        for c in range(n_slab):
            slab[c, j * pitch:j * pitch + n_steps, :] = x_ref[j * n_steps:(j + 1) * n_steps,
                                                             c * LANES:(c + 1) * LANES]

    def gather(k, _):
        r0 = pl.multiple_of(k * SUBLANES, SUBLANES)
        for c in range(n_slab):
            xp[pl.ds(r0, SUBLANES), c * LANES:(c + 1) * LANES] = slab[c, pl.ds(k, SUBLANES, stride=pitch), :]
        return 0

    lax.fori_loop(0, n_steps, gather, 0, unroll=4)

    x = xp[...]
    u = _rms(x, gmix_ref[...])
    _b_project(u.astype(BF16), bre_ref, bim_ref, bure, buim)

    def zero_init(r0, cs):
        z = jnp.zeros((SUBLANES, SCAN_COLS), F32)
        return z, z

    def keep_end(r0, cs, hr, hi):
        hin_re[:, cs] = hr
        hin_im[:, cs] = hi

    _scan(bure, buim, are_ref, aim_ref, SUBLANES, n_steps, zero_init, keep_end, store=False)

    hr = car_re[...]
    hi = car_im[...]
    apr = apw_re_ref[...]
    api = apw_im_ref[...]
    for j in range(SUBLANES):
        er = hin_re[j:j + 1, :]
        ei = hin_im[j:j + 1, :]
        hin_re[j:j + 1, :] = hr
        hin_im[j:j + 1, :] = hi
        hr, hi = apr * hr - api * hi + er, apr * hi + api * hr + ei
    car_re[...] = hr
    car_im[...] = hi
    sre_ref[...] = hr
    sim_ref[...] = hi

    def true_init(r0, cs):
        return hin_re[:, cs], hin_im[:, cs]

    _scan(bure, buim, are_ref, aim_ref, SUBLANES, n_steps, true_init, lambda *_: None, store=True)

    xp[...] = _c_project_glu(x, u, bure, buim, cre_ref, ncim_ref, dskip_ref, wglu_ref, bglu_ref)

    def scatter(k, _):
        r0 = pl.multiple_of(k * SUBLANES, SUBLANES)
        for c in range(n_slab):
            slab[c, pl.ds(k, SUBLANES, stride=pitch), :] = xp[pl.ds(r0, SUBLANES), c * LANES:(c + 1) * LANES]
        return 0

    lax.fori_loop(0, n_steps, scatter, 0, unroll=4)

    for j in range(SUBLANES):
        for c in range(n_slab):
            out_ref[j * n_steps:(j + 1) * n_steps, c * LANES:(c + 1) * LANES] = slab[c, j * pitch:j * pitch + n_steps, :]


def _ssm_sample_kernel(x_ref, h0re_ref, h0im_ref, gmix_ref, are_ref, aim_ref, bre_ref, bim_ref, cre_ref,
                       ncim_ref, dskip_ref, wglu_ref, bglu_ref,
                       out_ref, sre_ref, sim_ref, bure, buim, *, n_seq, n_steps):
    x = x_ref[...]
    u = _rms(x, gmix_ref[...])
    _b_project(u.astype(BF16), bre_ref, bim_ref, bure, buim)

    def init(r0, cs):
        return h0re_ref[pl.ds(r0, SUBLANES), cs], h0im_ref[pl.ds(r0, SUBLANES), cs]

    def final(r0, cs, hr, hi):
        sre_ref[pl.ds(r0, SUBLANES), cs] = hr
        sim_ref[pl.ds(r0, SUBLANES), cs] = hi

    _scan(bure, buim, are_ref, aim_ref, n_seq, n_steps, init, final, store=True)
    out_ref[...] = _c_project_glu(x, u, bure, buim, cre_ref, ncim_ref, dskip_ref, wglu_ref, bglu_ref)


def _ssm_params(lam_re, lam_im, log_dt, b_re, b_im, c_re, c_im, n_pow):
    g, p = lam_re.shape
    lr = lam_re.astype(F32)
    li = lam_im.astype(F32)
    dt = jnp.exp(log_dt.astype(F32))[:, None]
    mag = jnp.exp(lr * dt)
    ang = li * dt
    ab_re = mag * jnp.cos(ang)
    ab_im = mag * jnp.sin(ang)
    den = lr * lr + li * li
    nr = ab_re - 1.0
    f_re = (nr * lr + ab_im * li) / den
    f_im = (ab_im * lr - nr * li) / den
    br = b_re.astype(F32)
    bi = b_im.astype(F32)
    bb_re = f_re[..., None] * br - f_im[..., None] * bi
    bb_im = f_re[..., None] * bi + f_im[..., None] * br

    c = bb_re.shape[2]
    gk = MXU_DIM // c
    eye_k = jnp.eye(gk, dtype=F32)

    def b_blocks(bb):
        bt = jnp.transpose(bb, (0, 2, 1)).reshape(g // gk, gk, c, p)
        return jnp.einsum('tgcp,gh->tgchp', bt, eye_k).reshape(g // gk, gk * c, gk * p).astype(BF16)

    gc = LANES // c
    eye_c = jnp.eye(gc, dtype=F32)

    def c_blocks(cc):
        ct = jnp.transpose(cc.astype(F32), (0, 2, 1)).reshape(g // gc, gc, p, c)
        return jnp.einsum('tgpc,gh->tgphc', ct, eye_c).reshape(g // gc, gc * p, gc * c).astype(BF16)

    pw_re = ab_re.reshape(1, g * p)
    pw_im = ab_im.reshape(1, g * p)
    for _ in range(n_pow.bit_length() - 1):
        pw_re, pw_im = pw_re * pw_re - pw_im * pw_im, 2.0 * pw_re * pw_im
    return dict(a_re=ab_re.reshape(1, g * p), a_im=ab_im.reshape(1, g * p),
                ap_re=pw_re, ap_im=pw_im,
                b_re=b_blocks(bb_re), b_im=b_blocks(bb_im),
                c_re=c_blocks(c_re), nc_im=c_blocks(-c_im))


def _ssm_prompt(x, sp, gmix, dskip, wglu, bglu, n_steps):
    n, l, d = x.shape
    nst = sp['a_re'].shape[1]
    rows = SUBLANES * n_steps
    pitch = n_steps + SUBLANES
    consts = [gmix, sp['a_re'], sp['a_im'], sp['ap_re'], sp['ap_im'], sp['b_re'], sp['b_im'], sp['c_re'],
              sp['nc_im'], dskip, wglu, bglu]
    out, sre, sim = pl.pallas_call(
        functools.partial(_ssm_prompt_kernel, n_steps=n_steps, pitch=pitch),
        grid=(n, l // rows),
        in_specs=[pl.BlockSpec((None, rows, d), lambda i, t: (i, t, 0))] + [_const_spec(c.shape) for c in consts],
        out_specs=[pl.BlockSpec((None, rows, d), lambda i, t: (i, t, 0)),
                   pl.BlockSpec((None, 1, nst), lambda i, t: (i, 0, 0)),
                   pl.BlockSpec((None, 1, nst), lambda i, t: (i, 0, 0))],
        out_shape=[jax.ShapeDtypeStruct((n, l, d), F32),
                   jax.ShapeDtypeStruct((n, 1, nst), F32),
                   jax.ShapeDtypeStruct((n, 1, nst), F32)],
        scratch_shapes=[pltpu.VMEM((d // LANES, SUBLANES * pitch, LANES), F32),
                        pltpu.VMEM((rows, d), F32),
                        pltpu.VMEM((rows, nst), F32),
                        pltpu.VMEM((rows, nst), F32),
                        pltpu.VMEM((SUBLANES, nst), F32),
                        pltpu.VMEM((SUBLANES, nst), F32),
                        pltpu.VMEM((1, nst), F32),
                        pltpu.VMEM((1, nst), F32)],
        compiler_params=pltpu.CompilerParams(dimension_semantics=("arbitrary", "arbitrary"),
                                             vmem_limit_bytes=VMEM_LIMIT),
        name="ssm_prompt",
    )(x, *consts)
    return out, sre[:, 0], sim[:, 0]


def _ssm_sample(x_tm, h0re, h0im, sp, gmix, dskip, wglu, bglu, n_seq, n_steps):
    rows, d = x_tm.shape
    nst = sp['a_re'].shape[1]
    args = [x_tm, h0re, h0im, gmix, sp['a_re'], sp['a_im'], sp['b_re'], sp['b_im'], sp['c_re'], sp['nc_im'],
            dskip, wglu, bglu]
    return pl.pallas_call(
        functools.partial(_ssm_sample_kernel, n_seq=n_seq, n_steps=n_steps),
        grid=(1,),
        in_specs=[_const_spec(a.shape) for a in args],
        out_specs=[pl.BlockSpec((rows, d), lambda i: (0, 0)), pl.BlockSpec((n_seq, nst), lambda i: (0, 0)),
                   pl.BlockSpec((n_seq, nst), lambda i: (0, 0))],
        out_shape=[jax.ShapeDtypeStruct((rows, d), F32),
                   jax.ShapeDtypeStruct((n_seq, nst), F32),
                   jax.ShapeDtypeStruct((n_seq, nst), F32)],
        scratch_shapes=[pltpu.VMEM((rows, nst), F32), pltpu.VMEM((rows, nst), F32)],
        compiler_params=pltpu.CompilerParams(dimension_semantics=("arbitrary",),
                                             vmem_limit_bytes=VMEM_LIMIT),
        name="ssm_sample",
    )(*args)


def _ffn_kernel(*refs, rows, head_rows, shift, has_attn):
    if has_attn:
        h_ref, o_ref, wo_ref = refs[:3]
        refs = refs[3:]
    else:
        h_ref = refs[0]
        refs = refs[1:]
    (p_ref, head_ref, gffn_ref, wup_ref, cw_ref, cb_ref, wdown_ref, gple_ref, wgate_ref, win_ref,
     out_ref, tail_ref, acc, xn_ref, ext_a, ext_b) = refs
    n_chunk = wdown_ref.shape[0]
    up0 = head_rows
    up1 = head_rows + rows

    @pl.when(pl.program_id(1) == 0)
    def _():
        tail_ref[...] = head_ref[...]

    x = h_ref[...]
    if has_attn:
        x = x + _dot(o_ref[...], wo_ref[...])
    xn_ref[...] = _rms(x, gffn_ref[...]).astype(BF16)
    acc[...] = jnp.zeros_like(acc)

    def up_project(c, ext):
        for half, cc in enumerate((c, n_chunk + c)):
            ext[half, 0:up0, :] = tail_ref[cc]
            ext[half, up0:up1, :] = _dot(xn_ref[...], wup_ref[cc])

    def conv(ext, half, cc):
        w = cw_ref[cc]
        t2 = ext[half, up0 - 2 * shift:up1 - 2 * shift, :]
        t1 = ext[half, up0 - shift:up1 - shift, :]
        tail_ref[cc] = ext[half, rows:up1, :]
        r = cb_ref[cc] + t2 * w[0:1, :]
        r = r + t1 * w[1:2, :]
        return r + ext[half, up0:up1, :] * w[2:3, :]

    def down_project(c, ext):
        cg = conv(ext, 0, c)
        cv = conv(ext, 1, n_chunk + c)
        act = (cg * jax.nn.sigmoid(cg) * cv).astype(BF16)
        acc[...] += _dot(act, wdown_ref[c])

    up_project(0, ext_a)

    def pair(i, _):
        c = 2 * i
        up_project(c + 1, ext_b)
        down_project(c, ext_a)
        up_project(c + 2, ext_a)
        down_project(c + 1, ext_b)
        return 0

    lax.fori_loop(0, (n_chunk - 1) // 2, pair, 0)
    if n_chunk % 2:
        down_project(n_chunk - 1, ext_a)
    else:
        up_project(n_chunk - 1, ext_b)
        down_project(n_chunk - 2, ext_a)
        down_project(n_chunk - 1, ext_b)

    h2 = x + acc[...]
    gate = jax.nn.sigmoid(_dot(_rms(h2, gple_ref[...]).astype(BF16), wgate_ref[...]))
    pe = _dot(p_ref[...].astype(BF16), win_ref[...])
    out_ref[...] = h2 + pe * gate


def _ffn_weights(g_ffn, w_up, conv_w, conv_b, w_down, g_ple, w_gate, w_in):
    d, ff2 = w_up.shape
    nc2 = ff2 // FF_CHUNK
    return dict(
        g_ffn=g_ffn.reshape(1, d),
        w_up=jnp.transpose(w_up.astype(BF16).reshape(d, nc2, FF_CHUNK), (1, 0, 2)),
        conv_w=jnp.transpose(conv_w.reshape(CONV_WIDTH, nc2, FF_CHUNK), (1, 0, 2)),
        conv_b=conv_b.reshape(nc2, 1, FF_CHUNK),
        w_down=w_down.astype(BF16).reshape(nc2 // 2, FF_CHUNK, d),
        g_ple=g_ple.reshape(1, d),
        w_gate=w_gate.astype(BF16),
        w_in=w_in.astype(BF16),
    )


def _ffn(h, p, head, fw, rows, n_seq_tiles, shift, attn=None):
    total, d = h.shape
    tiles = total // rows // n_seq_tiles
    nc2, head_rows = head.shape[1], head.shape[2]
    row_map = lambda i, t: (i * tiles + t, 0)
    consts = [fw['g_ffn'], fw['w_up'], fw['conv_w'], fw['conv_b'], fw['w_down'], fw['g_ple'], fw['w_gate'],
              fw['w_in']]
    args = [h]
    in_specs = [pl.BlockSpec((rows, d), row_map)]
    if attn is not None:
        o, wo = attn
        args += [o, wo]
        in_specs += [pl.BlockSpec((rows, o.shape[1]), row_map), _const_spec(wo.shape)]
    args += [p, head] + consts
    in_specs += [pl.BlockSpec((rows, p.shape[1]), row_map),
                 pl.BlockSpec((None, nc2, head_rows, FF_CHUNK), lambda i, t: (i, 0, 0, 0))]
    in_specs += [_const_spec(c.shape) for c in consts]
    out, tail = pl.pallas_call(
        functools.partial(_ffn_kernel, rows=rows, head_rows=head_rows, shift=shift, has_attn=attn is not None),
        grid=(n_seq_tiles, tiles),
        in_specs=in_specs,
        out_specs=[pl.BlockSpec((rows, d), row_map),
                   pl.BlockSpec((None, nc2, head_rows, FF_CHUNK), lambda i, t: (i, 0, 0, 0))],
        out_shape=[jax.ShapeDtypeStruct((total, d), F32),
                   jax.ShapeDtypeStruct(head.shape, F32)],
        scratch_shapes=[pltpu.VMEM((rows, d), F32),
                        pltpu.VMEM((rows, d), BF16),
                        pltpu.VMEM((2, head_rows + rows, FF_CHUNK), F32),
                        pltpu.VMEM((2, head_rows + rows, FF_CHUNK), F32)],
        compiler_params=pltpu.CompilerParams(dimension_semantics=("arbitrary", "arbitrary"),
                                             vmem_limit_bytes=VMEM_LIMIT),
        name="ffn_ple",
    )(*args)
    return out, tail


def _qkv_kernel(h_ref, gkv_ref, gmix_ref, wk_ref, wv_ref, wq_ref, knorm_ref, qnorm_ref, gk_ref, gq_ref,
                q_ref, k_ref, v_ref):
    x = h_ref[...]
    s = _rms(x, gkv_ref[...]).astype(BF16)
    k_ref[...] = _head_rms(_dot(s, wk_ref[...]), gk_ref[...], knorm_ref[...])
    v_ref[...] = _dot(s, wv_ref[...])
    xn = _rms(x, gmix_ref[...]).astype(BF16)
    q = _head_rms(_dot(xn, wq_ref[...]), gq_ref[...], qnorm_ref[...])
    q_ref[...] = (q * (HEAD_DIM ** -0.5)).astype(q_ref.dtype)


def _head_mean_matrix(n):
    idx = jnp.arange(n) // HEAD_DIM
    return (idx[:, None] == idx[None, :]).astype(BF16) * (1.0 / HEAD_DIM)


def _qkv(h, qw, rows):
    total, d = h.shape
    hk = qw['w_k'].shape[1]
    hq = qw['w_q'].shape[1]
    consts = [qw['g_kv'], qw['g_mix'], qw['w_k'], qw['w_v'], qw['w_q'], qw['k_norm'], qw['q_norm'],
              qw['gk'], qw['gq']]
    row_map = lambda i: (i, 0)
    return pl.pallas_call(
        _qkv_kernel,
        grid=(total // rows,),
        in_specs=[pl.BlockSpec((rows, d), row_map)] + [_const_spec(c.shape) for c in consts],
        out_specs=[pl.BlockSpec((rows, hq), row_map), pl.BlockSpec((rows, hk), row_map),
                   pl.BlockSpec((rows, hk), row_map)],
        out_shape=[jax.ShapeDtypeStruct((total, hq), BF16),
                   jax.ShapeDtypeStruct((total, hk), F32),
                   jax.ShapeDtypeStruct((total, hk), F32)],
        compiler_params=pltpu.CompilerParams(dimension_semantics=("arbitrary",),
                                             vmem_limit_bytes=VMEM_LIMIT),
        name="qkv",
    )(h, *consts)


def _t5_bucket(dist):
    max_exact = NUM_BUCKETS // 2
    df = jnp.maximum(dist, 1).astype(F32)
    large = max_exact + (jnp.log(df / max_exact) / math.log(MAX_DISTANCE / max_exact)
                         * (NUM_BUCKETS - max_exact)).astype(jnp.int32)
    return jnp.where(dist < max_exact, dist, jnp.minimum(large, NUM_BUCKETS - 1))


def _bias_table(rel_bias, lb, qb, lk):
    dist = lb + jnp.arange(qb)[:, None] - jnp.arange(lk)[None, :]
    onehot = jax.nn.one_hot(_t5_bucket(jnp.maximum(dist, 0)), NUM_BUCKETS, dtype=F32)
    bias = jnp.einsum('qkb,bh->hqk', onehot, rel_bias.astype(F32), precision=lax.Precision.HIGHEST)
    ok = (dist >= 0) & (dist <= WINDOW)
    return jnp.where(ok[None], bias, -jnp.inf)


def _softmax_pv(s, sink, v):
    m = jnp.maximum(jnp.max(s, axis=-1, keepdims=True), sink)
    pr = jnp.exp(s - m)
    den = jnp.sum(pr, axis=-1, keepdims=True) + jnp.exp(sink - m)
    return _dot(pr.astype(BF16), v) / den


def _attn_prompt_kernel(sink_ref, q_ref, kp_ref, kc_ref, vp_ref, vc_ref, bias_ref, o_ref):
    b = pl.program_id(1)
    qb = q_ref.shape[0]
    n_heads = bias_ref.shape[0]
    rep = n_heads // N_KV_HEADS
    q = q_ref[...]
    kk = jnp.concatenate([kp_ref[...], kc_ref[...]], axis=0).astype(BF16)
    vv = jnp.concatenate([vp_ref[...], vc_ref[...]], axis=0).astype(BF16)
    col = lax.broadcasted_iota(jnp.int32, (1, 2 * qb), 1)
    key_ok = jnp.logical_or(col >= qb, b > 0)
    outs = []
    for g in range(N_KV_HEADS):
        kg = kk[:, g * HEAD_DIM:(g + 1) * HEAD_DIM]
        vg = vv[:, g * HEAD_DIM:(g + 1) * HEAD_DIM]
        for r in range(rep):
            h = g * rep + r
            s = _dot_nt(q[:, h * HEAD_DIM:(h + 1) * HEAD_DIM], kg) + bias_ref[h]
            s = jnp.where(key_ok, s, -jnp.inf)
            outs.append(_softmax_pv(s, sink_ref[h], vg))
    o_ref[...] = jnp.concatenate(outs, axis=1).astype(o_ref.dtype)


def _attn_prompt(q, k, v, bias, sinks, n, l):
    hq = q.shape[1]
    hk = k.shape[1]
    qb = WINDOW
    nb = l // qb
    cur = lambda i, b, *_: (i * nb + b, 0)
    prev = lambda i, b, *_: (i * nb + jnp.maximum(b - 1, 0), 0)
    return pl.pallas_call(
        _attn_prompt_kernel,
        grid_spec=pltpu.PrefetchScalarGridSpec(
            num_scalar_prefetch=1,
            grid=(n, nb),
            in_specs=[pl.BlockSpec((qb, hq), cur),
                      pl.BlockSpec((qb, hk), prev), pl.BlockSpec((qb, hk), cur),
                      pl.BlockSpec((qb, hk), prev), pl.BlockSpec((qb, hk), cur),
                      pl.BlockSpec(bias.shape, lambda i, b, *_: (0, 0, 0), pipeline_mode=pl.Buffered(1))],
            out_specs=pl.BlockSpec((qb, hq), cur)),
        out_shape=jax.ShapeDtypeStruct((n * l, hq), BF16),
        compiler_params=pltpu.CompilerParams(dimension_semantics=("arbitrary", "arbitrary"),
                                             vmem_limit_bytes=VMEM_LIMIT),
        name="attn_prompt",
    )(sinks, q, k, k, v, v, bias)


def _attn_sample_kernel(sink_ref, q_ref, k_ref, v_ref, bias_ref, o_ref):
    for s_i in range(q_ref.shape[0]):
        kk = k_ref[s_i].astype(BF16)
        vv = v_ref[s_i].astype(BF16)
        for g in range(N_KV_HEADS):
            kg = kk[:, g * HEAD_DIM:(g + 1) * HEAD_DIM]
            vg = vv[:, g * HEAD_DIM:(g + 1) * HEAD_DIM]
            s = _dot_nt(q_ref[s_i, g].astype(BF16), kg) + bias_ref[g]
            o_ref[s_i, g] = _softmax_pv(s, sink_ref[g], vg)


def _attn_sample(q4, k_ext, v_ext, bias4, sink4, block):
    ns, kvh, rr, hd = q4.shape
    lkp, hk = k_ext.shape[1], k_ext.shape[2]
    return pl.pallas_call(
        _attn_sample_kernel,
        grid=(ns // block,),
        in_specs=[_const_spec(sink4.shape),
                  pl.BlockSpec((block, kvh, rr, hd), lambda i: (i, 0, 0, 0)),
                  pl.BlockSpec((block, lkp, hk), lambda i: (i, 0, 0)),
                  pl.BlockSpec((block, lkp, hk), lambda i: (i, 0, 0)),
                  _const_spec(bias4.shape)],
        out_specs=pl.BlockSpec((block, kvh, rr, hd), lambda i: (i, 0, 0, 0)),
        out_shape=jax.ShapeDtypeStruct((ns, kvh, rr, hd), F32),
        compiler_params=pltpu.CompilerParams(dimension_semantics=("arbitrary",),
                                             vmem_limit_bytes=VMEM_LIMIT),
        name="attn_sample",
    )(sink4, q4, k_ext, v_ext, bias4)


def _chunk_major(a, nc2):
    lead = a.shape[:-2]
    r = a.shape[-2]
    a = a.reshape(lead + (r, nc2, FF_CHUNK))
    return jnp.swapaxes(a, -3, -2)


def _row_major(a):
    a = jnp.swapaxes(a, -3, -2)
    return a.reshape(a.shape[:-2] + (a.shape[-2] * a.shape[-1],))


def kernel(x_prompt, x_sample, state_ssm_re, state_ssm_im, state_ffn_conv, cache_k_win, cache_v_win, p_prompt, p_sample, g_mix, g_ffn, g_ple, ssm_lam_re, ssm_lam_im, ssm_log_dt, ssm_b_re, ssm_b_im, ssm_c_re, ssm_c_im, ssm_d, w_glu, b_glu, g_kv, w_k, w_v, k_norm, w_q, q_norm, sinks, w_o, rel_bias, w_up, conv_w, conv_b, w_down, w_ple_in, w_ple_gate):
    n, l, d = x_prompt.shape
    ns, ls, _ = x_sample.shape
    n_groups, n_state = ssm_lam_re.shape[1:]
    nst = n_groups * n_state
    ff2 = w_up.shape[2]
    nc2 = ff2 // FF_CHUNK
    hk = w_k.shape[1]
    hq = w_q.shape[2]
    n_heads = hq // HEAD_DIM
    rep = n_heads // N_KV_HEADS
    lb = cache_k_win.shape[1]

    ssm_steps = min(32, l // SUBLANES)
    ffn_rows = min(512, l)
    head_rows_p = SUBLANES
    tm = lambda a: jnp.swapaxes(a, 0, 1)

    sp = _ssm_params(ssm_lam_re[0], ssm_lam_im[0], ssm_log_dt[0], ssm_b_re[0], ssm_b_im[0], ssm_c_re[0],
                     ssm_c_im[0], ssm_steps)
    gmix0 = g_mix[0].reshape(1, d)
    dskip = ssm_d[0].reshape(1, d)
    wglu = w_glu[0].astype(BF16)
    bglu = b_glu[0].reshape(1, 2 * d)
    hp, sre_p, sim_p = _ssm_prompt(x_prompt, sp, gmix0, dskip, wglu, bglu, ssm_steps)
    hp = hp.reshape(n * l, d)
    xs_tm = tm(x_sample).reshape(ls * ns, d)
    hs, sre_s, sim_s = _ssm_sample(xs_tm, state_ssm_re[0].reshape(ns, nst), state_ssm_im[0].reshape(ns, nst),
                                   sp, gmix0, dskip, wglu, bglu, ns, ls)

    pp = p_prompt.reshape(p_prompt.shape[0], n * l, -1)
    ps = jnp.swapaxes(p_sample, 1, 2).reshape(p_sample.shape[0], ls * ns, -1)
    zero_head = jnp.zeros((n, nc2, head_rows_p, FF_CHUNK), F32)
    conv_s_tm = jnp.swapaxes(state_ffn_conv, 1, 2).reshape(state_ffn_conv.shape[0], 1, (CONV_WIDTH - 1) * ns, ff2)

    def ffn_layer(i, hp, hs, attn_p=None, attn_s=None):
        fw = _ffn_weights(g_ffn[i], w_up[i], conv_w[i], conv_b[i], w_down[i], g_ple[i], w_ple_gate[i],
                          w_ple_in[i])
        hp, tail_p = _ffn(hp, pp[i], zero_head, fw, ffn_rows, n, 1, attn_p)
        hs, tail_s = _ffn(hs, ps[i], _chunk_major(conv_s_tm[i], nc2), fw, ls * ns, 1, ns, attn_s)
        conv_p = _row_major(tail_p)[:, head_rows_p - (CONV_WIDTH - 1):, :]
        conv_s = jnp.swapaxes(_row_major(tail_s).reshape(CONV_WIDTH - 1, ns, ff2), 0, 1)
        return hp, hs, conv_p, conv_s

    hp, hs, conv_p0, conv_s0 = ffn_layer(0, hp, hs)

    qw = dict(g_kv=g_kv.reshape(1, d), g_mix=g_mix[1].reshape(1, d), w_k=w_k.astype(BF16), w_v=w_v.astype(BF16),
              w_q=w_q[0].astype(BF16), k_norm=jnp.tile(k_norm, hk // HEAD_DIM).reshape(1, hk),
              q_norm=jnp.tile(q_norm[0], n_heads).reshape(1, hq),
              gk=_head_mean_matrix(hk), gq=_head_mean_matrix(hq))
    q_p, k_p, v_p = _qkv(hp, qw, ffn_rows)
    q_s, k_s, v_s = _qkv(hs, qw, ls * ns)

    wo = w_o[0].astype(BF16)
    bias_p = _bias_table(rel_bias, WINDOW, WINDOW, 2 * WINDOW)
    o_p = _attn_prompt(q_p, k_p, v_p, bias_p, sinks[0].astype(F32), n, l)

    lkp = 2 * WINDOW
    k_new = tm(k_s.reshape(ls, ns, hk))
    v_new = tm(v_s.reshape(ls, ns, hk))
    kc = cache_k_win.reshape(ns, lb, hk)
    vc = cache_v_win.reshape(ns, lb, hk)
    pad = jnp.zeros((ns, lkp - lb - ls, hk), F32)
    k_ext = jnp.concatenate([kc, k_new, pad], axis=1)
    v_ext = jnp.concatenate([vc, v_new, pad], axis=1)
    bias_s = _bias_table(rel_bias, lb, ls, lb + ls)
    bias_s = jnp.pad(bias_s, ((0, 0), (0, 0), (0, lkp - lb - ls)), constant_values=-jnp.inf)
    bias4 = bias_s.reshape(N_KV_HEADS, rep * ls, lkp)
    sink4 = jnp.repeat(sinks[0].astype(F32), ls).reshape(N_KV_HEADS, rep * ls, 1)
    q4 = jnp.transpose(q_s.reshape(ls, ns, N_KV_HEADS, rep, HEAD_DIM), (1, 2, 3, 0, 4))
    q4 = q4.reshape(ns, N_KV_HEADS, rep * ls, HEAD_DIM)
    o4 = _attn_sample(q4, k_ext, v_ext, bias4, sink4, 8)
    o_s = jnp.transpose(o4.reshape(ns, N_KV_HEADS, rep, ls, HEAD_DIM), (3, 0, 1, 2, 4)).reshape(ls * ns, hq)
    o_s = o_s.astype(BF16)

    hp, hs, conv_p1, conv_s1 = ffn_layer(1, hp, hs, (o_p, wo), (o_s, wo))

    y_prompt = hp.reshape(n, l, d)
    y_sample = tm(hs.reshape(ls, ns, d))
    ssm_shape = (1, -1, n_groups, n_state)
    kvh_shape = (-1, lb, N_KV_HEADS, HEAD_DIM)
    k_win_p = k_p.reshape(n, l, hk)[:, l - WINDOW:].reshape(n, WINDOW, N_KV_HEADS, HEAD_DIM)
    v_win_p = v_p.reshape(n, l, hk)[:, l - WINDOW:].reshape(n, WINDOW, N_KV_HEADS, HEAD_DIM)
    k_win_s = jnp.concatenate([kc, k_new], axis=1)[:, ls:].reshape(kvh_shape)
    v_win_s = jnp.concatenate([vc, v_new], axis=1)[:, ls:].reshape(kvh_shape)
    return (y_prompt, y_sample,
            sre_p.reshape(ssm_shape), sim_p.reshape(ssm_shape),
            sre_s.reshape(ssm_shape), sim_s.reshape(ssm_shape),
            jnp.stack([conv_p0, conv_p1]), jnp.stack([conv_s0, conv_s1]),
            k_win_p, v_win_p, k_win_s, v_win_s)
```

```python
import functools
import math

import jax
import jax.numpy as jnp
from jax import lax
from jax.experimental import pallas as pl
from jax.experimental.pallas import tpu as pltpu

F32 = jnp.float32
BF16 = jnp.bfloat16

EPS = 1e-6
SSM_GROUP = 16
SSM_STATE = 64
HEAD_DIM = 64
N_KV_HEADS = 4
WINDOW = 128
NUM_BUCKETS = 32
MAX_DISTANCE = 128
CONV_WIDTH = 3

LANES = 128
SUBLANES = 8
MXU_DIM = 256
VMEM_LIMIT = 56 * 1024 * 1024

SCAN_COLS = 1024
FF_CHUNK = 256


def _dot(a, b):
    return jnp.dot(a, b, preferred_element_type=F32)


def _dot_nt(a, b):
    return lax.dot_general(a, b, (((1,), (1,)), ((), ())), preferred_element_type=F32)


def _rms(x, g):
    ms = jnp.mean(x * x, axis=-1, keepdims=True)
    return x * lax.rsqrt(ms + EPS) * g


def _head_rms(x, pool, spread, g):
    ms = _dot((x * x).astype(BF16), pool)
    scale = lax.rsqrt(ms + EPS)
    hi = scale.astype(BF16)
    lo = (scale - hi.astype(F32)).astype(BF16)
    return x * (_dot(hi, spread) + _dot(lo, spread)) * g


def _const_spec(shape):
    nd = len(shape)
    return pl.BlockSpec(shape, lambda *_: (0,) * nd, pipeline_mode=pl.Buffered(1))


def _b_project(ub, bre_ref, bim_ref, bure, buim):
    n_kt = bre_ref.shape[0]
    kw = bre_ref.shape[1]
    nw = bre_ref.shape[2]
    for kt in range(n_kt):
        lhs = ub[:, kt * kw:(kt + 1) * kw]
        bure[:, kt * nw:(kt + 1) * nw] = _dot(lhs, bre_ref[kt])
        buim[:, kt * nw:(kt + 1) * nw] = _dot(lhs, bim_ref[kt])


def _scan(bure, buim, are_ref, aim_ref, n_seq, n_steps, init_fn, final_fn, store):
    nst = bure.shape[1]
    for cb in range(nst // SCAN_COLS):
        cs = slice(cb * SCAN_COLS, (cb + 1) * SCAN_COLS)
        ar = jnp.broadcast_to(are_ref[:, cs], (SUBLANES, SCAN_COLS))
        ai = jnp.broadcast_to(aim_ref[:, cs], (SUBLANES, SCAN_COLS))

        def group(g, _, cs=cs, ar=ar, ai=ai):
            r0 = pl.multiple_of(g * SUBLANES, SUBLANES)

            def step(k, carry):
                hr, hi = carry
                row = pl.multiple_of(k * n_seq + r0, SUBLANES)
                br = bure[pl.ds(row, SUBLANES), cs]
                bi = buim[pl.ds(row, SUBLANES), cs]
                nr = ar * hr - ai * hi + br
                ni = ar * hi + ai * hr + bi
                if store:
                    bure[pl.ds(row, SUBLANES), cs] = nr
                    buim[pl.ds(row, SUBLANES), cs] = ni
                return nr, ni

            hr, hi = lax.fori_loop(0, n_steps, step, init_fn(r0, cs), unroll=min(n_steps, 4))
            final_fn(r0, cs, hr, hi)
            return 0

        if n_seq == SUBLANES:
            group(0, 0)
        else:
            lax.fori_loop(0, n_seq // SUBLANES, group, 0)


def _c_project_glu(x, u, bure, buim, cre_ref, ncim_ref, dskip_ref, wglu_ref, bglu_ref):
    d = x.shape[1]
    n_blk = cre_ref.shape[0]
    kw = cre_ref.shape[1]
    ys = []
    for m in range(n_blk):
        hr = bure[:, m * kw:(m + 1) * kw].astype(BF16)
        hi = buim[:, m * kw:(m + 1) * kw].astype(BF16)
        ys.append(_dot(hr, cre_ref[m]) + _dot(hi, ncim_ref[m]))
    y = jnp.concatenate(ys, axis=1) + dskip_ref[...] * u
    z = jax.nn.gelu(y).astype(BF16)
    gl = _dot(z, wglu_ref[...]) + bglu_ref[...]
    return x + gl[:, :d] * jax.nn.sigmoid(gl[:, d:])


def _ssm_prompt_kernel(x_ref, gmix_ref, are_ref, aim_ref, apw_re_ref, apw_im_ref, bre_ref, bim_ref, cre_ref,
                       ncim_ref, dskip_ref, wglu_ref, bglu_ref,
                       out_ref, sre_ref, sim_ref,
                       slab, xp, bure, buim, hin_re, hin_im, car_re, car_im, *, n_steps, pitch):
    n_slab = slab.shape[0]

    @pl.when(pl.program_id(1) == 0)
    def _():
        car_re[...] = jnp.zeros_like(car_re)
        car_im[...] = jnp.zeros_like(car_im)

    for j in range(SUBLANES):
        for c in range(n_slab):
            slab[c, j * pitch:j * pitch + n_steps, :] = x_ref[j * n_steps:(j + 1) * n_steps,
                                                             c * LANES:(c + 1) * LANES]

    def gather(k, _):
        r0 = pl.multiple_of(k * SUBLANES, SUBLANES)
        for c in range(n_slab):
            xp[pl.ds(r0, SUBLANES), c * LANES:(c + 1) * LANES] = slab[c, pl.ds(k, SUBLANES, stride=pitch), :]
        return 0

    lax.fori_loop(0, n_steps, gather, 0, unroll=4)

    x = xp[...]
    u = _rms(x, gmix_ref[...])
    _b_project(u.astype(BF16), bre_ref, bim_ref, bure, buim)

    def zero_init(r0, cs):
        z = jnp.zeros((SUBLANES, SCAN_COLS), F32)
        return z, z

    def keep_end(r0, cs, hr, hi):
        hin_re[:, cs] = hr
        hin_im[:, cs] = hi

    _scan(bure, buim, are_ref, aim_ref, SUBLANES, n_steps, zero_init, keep_end, store=False)

    hr = car_re[...]
    hi = car_im[...]
    apr = apw_re_ref[...]
    api = apw_im_ref[...]
    for j in range(SUBLANES):
        er = hin_re[j:j + 1, :]
        ei = hin_im[j:j + 1, :]
        hin_re[j:j + 1, :] = hr
        hin_im[j:j + 1, :] = hi
        hr, hi = apr * hr - api * hi + er, apr * hi + api * hr + ei
    car_re[...] = hr
    car_im[...] = hi
    sre_ref[...] = hr
    sim_ref[...] = hi

    def true_init(r0, cs):
        return hin_re[:, cs], hin_im[:, cs]

    _scan(bure, buim, are_ref, aim_ref, SUBLANES, n_steps, true_init, lambda *_: None, store=True)

    xp[...] = _c_project_glu(x, u, bure, buim, cre_ref, ncim_ref, dskip_ref, wglu_ref, bglu_ref)

    def scatter(k, _):
        r0 = pl.multiple_of(k * SUBLANES, SUBLANES)
        for c in range(n_slab):
            slab[c, pl.ds(k, SUBLANES, stride=pitch), :] = xp[pl.ds(r0, SUBLANES), c * LANES:(c + 1) * LANES]
        return 0

    lax.fori_loop(0, n_steps, scatter, 0, unroll=4)

    for j in range(SUBLANES):
        for c in range(n_slab):
            out_ref[j * n_steps:(j + 1) * n_steps, c * LANES:(c + 1) * LANES] = slab[c, j * pitch:j * pitch + n_steps, :]


def _ssm_sample_kernel(x_ref, h0re_ref, h0im_ref, gmix_ref, are_ref, aim_ref, bre_ref, bim_ref, cre_ref,
                       ncim_ref, dskip_ref, wglu_ref, bglu_ref,
                       out_ref, sre_ref, sim_ref, bure, buim, *, n_seq, n_steps):
    x = x_ref[...]
    u = _rms(x, gmix_ref[...])
    _b_project(u.astype(BF16), bre_ref, bim_ref, bure, buim)

    def init(r0, cs):
        return h0re_ref[pl.ds(r0, SUBLANES), cs], h0im_ref[pl.ds(r0, SUBLANES), cs]

    def final(r0, cs, hr, hi):
        sre_ref[pl.ds(r0, SUBLANES), cs] = hr
        sim_ref[pl.ds(r0, SUBLANES), cs] = hi

    _scan(bure, buim, are_ref, aim_ref, n_seq, n_steps, init, final, store=True)
    out_ref[...] = _c_project_glu(x, u, bure, buim, cre_ref, ncim_ref, dskip_ref, wglu_ref, bglu_ref)


def _ssm_params(lam_re, lam_im, log_dt, b_re, b_im, c_re, c_im, n_pow):
    g, p = lam_re.shape
    lr = lam_re.astype(F32)
    li = lam_im.astype(F32)
    dt = jnp.exp(log_dt.astype(F32))[:, None]
    mag = jnp.exp(lr * dt)
    ang = li * dt
    ab_re = mag * jnp.cos(ang)
    ab_im = mag * jnp.sin(ang)
    den = lr * lr + li * li
    nr = ab_re - 1.0
    f_re = (nr * lr + ab_im * li) / den
    f_im = (ab_im * lr - nr * li) / den
    br = b_re.astype(F32)
    bi = b_im.astype(F32)
    bb_re = f_re[..., None] * br - f_im[..., None] * bi
    bb_im = f_re[..., None] * bi + f_im[..., None] * br

    c = bb_re.shape[2]
    gk = MXU_DIM // c
    eye_k = jnp.eye(gk, dtype=F32)

    def b_blocks(bb):
        bt = jnp.transpose(bb, (0, 2, 1)).reshape(g // gk, gk, c, p)
        return jnp.einsum('tgcp,gh->tgchp', bt, eye_k).reshape(g // gk, gk * c, gk * p).astype(BF16)

    gc = LANES // c
    eye_c = jnp.eye(gc, dtype=F32)

    def c_blocks(cc):
        ct = jnp.transpose(cc.astype(F32), (0, 2, 1)).reshape(g // gc, gc, p, c)
        return jnp.einsum('tgpc,gh->tgphc', ct, eye_c).reshape(g // gc, gc * p, gc * c).astype(BF16)

    pw_re = ab_re.reshape(1, g * p)
    pw_im = ab_im.reshape(1, g * p)
    for _ in range(n_pow.bit_length() - 1):
        pw_re, pw_im = pw_re * pw_re - pw_im * pw_im, 2.0 * pw_re * pw_im
    return dict(a_re=ab_re.reshape(1, g * p), a_im=ab_im.reshape(1, g * p),
                ap_re=pw_re, ap_im=pw_im,
                b_re=b_blocks(bb_re), b_im=b_blocks(bb_im),
                c_re=c_blocks(c_re), nc_im=c_blocks(-c_im))


def _ssm_prompt(x, sp, gmix, dskip, wglu, bglu, n_steps):
    n, l, d = x.shape
    nst = sp['a_re'].shape[1]
    rows = SUBLANES * n_steps
    pitch = n_steps + SUBLANES
    consts = [gmix, sp['a_re'], sp['a_im'], sp['ap_re'], sp['ap_im'], sp['b_re'], sp['b_im'], sp['c_re'],
              sp['nc_im'], dskip, wglu, bglu]
    out, sre, sim = pl.pallas_call(
        functools.partial(_ssm_prompt_kernel, n_steps=n_steps, pitch=pitch),
        grid=(n, l // rows),
        in_specs=[pl.BlockSpec((None, rows, d), lambda i, t: (i, t, 0))] + [_const_spec(c.shape) for c in consts],
        out_specs=[pl.BlockSpec((None, rows, d), lambda i, t: (i, t, 0)),
                   pl.BlockSpec((None, 1, nst), lambda i, t: (i, 0, 0)),
                   pl.BlockSpec((None, 1, nst), lambda i, t: (i, 0, 0))],
        out_shape=[jax.ShapeDtypeStruct((n, l, d), F32),
                   jax.ShapeDtypeStruct((n, 1, nst), F32),
                   jax.ShapeDtypeStruct((n, 1, nst), F32)],
        scratch_shapes=[pltpu.VMEM((d // LANES, SUBLANES * pitch, LANES), F32),
                        pltpu.VMEM((rows, d), F32),
                        pltpu.VMEM((rows, nst), F32),
                        pltpu.VMEM((rows, nst), F32),
                        pltpu.VMEM((SUBLANES, nst), F32),
                        pltpu.VMEM((SUBLANES, nst), F32),
                        pltpu.VMEM((1, nst), F32),
                        pltpu.VMEM((1, nst), F32)],
        compiler_params=pltpu.CompilerParams(dimension_semantics=("arbitrary", "arbitrary"),
                                             vmem_limit_bytes=VMEM_LIMIT),
        name="ssm_prompt",
    )(x, *consts)
    return out, sre[:, 0], sim[:, 0]


def _ssm_sample(x_tm, h0re, h0im, sp, gmix, dskip, wglu, bglu, n_seq, n_steps):
    rows, d = x_tm.shape
    nst = sp['a_re'].shape[1]
    args = [x_tm, h0re, h0im, gmix, sp['a_re'], sp['a_im'], sp['b_re'], sp['b_im'], sp['c_re'], sp['nc_im'],
            dskip, wglu, bglu]
    return pl.pallas_call(
        functools.partial(_ssm_sample_kernel, n_seq=n_seq, n_steps=n_steps),
        grid=(1,),
        in_specs=[_const_spec(a.shape) for a in args],
        out_specs=[pl.BlockSpec((rows, d), lambda i: (0, 0)), pl.BlockSpec((n_seq, nst), lambda i: (0, 0)),
                   pl.BlockSpec((n_seq, nst), lambda i: (0, 0))],
        out_shape=[jax.ShapeDtypeStruct((rows, d), F32),
                   jax.ShapeDtypeStruct((n_seq, nst), F32),
                   jax.ShapeDtypeStruct((n_seq, nst), F32)],
        scratch_shapes=[pltpu.VMEM((rows, nst), F32), pltpu.VMEM((rows, nst), F32)],
        compiler_params=pltpu.CompilerParams(dimension_semantics=("arbitrary",),
                                             vmem_limit_bytes=VMEM_LIMIT),
        name="ssm_sample",
    )(*args)


def _ffn_kernel(*refs, rows, head_rows, shift, has_attn):
    if has_attn:
        h_ref, o_ref, wo_ref = refs[:3]
        refs = refs[3:]
    else:
        h_ref = refs[0]
        refs = refs[1:]
    (p_ref, head_ref, gffn_ref, wup_ref, cw_ref, cb_ref, wdown_ref, gple_ref, wgate_ref, win_ref,
     out_ref, tail_ref, acc, xn_ref, ext_a, ext_b) = refs
    n_chunk = wdown_ref.shape[0]
    up0 = head_rows
    up1 = head_rows + rows

    @pl.when(pl.program_id(1) == 0)
    def _():
        tail_ref[...] = head_ref[...]

    x = h_ref[...]
    if has_attn:
        x = x + _dot(o_ref[...], wo_ref[...])
    xn_ref[...] = _rms(x, gffn_ref[...]).astype(BF16)
    acc[...] = jnp.zeros_like(acc)

    n_slab = ext_a.shape[1]

    def up_project(c, ext):
        for half, cc in enumerate((c, n_chunk + c)):
            up = _dot(xn_ref[...], wup_ref[cc])
            for s in range(n_slab):
                ext[half, s, 0:up0, :] = tail_ref[cc, :, s * LANES:(s + 1) * LANES]
                ext[half, s, up0:up1, :] = up[:, s * LANES:(s + 1) * LANES]

    def conv(ext, half, cc):
        w = cw_ref[cc]
        b = cb_ref[cc]
        parts = []
        for s in range(n_slab):
            ls = slice(s * LANES, (s + 1) * LANES)
            t2 = ext[half, s, up0 - 2 * shift:up1 - 2 * shift, :]
            t1 = ext[half, s, up0 - shift:up1 - shift, :]
            tail_ref[cc, :, ls] = ext[half, s, rows:up1, :]
            r = b[:, ls] + t2 * w[0:1, ls]
            r = r + t1 * w[1:2, ls]
            parts.append(r + ext[half, s, up0:up1, :] * w[2:3, ls])
        return jnp.concatenate(parts, axis=1)

    def down_project(c, ext):
        cg = conv(ext, 0, c)
        cv = conv(ext, 1, n_chunk + c)
        act = (cg * jax.nn.sigmoid(cg) * cv).astype(BF16)
        acc[...] += _dot(act, wdown_ref[c])

    up_project(0, ext_a)

    def pair(i, _):
        c = 2 * i
        up_project(c + 1, ext_b)
        down_project(c, ext_a)
        up_project(c + 2, ext_a)
        down_project(c + 1, ext_b)
        return 0

    lax.fori_loop(0, (n_chunk - 1) // 2, pair, 0)
    if n_chunk % 2:
        down_project(n_chunk - 1, ext_a)
    else:
        up_project(n_chunk - 1, ext_b)
        down_project(n_chunk - 2, ext_a)
        down_project(n_chunk - 1, ext_b)

    h2 = x + acc[...]
    gate = jax.nn.sigmoid(_dot(_rms(h2, gple_ref[...]).astype(BF16), wgate_ref[...]))
    pe = _dot(p_ref[...].astype(BF16), win_ref[...])
    out_ref[...] = h2 + pe * gate


def _ffn_weights(g_ffn, w_up, conv_w, conv_b, w_down, g_ple, w_gate, w_in):
    d, ff2 = w_up.shape
    nc2 = ff2 // FF_CHUNK
    return dict(
        g_ffn=g_ffn.reshape(1, d),
        w_up=jnp.transpose(w_up.astype(BF16).reshape(d, nc2, FF_CHUNK), (1, 0, 2)),
        conv_w=jnp.transpose(conv_w.reshape(CONV_WIDTH, nc2, FF_CHUNK), (1, 0, 2)),
        conv_b=conv_b.reshape(nc2, 1, FF_CHUNK),
        w_down=w_down.astype(BF16).reshape(nc2 // 2, FF_CHUNK, d),
        g_ple=g_ple.reshape(1, d),
        w_gate=w_gate.astype(BF16),
        w_in=w_in.astype(BF16),
    )


def _ffn(h, p, layer, head, fw, rows, n_seq_tiles, shift, attn=None):
    total, d = h.shape
    tiles = total // rows // n_seq_tiles
    nc2, head_rows = head.shape[1], head.shape[2]
    row_map = lambda i, t: (i * tiles + t, 0)
    p_map = lambda i, t: (layer * (total // rows) + i * tiles + t, 0)
    consts = [fw['g_ffn'], fw['w_up'], fw['conv_w'], fw['conv_b'], fw['w_down'], fw['g_ple'], fw['w_gate'],
              fw['w_in']]
    args = [h]
    in_specs = [pl.BlockSpec((rows, d), row_map)]
    if attn is not None:
        o, wo = attn
        args += [o, wo]
        in_specs += [pl.BlockSpec((rows, o.shape[1]), row_map), _const_spec(wo.shape)]
    args += [p, head] + consts
    in_specs += [pl.BlockSpec((rows, p.shape[1]), p_map),
                 pl.BlockSpec((None, nc2, head_rows, FF_CHUNK), lambda i, t: (i, 0, 0, 0))]
    in_specs += [_const_spec(c.shape) for c in consts]
    out, tail = pl.pallas_call(
        functools.partial(_ffn_kernel, rows=rows, head_rows=head_rows, shift=shift, has_attn=attn is not None),
        grid=(n_seq_tiles, tiles),
        in_specs=in_specs,
        out_specs=[pl.BlockSpec((rows, d), row_map),
                   pl.BlockSpec((None, nc2, head_rows, FF_CHUNK), lambda i, t: (i, 0, 0, 0))],
        out_shape=[jax.ShapeDtypeStruct((total, d), F32),
                   jax.ShapeDtypeStruct(head.shape, F32)],
        scratch_shapes=[pltpu.VMEM((rows, d), F32),
                        pltpu.VMEM((rows, d), BF16),
                        pltpu.VMEM((2, FF_CHUNK // LANES, head_rows + rows, LANES), F32),
                        pltpu.VMEM((2, FF_CHUNK // LANES, head_rows + rows, LANES), F32)],
        compiler_params=pltpu.CompilerParams(dimension_semantics=("arbitrary", "arbitrary"),
                                             vmem_limit_bytes=VMEM_LIMIT),
        name="ffn_ple",
    )(*args)
    return out, tail


def _qkv_kernel(h_ref, gkv_ref, gmix_ref, wk_ref, wv_ref, wq_ref, knorm_ref, qnorm_ref, kpool_ref, kspread_ref,
                qpool_ref, qspread_ref, q_ref, k_ref, v_ref):
    x = h_ref[...]
    s = _rms(x, gkv_ref[...]).astype(BF16)
    k_ref[...] = _head_rms(_dot(s, wk_ref[...]), kpool_ref[...], kspread_ref[...], knorm_ref[...])
    v_ref[...] = _dot(s, wv_ref[...])
    xn = _rms(x, gmix_ref[...]).astype(BF16)
    q = _head_rms(_dot(xn, wq_ref[...]), qpool_ref[...], qspread_ref[...], qnorm_ref[...])
    q_ref[...] = (q * (HEAD_DIM ** -0.5)).astype(q_ref.dtype)


def _head_pool_matrices(n):
    member = (jnp.arange(n)[:, None] // HEAD_DIM) == jnp.arange(LANES)[None, :]
    return member.astype(BF16) * (1.0 / HEAD_DIM), member.T.astype(BF16)


def _qkv(h, qw, rows):
    total, d = h.shape
    hk = qw['w_k'].shape[1]
    hq = qw['w_q'].shape[1]
    consts = [qw['g_kv'], qw['g_mix'], qw['w_k'], qw['w_v'], qw['w_q'], qw['k_norm'], qw['q_norm'],
              *qw['k_pool'], *qw['q_pool']]
    row_map = lambda i: (i, 0)
    return pl.pallas_call(
        _qkv_kernel,
        grid=(total // rows,),
        in_specs=[pl.BlockSpec((rows, d), row_map)] + [_const_spec(c.shape) for c in consts],
        out_specs=[pl.BlockSpec((rows, hq), row_map), pl.BlockSpec((rows, hk), row_map),
                   pl.BlockSpec((rows, hk), row_map)],
        out_shape=[jax.ShapeDtypeStruct((total, hq), BF16),
                   jax.ShapeDtypeStruct((total, hk), F32),
                   jax.ShapeDtypeStruct((total, hk), F32)],
        compiler_params=pltpu.CompilerParams(dimension_semantics=("arbitrary",),
                                             vmem_limit_bytes=VMEM_LIMIT),
        name="qkv",
    )(h, *consts)


def _t5_bucket(dist):
    max_exact = NUM_BUCKETS // 2
    df = jnp.maximum(dist, 1).astype(F32)
    large = max_exact + (jnp.log(df / max_exact) / math.log(MAX_DISTANCE / max_exact)
                         * (NUM_BUCKETS - max_exact)).astype(jnp.int32)
    return jnp.where(dist < max_exact, dist, jnp.minimum(large, NUM_BUCKETS - 1))


def _bias_table(rel_bias, lb, qb, lk):
    dist = lb + jnp.arange(qb)[:, None] - jnp.arange(lk)[None, :]
    onehot = jax.nn.one_hot(_t5_bucket(jnp.maximum(dist, 0)), NUM_BUCKETS, dtype=F32)
    bias = jnp.einsum('qkb,bh->hqk', onehot, rel_bias.astype(F32), precision=lax.Precision.HIGHEST)
    ok = (dist >= 0) & (dist <= WINDOW)
    return jnp.where(ok[None], bias, -jnp.inf)


def _softmax_pv(s, sink, v):
    m = jnp.maximum(jnp.max(s, axis=-1, keepdims=True), sink)
    pr = jnp.exp(s - m)
    den = jnp.sum(pr, axis=-1, keepdims=True) + jnp.exp(sink - m)
    return _dot(pr.astype(BF16), v) / den


def _attn_prompt_kernel(sink_ref, q_ref, kp_ref, kc_ref, vp_ref, vc_ref, bias_ref, o_ref):
    b = pl.program_id(1)
    qb = q_ref.shape[0]
    n_heads = bias_ref.shape[0]
    rep = n_heads // N_KV_HEADS
    q = q_ref[...]
    kk = jnp.concatenate([kp_ref[...], kc_ref[...]], axis=0).astype(BF16)
    vv = jnp.concatenate([vp_ref[...], vc_ref[...]], axis=0).astype(BF16)
    col = lax.broadcasted_iota(jnp.int32, (1, 2 * qb), 1)
    key_ok = jnp.logical_or(col >= qb, b > 0)
    outs = []
    for g in range(N_KV_HEADS):
        kg = kk[:, g * HEAD_DIM:(g + 1) * HEAD_DIM]
        vg = vv[:, g * HEAD_DIM:(g + 1) * HEAD_DIM]
        for r in range(rep):
            h = g * rep + r
            s = _dot_nt(q[:, h * HEAD_DIM:(h + 1) * HEAD_DIM], kg) + bias_ref[h]
            s = jnp.where(key_ok, s, -jnp.inf)
            outs.append(_softmax_pv(s, sink_ref[h], vg))
    o_ref[...] = jnp.concatenate(outs, axis=1).astype(o_ref.dtype)


def _attn_prompt(q, k, v, bias, sinks, n, l):
    hq = q.shape[1]
    hk = k.shape[1]
    qb = WINDOW
    nb = l // qb
    cur = lambda i, b, *_: (i * nb + b, 0)
    prev = lambda i, b, *_: (i * nb + jnp.maximum(b - 1, 0), 0)
    return pl.pallas_call(
        _attn_prompt_kernel,
        grid_spec=pltpu.PrefetchScalarGridSpec(
            num_scalar_prefetch=1,
            grid=(n, nb),
            in_specs=[pl.BlockSpec((qb, hq), cur),
                      pl.BlockSpec((qb, hk), prev), pl.BlockSpec((qb, hk), cur),
                      pl.BlockSpec((qb, hk), prev), pl.BlockSpec((qb, hk), cur),
                      pl.BlockSpec(bias.shape, lambda i, b, *_: (0, 0, 0), pipeline_mode=pl.Buffered(1))],
            out_specs=pl.BlockSpec((qb, hq), cur)),
        out_shape=jax.ShapeDtypeStruct((n * l, hq), BF16),
        compiler_params=pltpu.CompilerParams(dimension_semantics=("arbitrary", "arbitrary"),
                                             vmem_limit_bytes=VMEM_LIMIT),
        name="attn_prompt",
    )(sinks, q, k, k, v, v, bias)


def _attn_sample_kernel(sink_ref, q_ref, kc_ref, vc_ref, kn_ref, vn_ref, bias_ref,
                        o_ref, kwin_ref, vwin_ref, kk, vv, *, n_new):
    lb = kc_ref.shape[1]
    new_rows = kn_ref.shape[1]
    rows = q_ref.shape[1]
    hk = kc_ref.shape[2]
    grp = rows // N_KV_HEADS
    kk[...] = jnp.zeros_like(kk)
    vv[...] = jnp.zeros_like(vv)
    lane_head = lax.broadcasted_iota(jnp.int32, (grp, hk), 1) // HEAD_DIM
    for s_i in range(q_ref.shape[0]):
        kc = kc_ref[s_i]
        vc = vc_ref[s_i]
        kn = kn_ref[s_i]
        vn = vn_ref[s_i]
        kwin_ref[s_i, 0:lb - n_new, :] = kc[n_new:, :]
        kwin_ref[s_i, lb - n_new:lb, :] = kn[0:n_new, :]
        vwin_ref[s_i, 0:lb - n_new, :] = vc[n_new:, :]
        vwin_ref[s_i, lb - n_new:lb, :] = vn[0:n_new, :]
        kk[0:lb, :] = kc.astype(BF16)
        kk[lb:lb + new_rows, :] = kn.astype(BF16)
        vv[0:lb, :] = vc.astype(BF16)
        vv[lb:lb + new_rows, :] = vn.astype(BF16)
        s = _dot_nt(q_ref[s_i], kk[...]) + bias_ref[...]
        pv = _softmax_pv(s, sink_ref[...], vv[...])
        o = jnp.zeros((grp, hk), F32)
        for g in range(N_KV_HEADS):
            o = o + jnp.where(lane_head == g, pv[g * grp:(g + 1) * grp, :], 0.0)
        o_ref[s_i] = o


def _attn_sample(q_blk, kc, vc, kn, vn, bias, sink, n_new, block):
    ns, rows, hk = q_blk.shape
    lb = kc.shape[1]
    lkp = bias.shape[1]
    grp = rows // N_KV_HEADS
    per_s = lambda i: (i, 0, 0)
    return pl.pallas_call(
        functools.partial(_attn_sample_kernel, n_new=n_new),
        grid=(ns // block,),
        in_specs=[_const_spec(sink.shape),
                  pl.BlockSpec((block, rows, hk), per_s),
                  pl.BlockSpec((block, lb, hk), per_s), pl.BlockSpec((block, lb, hk), per_s),
                  pl.BlockSpec((block,) + kn.shape[1:], per_s), pl.BlockSpec((block,) + vn.shape[1:], per_s),
                  _const_spec(bias.shape)],
        out_specs=[pl.BlockSpec((block, grp, hk), per_s),
                   pl.BlockSpec((block, lb, hk), per_s), pl.BlockSpec((block, lb, hk), per_s)],
        out_shape=[jax.ShapeDtypeStruct((ns, grp, hk), F32),
                   jax.ShapeDtypeStruct((ns, lb, hk), F32),
                   jax.ShapeDtypeStruct((ns, lb, hk), F32)],
        scratch_shapes=[pltpu.VMEM((lkp, hk), BF16), pltpu.VMEM((lkp, hk), BF16)],
        compiler_params=pltpu.CompilerParams(dimension_semantics=("arbitrary",),
                                             vmem_limit_bytes=VMEM_LIMIT),
        name="attn_sample",
    )(sink, q_blk, kc, vc, kn, vn, bias)


def _chunk_major(a, nc2):
    lead = a.shape[:-2]
    r = a.shape[-2]
    a = a.reshape(lead + (r, nc2, FF_CHUNK))
    return jnp.swapaxes(a, -3, -2)


def _row_major(a):
    a = jnp.swapaxes(a, -3, -2)
    return a.reshape(a.shape[:-2] + (a.shape[-2] * a.shape[-1],))


def kernel(x_prompt, x_sample, state_ssm_re, state_ssm_im, state_ffn_conv, cache_k_win, cache_v_win, p_prompt, p_sample, g_mix, g_ffn, g_ple, ssm_lam_re, ssm_lam_im, ssm_log_dt, ssm_b_re, ssm_b_im, ssm_c_re, ssm_c_im, ssm_d, w_glu, b_glu, g_kv, w_k, w_v, k_norm, w_q, q_norm, sinks, w_o, rel_bias, w_up, conv_w, conv_b, w_down, w_ple_in, w_ple_gate):
    n, l, d = x_prompt.shape
    ns, ls, _ = x_sample.shape
    n_groups, n_state = ssm_lam_re.shape[1:]
    nst = n_groups * n_state
    ff2 = w_up.shape[2]
    nc2 = ff2 // FF_CHUNK
    hk = w_k.shape[1]
    hq = w_q.shape[2]
    n_heads = hq // HEAD_DIM
    rep = n_heads // N_KV_HEADS
    lb = cache_k_win.shape[1]

    ssm_steps = min(32, l // SUBLANES)
    ffn_rows = min(512, l)
    head_rows_p = SUBLANES
    tm = lambda a: jnp.swapaxes(a, 0, 1)

    sp = _ssm_params(ssm_lam_re[0], ssm_lam_im[0], ssm_log_dt[0], ssm_b_re[0], ssm_b_im[0], ssm_c_re[0],
                     ssm_c_im[0], ssm_steps)
    gmix0 = g_mix[0].reshape(1, d)
    dskip = ssm_d[0].reshape(1, d)
    wglu = w_glu[0].astype(BF16)
    bglu = b_glu[0].reshape(1, 2 * d)
    hp, sre_p, sim_p = _ssm_prompt(x_prompt, sp, gmix0, dskip, wglu, bglu, ssm_steps)
    hp = hp.reshape(n * l, d)
    xs_tm = tm(x_sample).reshape(ls * ns, d)
    hs, sre_s, sim_s = _ssm_sample(xs_tm, state_ssm_re[0].reshape(ns, nst), state_ssm_im[0].reshape(ns, nst),
                                   sp, gmix0, dskip, wglu, bglu, ns, ls)

    pp = p_prompt.reshape(p_prompt.shape[0] * n * l, -1)
    ps = jnp.swapaxes(p_sample, 1, 2).reshape(p_sample.shape[0] * ls * ns, -1)
    zero_head = jnp.zeros((n, nc2, head_rows_p, FF_CHUNK), F32)
    conv_s_tm = jnp.swapaxes(state_ffn_conv, 1, 2).reshape(state_ffn_conv.shape[0], 1, (CONV_WIDTH - 1) * ns, ff2)

    def ffn_layer(i, hp, hs, attn_p=None, attn_s=None):
        fw = _ffn_weights(g_ffn[i], w_up[i], conv_w[i], conv_b[i], w_down[i], g_ple[i], w_ple_gate[i],
                          w_ple_in[i])
        hp, tail_p = _ffn(hp, pp, i, zero_head, fw, ffn_rows, n, 1, attn_p)
        hs, tail_s = _ffn(hs, ps, i, _chunk_major(conv_s_tm[i], nc2), fw, ls * ns, 1, ns, attn_s)
        conv_p = _row_major(tail_p)[:, head_rows_p - (CONV_WIDTH - 1):, :]
        conv_s = jnp.swapaxes(_row_major(tail_s).reshape(CONV_WIDTH - 1, ns, ff2), 0, 1)
        return hp, hs, conv_p, conv_s

    hp, hs, conv_p0, conv_s0 = ffn_layer(0, hp, hs)

    qw = dict(g_kv=g_kv.reshape(1, d), g_mix=g_mix[1].reshape(1, d), w_k=w_k.astype(BF16), w_v=w_v.astype(BF16),
              w_q=w_q[0].astype(BF16), k_norm=jnp.tile(k_norm, hk // HEAD_DIM).reshape(1, hk),
              q_norm=jnp.tile(q_norm[0], n_heads).reshape(1, hq),
              k_pool=_head_pool_matrices(hk), q_pool=_head_pool_matrices(hq))
    q_p, k_p, v_p = _qkv(hp, qw, ffn_rows)
    q_s, k_s, v_s = _qkv(hs, qw, ls * ns)

    wo = w_o[0].astype(BF16)
    bias_p = _bias_table(rel_bias, WINDOW, WINDOW, 2 * WINDOW)
    o_p = _attn_prompt(q_p, k_p, v_p, bias_p, sinks[0].astype(F32), n, l)

    lkp = 2 * WINDOW
    new_rows = 16
    pad_new = lambda a: jnp.pad(tm(a.reshape(ls, ns, hk)), ((0, 0), (0, new_rows - ls), (0, 0)))
    kc = cache_k_win.reshape(ns, lb, hk)
    vc = cache_v_win.reshape(ns, lb, hk)
    bias_s = _bias_table(rel_bias, lb, ls, lb + ls)
    bias_s = jnp.pad(bias_s, ((0, 0), (0, 0), (0, lkp - lb - ls)), constant_values=-jnp.inf)
    bias_s = bias_s.reshape(n_heads * ls, lkp)
    sink_s = jnp.repeat(sinks[0].astype(F32), ls).reshape(n_heads * ls, 1)
    q5 = jnp.transpose(q_s.reshape(ls, ns, N_KV_HEADS, rep, HEAD_DIM), (1, 2, 3, 0, 4))
    q_blk = jnp.einsum('sgrtd,gh->sgrthd', q5, jnp.eye(N_KV_HEADS, dtype=q5.dtype))
    q_blk = q_blk.reshape(ns, n_heads * ls, hk)
    o4, k_win_s, v_win_s = _attn_sample(q_blk, kc, vc, pad_new(k_s), pad_new(v_s), bias_s, sink_s, ls, 8)
    o_s = jnp.transpose(o4.reshape(ns, rep, ls, N_KV_HEADS, HEAD_DIM), (2, 0, 3, 1, 4)).reshape(ls * ns, hq)
    o_s = o_s.astype(BF16)

    hp, hs, conv_p1, conv_s1 = ffn_layer(1, hp, hs, (o_p, wo), (o_s, wo))

    y_prompt = hp.reshape(n, l, d)
    y_sample = tm(hs.reshape(ls, ns, d))
    ssm_shape = (1, -1, n_groups, n_state)
    kvh_shape = (-1, lb, N_KV_HEADS, HEAD_DIM)
    k_win_p = k_p.reshape(n, l, hk)[:, l - WINDOW:].reshape(n, WINDOW, N_KV_HEADS, HEAD_DIM)
    v_win_p = v_p.reshape(n, l, hk)[:, l - WINDOW:].reshape(n, WINDOW, N_KV_HEADS, HEAD_DIM)
    return (y_prompt, y_sample,
            sre_p.reshape(ssm_shape), sim_p.reshape(ssm_shape),
            sre_s.reshape(ssm_shape), sim_s.reshape(ssm_shape),
            jnp.stack([conv_p0, conv_p1]), jnp.stack([conv_s0, conv_s1]),
            k_win_p, v_win_p, k_win_s.reshape(kvh_shape), v_win_s.reshape(kvh_shape))
```

```python
import functools
import math

import jax
import jax.numpy as jnp
from jax import lax
from jax.experimental import pallas as pl
from jax.experimental.pallas import tpu as pltpu

F32 = jnp.float32
BF16 = jnp.bfloat16

EPS = 1e-6
SSM_GROUP = 16
SSM_STATE = 64
HEAD_DIM = 64
N_KV_HEADS = 4
WINDOW = 128
NUM_BUCKETS = 32
MAX_DISTANCE = 128
CONV_WIDTH = 3

LANES = 128
SUBLANES = 8
MXU_DIM = 256
VMEM_LIMIT = 56 * 1024 * 1024

SCAN_COLS = 1024
FF_CHUNK = 256


def _dot(a, b):
    return jnp.dot(a, b, preferred_element_type=F32)


def _dot_nt(a, b):
    return lax.dot_general(a, b, (((1,), (1,)), ((), ())), preferred_element_type=F32)


def _rms(x, g):
    ms = jnp.mean(x * x, axis=-1, keepdims=True)
    return x * lax.rsqrt(ms + EPS) * g


def _head_rms(x, pool, spread, g):
    ms = _dot((x * x).astype(BF16), pool)
    scale = lax.rsqrt(ms + EPS)
    hi = scale.astype(BF16)
    lo = (scale - hi.astype(F32)).astype(BF16)
    return x * (_dot(hi, spread) + _dot(lo, spread)) * g


def _const_spec(shape):
    nd = len(shape)
    return pl.BlockSpec(shape, lambda *_: (0,) * nd, pipeline_mode=pl.Buffered(1))


def _b_project(ub, bre_ref, bim_ref, bure, buim):
    n_kt = bre_ref.shape[0]
    kw = bre_ref.shape[1]
    nw = bre_ref.shape[2]
    for kt in range(n_kt):
        lhs = ub[:, kt * kw:(kt + 1) * kw]
        bure[:, kt * nw:(kt + 1) * nw] = _dot(lhs, bre_ref[kt])
        buim[:, kt * nw:(kt + 1) * nw] = _dot(lhs, bim_ref[kt])


def _scan(bure, buim, are_ref, aim_ref, n_seq, n_steps, init_fn, final_fn, store):
    nst = bure.shape[1]
    for cb in range(nst // SCAN_COLS):
        cs = slice(cb * SCAN_COLS, (cb + 1) * SCAN_COLS)
        ar = jnp.broadcast_to(are_ref[:, cs], (SUBLANES, SCAN_COLS))
        ai = jnp.broadcast_to(aim_ref[:, cs], (SUBLANES, SCAN_COLS))

        def group(g, _, cs=cs, ar=ar, ai=ai):
            r0 = pl.multiple_of(g * SUBLANES, SUBLANES)

            def step(k, carry):
                hr, hi = carry
                row = pl.multiple_of(k * n_seq + r0, SUBLANES)
                br = bure[pl.ds(row, SUBLANES), cs]
                bi = buim[pl.ds(row, SUBLANES), cs]
                nr = ar * hr - ai * hi + br
                ni = ar * hi + ai * hr + bi
                if store:
                    bure[pl.ds(row, SUBLANES), cs] = nr
                    buim[pl.ds(row, SUBLANES), cs] = ni
                return nr, ni

            hr, hi = lax.fori_loop(0, n_steps, step, init_fn(r0, cs), unroll=min(n_steps, 4))
            final_fn(r0, cs, hr, hi)
            return 0

        if n_seq == SUBLANES:
            group(0, 0)
        else:
            lax.fori_loop(0, n_seq // SUBLANES, group, 0)


def _c_project_glu(x, u, bure, buim, cre_ref, ncim_ref, dskip_ref, wglu_ref, bglu_ref):
    d = x.shape[1]
    n_blk = cre_ref.shape[0]
    kw = cre_ref.shape[1]
    ys = []
    for m in range(n_blk):
        hr = bure[:, m * kw:(m + 1) * kw].astype(BF16)
        hi = buim[:, m * kw:(m + 1) * kw].astype(BF16)
        ys.append(_dot(hr, cre_ref[m]) + _dot(hi, ncim_ref[m]))
    y = jnp.concatenate(ys, axis=1) + dskip_ref[...] * u
    z = jax.nn.gelu(y).astype(BF16)
    gl = _dot(z, wglu_ref[...]) + bglu_ref[...]
    return x + gl[:, :d] * jax.nn.sigmoid(gl[:, d:])


def _ssm_prompt_kernel(x_ref, gmix_ref, are_ref, aim_ref, apw_re_ref, apw_im_ref, bre_ref, bim_ref, cre_ref,
                       ncim_ref, dskip_ref, wglu_ref, bglu_ref,
                       out_ref, sre_ref, sim_ref,
                       slab, xp, bure, buim, hin_re, hin_im, car_re, car_im, *, n_steps, pitch):
    n_slab = slab.shape[0]

    @pl.when(pl.program_id(1) == 0)
    def _():
        car_re[...] = jnp.zeros_like(car_re)
        car_im[...] = jnp.zeros_like(car_im)

    for j in range(SUBLANES):
        for c in range(n_slab):
            slab[c, j * pitch:j * pitch + n_steps, :] = x_ref[j * n_steps:(j + 1) * n_steps,
                                                             c * LANES:(c + 1) * LANES]

    def gather(k, _):
        r0 = pl.multiple_of(k * SUBLANES, SUBLANES)
        for c in range(n_slab):
            xp[pl.ds(r0, SUBLANES), c * LANES:(c + 1) * LANES] = slab[c, pl.ds(k, SUBLANES, stride=pitch), :]
        return 0

    lax.fori_loop(0, n_steps, gather, 0, unroll=4)

    x = xp[...]
    u = _rms(x, gmix_ref[...])
    _b_project(u.astype(BF16), bre_ref, bim_ref, bure, buim)

    def zero_init(r0, cs):
        z = jnp.zeros((SUBLANES, SCAN_COLS), F32)
        return z, z

    def keep_end(r0, cs, hr, hi):
        hin_re[:, cs] = hr
        hin_im[:, cs] = hi

    _scan(bure, buim, are_ref, aim_ref, SUBLANES, n_steps, zero_init, keep_end, store=False)

    hr = car_re[...]
    hi = car_im[...]
    apr = apw_re_ref[...]
    api = apw_im_ref[...]
    for j in range(SUBLANES):
        er = hin_re[j:j + 1, :]
        ei = hin_im[j:j + 1, :]
        hin_re[j:j + 1, :] = hr
        hin_im[j:j + 1, :] = hi
        hr, hi = apr * hr - api * hi + er, apr * hi + api * hr + ei
    car_re[...] = hr
    car_im[...] = hi
    sre_ref[...] = hr
    sim_ref[...] = hi

    def true_init(r0, cs):
        return hin_re[:, cs], hin_im[:, cs]

    _scan(bure, buim, are_ref, aim_ref, SUBLANES, n_steps, true_init, lambda *_: None, store=True)

    xp[...] = _c_project_glu(x, u, bure, buim, cre_ref, ncim_ref, dskip_ref, wglu_ref, bglu_ref)

    def scatter(k, _):
        r0 = pl.multiple_of(k * SUBLANES, SUBLANES)
        for c in range(n_slab):
            slab[c, pl.ds(k, SUBLANES, stride=pitch), :] = xp[pl.ds(r0, SUBLANES), c * LANES:(c + 1) * LANES]
        return 0

    lax.fori_loop(0, n_steps, scatter, 0, unroll=4)

    for j in range(SUBLANES):
        for c in range(n_slab):
            out_ref[j * n_steps:(j + 1) * n_steps, c * LANES:(c + 1) * LANES] = slab[c, j * pitch:j * pitch + n_steps, :]


def _ssm_sample_kernel(x_ref, h0re_ref, h0im_ref, gmix_ref, are_ref, aim_ref, bre_ref, bim_ref, cre_ref,
                       ncim_ref, dskip_ref, wglu_ref, bglu_ref,
                       out_ref, sre_ref, sim_ref, bure, buim, *, n_seq, n_steps):
    x = x_ref[...]
    u = _rms(x, gmix_ref[...])
    _b_project(u.astype(BF16), bre_ref, bim_ref, bure, buim)

    def init(r0, cs):
        return h0re_ref[pl.ds(r0, SUBLANES), cs], h0im_ref[pl.ds(r0, SUBLANES), cs]

    def final(r0, cs, hr, hi):
        sre_ref[pl.ds(r0, SUBLANES), cs] = hr
        sim_ref[pl.ds(r0, SUBLANES), cs] = hi

    _scan(bure, buim, are_ref, aim_ref, n_seq, n_steps, init, final, store=True)
    out_ref[...] = _c_project_glu(x, u, bure, buim, cre_ref, ncim_ref, dskip_ref, wglu_ref, bglu_ref)


def _ssm_params(lam_re, lam_im, log_dt, b_re, b_im, c_re, c_im, n_pow):
    g, p = lam_re.shape
    lr = lam_re.astype(F32)
    li = lam_im.astype(F32)
    dt = jnp.exp(log_dt.astype(F32))[:, None]
    mag = jnp.exp(lr * dt)
    ang = li * dt
    ab_re = mag * jnp.cos(ang)
    ab_im = mag * jnp.sin(ang)
    den = lr * lr + li * li
    nr = ab_re - 1.0
    f_re = (nr * lr + ab_im * li) / den
    f_im = (ab_im * lr - nr * li) / den
    br = b_re.astype(F32)
    bi = b_im.astype(F32)
    bb_re = f_re[..., None] * br - f_im[..., None] * bi
    bb_im = f_re[..., None] * bi + f_im[..., None] * br

    c = bb_re.shape[2]
    gk = MXU_DIM // c
    eye_k = jnp.eye(gk, dtype=F32)

    def b_blocks(bb):
        bt = jnp.transpose(bb, (0, 2, 1)).reshape(g // gk, gk, c, p)
        return jnp.einsum('tgcp,gh->tgchp', bt, eye_k).reshape(g // gk, gk * c, gk * p).astype(BF16)

    gc = LANES // c
    eye_c = jnp.eye(gc, dtype=F32)

    def c_blocks(cc):
        ct = jnp.transpose(cc.astype(F32), (0, 2, 1)).reshape(g // gc, gc, p, c)
        return jnp.einsum('tgpc,gh->tgphc', ct, eye_c).reshape(g // gc, gc * p, gc * c).astype(BF16)

    pw_re = ab_re.reshape(1, g * p)
    pw_im = ab_im.reshape(1, g * p)
    for _ in range(n_pow.bit_length() - 1):
        pw_re, pw_im = pw_re * pw_re - pw_im * pw_im, 2.0 * pw_re * pw_im
    return dict(a_re=ab_re.reshape(1, g * p), a_im=ab_im.reshape(1, g * p),
                ap_re=pw_re, ap_im=pw_im,
                b_re=b_blocks(bb_re), b_im=b_blocks(bb_im),
                c_re=c_blocks(c_re), nc_im=c_blocks(-c_im))


def _ssm_prompt(x, sp, gmix, dskip, wglu, bglu, n_steps):
    n, l, d = x.shape
    nst = sp['a_re'].shape[1]
    rows = SUBLANES * n_steps
    pitch = n_steps + SUBLANES
    consts = [gmix, sp['a_re'], sp['a_im'], sp['ap_re'], sp['ap_im'], sp['b_re'], sp['b_im'], sp['c_re'],
              sp['nc_im'], dskip, wglu, bglu]
    out, sre, sim = pl.pallas_call(
        functools.partial(_ssm_prompt_kernel, n_steps=n_steps, pitch=pitch),
        grid=(n, l // rows),
        in_specs=[pl.BlockSpec((None, rows, d), lambda i, t: (i, t, 0))] + [_const_spec(c.shape) for c in consts],
        out_specs=[pl.BlockSpec((None, rows, d), lambda i, t: (i, t, 0)),
                   pl.BlockSpec((None, 1, nst), lambda i, t: (i, 0, 0)),
                   pl.BlockSpec((None, 1, nst), lambda i, t: (i, 0, 0))],
        out_shape=[jax.ShapeDtypeStruct((n, l, d), F32),
                   jax.ShapeDtypeStruct((n, 1, nst), F32),
                   jax.ShapeDtypeStruct((n, 1, nst), F32)],
        scratch_shapes=[pltpu.VMEM((d // LANES, SUBLANES * pitch, LANES), F32),
                        pltpu.VMEM((rows, d), F32),
                        pltpu.VMEM((rows, nst), F32),
                        pltpu.VMEM((rows, nst), F32),
                        pltpu.VMEM((SUBLANES, nst), F32),
                        pltpu.VMEM((SUBLANES, nst), F32),
                        pltpu.VMEM((1, nst), F32),
                        pltpu.VMEM((1, nst), F32)],
        compiler_params=pltpu.CompilerParams(dimension_semantics=("arbitrary", "arbitrary"),
                                             vmem_limit_bytes=VMEM_LIMIT),
        name="ssm_prompt",
    )(x, *consts)
    return out, sre[:, 0], sim[:, 0]


def _ssm_sample(x_tm, h0re, h0im, sp, gmix, dskip, wglu, bglu, n_seq, n_steps):
    rows, d = x_tm.shape
    nst = sp['a_re'].shape[1]
    args = [x_tm, h0re, h0im, gmix, sp['a_re'], sp['a_im'], sp['b_re'], sp['b_im'], sp['c_re'], sp['nc_im'],
            dskip, wglu, bglu]
    return pl.pallas_call(
        functools.partial(_ssm_sample_kernel, n_seq=n_seq, n_steps=n_steps),
        grid=(1,),
        in_specs=[_const_spec(a.shape) for a in args],
        out_specs=[pl.BlockSpec((rows, d), lambda i: (0, 0)), pl.BlockSpec((n_seq, nst), lambda i: (0, 0)),
                   pl.BlockSpec((n_seq, nst), lambda i: (0, 0))],
        out_shape=[jax.ShapeDtypeStruct((rows, d), F32),
                   jax.ShapeDtypeStruct((n_seq, nst), F32),
                   jax.ShapeDtypeStruct((n_seq, nst), F32)],
        scratch_shapes=[pltpu.VMEM((rows, nst), F32), pltpu.VMEM((rows, nst), F32)],
        compiler_params=pltpu.CompilerParams(dimension_semantics=("arbitrary",),
                                             vmem_limit_bytes=VMEM_LIMIT),
        name="ssm_sample",
    )(*args)


def _ffn_kernel(*refs, rows, head_rows, shift, has_attn):
    if has_attn:
        h_ref, o_ref, wo_ref = refs[:3]
        refs = refs[3:]
    else:
        h_ref = refs[0]
        refs = refs[1:]
    (p_ref, head_ref, gffn_ref, wup_ref, cw_ref, cb_ref, wdown_ref, gple_ref, wgate_ref, win_ref,
     out_ref, tail_ref, acc, xn_ref, ext_a, ext_b) = refs
    n_chunk = wdown_ref.shape[0]
    up0 = head_rows
    up1 = head_rows + rows

    @pl.when(pl.program_id(1) == 0)
    def _():
        tail_ref[...] = head_ref[...]

    x = h_ref[...]
    if has_attn:
        x = x + _dot(o_ref[...], wo_ref[...])
    xn_ref[...] = _rms(x, gffn_ref[...]).astype(BF16)
    acc[...] = jnp.zeros_like(acc)

    n_slab = ext_a.shape[1]

    def up_project(c, ext):
        for half, cc in enumerate((c, n_chunk + c)):
            up = _dot(xn_ref[...], wup_ref[cc])
            for s in range(n_slab):
                ext[half, s, 0:up0, :] = tail_ref[cc, :, s * LANES:(s + 1) * LANES]
                ext[half, s, up0:up1, :] = up[:, s * LANES:(s + 1) * LANES]

    def conv(ext, half, cc):
        w = cw_ref[cc]
        b = cb_ref[cc]
        parts = []
        for s in range(n_slab):
            ls = slice(s * LANES, (s + 1) * LANES)
            t2 = ext[half, s, up0 - 2 * shift:up1 - 2 * shift, :]
            t1 = ext[half, s, up0 - shift:up1 - shift, :]
            tail_ref[cc, :, ls] = ext[half, s, rows:up1, :]
            r = b[:, ls] + t2 * w[0:1, ls]
            r = r + t1 * w[1:2, ls]
            parts.append(r + ext[half, s, up0:up1, :] * w[2:3, ls])
        return jnp.concatenate(parts, axis=1)

    def down_project(c, ext):
        cg = conv(ext, 0, c)
        cv = conv(ext, 1, n_chunk + c)
        act = (cg * jax.nn.sigmoid(cg) * cv).astype(BF16)
        acc[...] += _dot(act, wdown_ref[c])

    up_project(0, ext_a)

    def pair(i, _):
        c = 2 * i
        up_project(c + 1, ext_b)
        down_project(c, ext_a)
        up_project(c + 2, ext_a)
        down_project(c + 1, ext_b)
        return 0

    lax.fori_loop(0, (n_chunk - 1) // 2, pair, 0)
    if n_chunk % 2:
        down_project(n_chunk - 1, ext_a)
    else:
        up_project(n_chunk - 1, ext_b)
        down_project(n_chunk - 2, ext_a)
        down_project(n_chunk - 1, ext_b)

    h2 = x + acc[...]
    gate = jax.nn.sigmoid(_dot(_rms(h2, gple_ref[...]).astype(BF16), wgate_ref[...]))
    pe = _dot(p_ref[...].astype(BF16), win_ref[...])
    out_ref[...] = h2 + pe * gate


def _ffn_weights(g_ffn, w_up, conv_w, conv_b, w_down, g_ple, w_gate, w_in):
    depth, d, ff2 = w_up.shape
    nc2 = ff2 // FF_CHUNK
    return dict(
        g_ffn=g_ffn.reshape(depth, 1, d),
        w_up=jnp.transpose(w_up.astype(BF16).reshape(depth, d, nc2, FF_CHUNK), (0, 2, 1, 3)),
        conv_w=jnp.transpose(conv_w.reshape(depth, CONV_WIDTH, nc2, FF_CHUNK), (0, 2, 1, 3)),
        conv_b=conv_b.reshape(depth, nc2, 1, FF_CHUNK),
        w_down=w_down.astype(BF16).reshape(depth, nc2 // 2, FF_CHUNK, d),
        g_ple=g_ple.reshape(depth, 1, d),
        w_gate=w_gate.astype(BF16),
        w_in=w_in.astype(BF16),
    )


def _layer_spec(shape, layer):
    nd = len(shape)
    return pl.BlockSpec((None,) + tuple(shape[1:]), lambda *_: (layer,) + (0,) * (nd - 1),
                        pipeline_mode=pl.Buffered(1))


def _ffn(h, p, layer, head, fw, rows, n_seq_tiles, shift, attn=None):
    total, d = h.shape
    tiles = total // rows // n_seq_tiles
    nc2, head_rows = head.shape[1], head.shape[2]
    row_map = lambda i, t: (i * tiles + t, 0)
    p_map = lambda i, t: (layer * (total // rows) + i * tiles + t, 0)
    consts = [fw['g_ffn'], fw['w_up'], fw['conv_w'], fw['conv_b'], fw['w_down'], fw['g_ple'], fw['w_gate'],
              fw['w_in']]
    args = [h]
    in_specs = [pl.BlockSpec((rows, d), row_map)]
    if attn is not None:
        o, wo = attn
        args += [o, wo]
        in_specs += [pl.BlockSpec((rows, o.shape[1]), row_map), _const_spec(wo.shape)]
    args += [p, head] + consts
    in_specs += [pl.BlockSpec((rows, p.shape[1]), p_map),
                 pl.BlockSpec((None, nc2, head_rows, FF_CHUNK), lambda i, t: (i, 0, 0, 0))]
    in_specs += [_layer_spec(c.shape, layer) for c in consts]
    out, tail = pl.pallas_call(
        functools.partial(_ffn_kernel, rows=rows, head_rows=head_rows, shift=shift, has_attn=attn is not None),
        grid=(n_seq_tiles, tiles),
        in_specs=in_specs,
        out_specs=[pl.BlockSpec((rows, d), row_map),
                   pl.BlockSpec((None, nc2, head_rows, FF_CHUNK), lambda i, t: (i, 0, 0, 0))],
        out_shape=[jax.ShapeDtypeStruct((total, d), F32),
                   jax.ShapeDtypeStruct(head.shape, F32)],
        scratch_shapes=[pltpu.VMEM((rows, d), F32),
                        pltpu.VMEM((rows, d), BF16),
                        pltpu.VMEM((2, FF_CHUNK // LANES, head_rows + rows, LANES), F32),
                        pltpu.VMEM((2, FF_CHUNK // LANES, head_rows + rows, LANES), F32)],
        compiler_params=pltpu.CompilerParams(dimension_semantics=("arbitrary", "arbitrary"),
                                             vmem_limit_bytes=VMEM_LIMIT),
        name="ffn_ple",
    )(*args)
    return out, tail


def _qkv_kernel(*refs, paired):
    (h_ref, gkv_ref, gmix_ref, wk_ref, wv_ref, wq_ref, knorm_ref, qnorm_ref, kpool_ref, kspread_ref,
     qpool_ref, qspread_ref) = refs[:12]
    refs = refs[12:]
    x = h_ref[...]
    s = _rms(x, gkv_ref[...]).astype(BF16)
    k = _head_rms(_dot(s, wk_ref[...]), kpool_ref[...], kspread_ref[...], knorm_ref[...])
    v = _dot(s, wv_ref[...])
    xn = _rms(x, gmix_ref[...]).astype(BF16)
    q = _head_rms(_dot(xn, wq_ref[...]), qpool_ref[...], qspread_ref[...], qnorm_ref[...])
    if paired:
        place_ref, fill_ref, q_ref, k_ref, v_ref, kx_ref, vx_ref = refs
        kx_ref[...] = _dot(k.astype(BF16), place_ref[...]).astype(BF16)
        vx_ref[...] = (_dot(v.astype(BF16), place_ref[...]) + fill_ref[...]).astype(BF16)
    else:
        q_ref, k_ref, v_ref = refs
    k_ref[...] = k
    v_ref[...] = v
    q_ref[...] = (q * (HEAD_DIM ** -0.5)).astype(q_ref.dtype)


def _head_pool_matrices(n):
    member = (jnp.arange(n)[:, None] // HEAD_DIM) == jnp.arange(LANES)[None, :]
    return member.astype(BF16) * (1.0 / HEAD_DIM), member.T.astype(BF16)


def _pair_placement(hk):
    src = jnp.arange(hk)
    dst = jnp.arange(4 * hk)
    same_head = (src[:, None] // HEAD_DIM) == (dst[None, :] // (4 * HEAD_DIM))
    sub = (dst % (4 * HEAD_DIM)) // HEAD_DIM
    same_dim = (src[:, None] % HEAD_DIM) == (dst[None, :] % HEAD_DIM)
    place = same_head & same_dim & ((sub == 0) | (sub == 3))[None, :]
    fill = ((sub == 1) | (sub == 2)).astype(F32).reshape(1, 4 * hk)
    return place.astype(BF16), fill


def _qkv(h, qw, rows, paired):
    total, d = h.shape
    hk = qw['w_k'].shape[1]
    hq = qw['w_q'].shape[1]
    consts = [qw['g_kv'], qw['g_mix'], qw['w_k'], qw['w_v'], qw['w_q'], qw['k_norm'], qw['q_norm'],
              *qw['k_pool'], *qw['q_pool']]
    row_map = lambda i: (i, 0)
    out_specs = [pl.BlockSpec((rows, hq), row_map), pl.BlockSpec((rows, hk), row_map),
                 pl.BlockSpec((rows, hk), row_map)]
    out_shape = [jax.ShapeDtypeStruct((total, hq), BF16),
                 jax.ShapeDtypeStruct((total, hk), F32),
                 jax.ShapeDtypeStruct((total, hk), F32)]
    if paired:
        consts += list(_pair_placement(hk))
        out_specs += [pl.BlockSpec((rows, 4 * hk), row_map)] * 2
        out_shape += [jax.ShapeDtypeStruct((total, 4 * hk), BF16)] * 2
    return pl.pallas_call(
        functools.partial(_qkv_kernel, paired=paired),
        grid=(total // rows,),
        in_specs=[pl.BlockSpec((rows, d), row_map)] + [_const_spec(c.shape) for c in consts],
        out_specs=out_specs,
        out_shape=out_shape,
        compiler_params=pltpu.CompilerParams(dimension_semantics=("arbitrary",),
                                             vmem_limit_bytes=VMEM_LIMIT),
        name="qkv",
    )(h, *consts)


def _t5_bucket(dist):
    max_exact = NUM_BUCKETS // 2
    df = jnp.maximum(dist, 1).astype(F32)
    large = max_exact + (jnp.log(df / max_exact) / math.log(MAX_DISTANCE / max_exact)
                         * (NUM_BUCKETS - max_exact)).astype(jnp.int32)
    return jnp.where(dist < max_exact, dist, jnp.minimum(large, NUM_BUCKETS - 1))


def _bias_table(rel_bias, lb, qb, lk):
    dist = lb + jnp.arange(qb)[:, None] - jnp.arange(lk)[None, :]
    onehot = jax.nn.one_hot(_t5_bucket(jnp.maximum(dist, 0)), NUM_BUCKETS, dtype=F32)
    bias = jnp.einsum('qkb,bh->hqk', onehot, rel_bias.astype(F32), precision=lax.Precision.HIGHEST)
    ok = (dist >= 0) & (dist <= WINDOW)
    return jnp.where(ok[None], bias, -jnp.inf)


def _softmax_pv(s, sink, v):
    m = jnp.maximum(jnp.max(s, axis=-1, keepdims=True), sink)
    pr = jnp.exp(s - m)
    den = jnp.sum(pr, axis=-1, keepdims=True) + jnp.exp(sink - m)
    return _dot(pr.astype(BF16), v) / den


def _attn_prompt_kernel(sink_ref, q_ref, kx_ref, vx_ref, bias_ref, o_ref, kx_prev, vx_prev):
    qb = q_ref.shape[0]
    n_heads = bias_ref.shape[0]
    rep = n_heads // N_KV_HEADS
    grp = 4 * HEAD_DIM

    @pl.when(pl.program_id(1) == 0)
    def _():
        kx_prev[...] = jnp.zeros_like(kx_prev)
        vx_prev[...] = jnp.zeros_like(vx_prev)

    low = lax.broadcasted_iota(jnp.int32, (qb, LANES), 1) < HEAD_DIM
    for g in range(N_KV_HEADS):
        gs = slice(g * grp, (g + 1) * grp)
        kx = jnp.concatenate([kx_prev[:, gs], kx_ref[:, gs]], axis=0)
        vx = jnp.concatenate([vx_prev[:, gs], vx_ref[:, gs]], axis=0)
        for pair in range(rep // 2):
            h0 = g * rep + 2 * pair
            ls = slice(h0 * HEAD_DIM, h0 * HEAD_DIM + LANES)
            qp = q_ref[:, ls]
            res = []
            for h, kh in ((h0, kx[:, :LANES]), (h0 + 1, kx[:, LANES:])):
                s = _dot_nt(qp, kh) + bias_ref[h]
                m = jnp.maximum(jnp.max(s, axis=-1, keepdims=True), sink_ref[h])
                res.append((_dot(jnp.exp(s - m).astype(BF16), vx), jnp.exp(sink_ref[h] - m)))
            (ra, ea), (rb, eb) = res
            num = jnp.where(low, ra[:, :LANES], rb[:, LANES:])
            den = jnp.where(low, ra[:, LANES:], rb[:, :LANES]) + jnp.where(low, ea, eb)
            o_ref[:, ls] = (num / den).astype(o_ref.dtype)
    kx_prev[...] = kx_ref[...]
    vx_prev[...] = vx_ref[...]


def _attn_prompt(q, kx, vx, bias, sinks, n, l):
    hq = q.shape[1]
    wx = kx.shape[1]
    qb = WINDOW
    nb = l // qb
    cur = lambda i, b, *_: (i * nb + b, 0)
    return pl.pallas_call(
        _attn_prompt_kernel,
        grid_spec=pltpu.PrefetchScalarGridSpec(
            num_scalar_prefetch=1,
            grid=(n, nb),
            in_specs=[pl.BlockSpec((qb, hq), cur), pl.BlockSpec((qb, wx), cur), pl.BlockSpec((qb, wx), cur),
                      pl.BlockSpec((None,) + bias.shape[1:], lambda i, b, *_: (jnp.minimum(b, 1), 0, 0, 0))],
            out_specs=pl.BlockSpec((qb, hq), cur),
            scratch_shapes=[pltpu.VMEM((qb, wx), BF16), pltpu.VMEM((qb, wx), BF16)]),
        out_shape=jax.ShapeDtypeStruct((n * l, hq), BF16),
        compiler_params=pltpu.CompilerParams(dimension_semantics=("arbitrary", "arbitrary"),
                                             vmem_limit_bytes=VMEM_LIMIT),
        name="attn_prompt",
    )(sinks, q, kx, vx, bias)


def _attn_sample_kernel(sink_ref, q_ref, kc_ref, vc_ref, kn_ref, vn_ref, bias_ref,
                        o_ref, kwin_ref, vwin_ref, kk, vv, *, n_new):
    lb = kc_ref.shape[1]
    new_rows = kn_ref.shape[1]
    rows = q_ref.shape[1]
    hk = kc_ref.shape[2]
    grp = rows // N_KV_HEADS
    kk[...] = jnp.zeros_like(kk)
    vv[...] = jnp.zeros_like(vv)
    lane_head = lax.broadcasted_iota(jnp.int32, (grp, hk), 1) // HEAD_DIM
    for s_i in range(q_ref.shape[0]):
        kc = kc_ref[s_i]
        vc = vc_ref[s_i]
        kn = kn_ref[s_i]
        vn = vn_ref[s_i]
        kwin_ref[s_i, 0:lb - n_new, :] = kc[n_new:, :]
        kwin_ref[s_i, lb - n_new:lb, :] = kn[0:n_new, :]
        vwin_ref[s_i, 0:lb - n_new, :] = vc[n_new:, :]
        vwin_ref[s_i, lb - n_new:lb, :] = vn[0:n_new, :]
        kk[0:lb, :] = kc.astype(BF16)
        kk[lb:lb + new_rows, :] = kn.astype(BF16)
        vv[0:lb, :] = vc.astype(BF16)
        vv[lb:lb + new_rows, :] = vn.astype(BF16)
        s = _dot_nt(q_ref[s_i], kk[...]) + bias_ref[...]
        pv = _softmax_pv(s, sink_ref[...], vv[...])
        o = jnp.zeros((grp, hk), F32)
        for g in range(N_KV_HEADS):
            o = o + jnp.where(lane_head == g, pv[g * grp:(g + 1) * grp, :], 0.0)
        o_ref[s_i] = o


def _attn_sample(q_blk, kc, vc, kn, vn, bias, sink, n_new, block):
    ns, rows, hk = q_blk.shape
    lb = kc.shape[1]
    lkp = bias.shape[1]
    grp = rows // N_KV_HEADS
    per_s = lambda i: (i, 0, 0)
    return pl.pallas_call(
        functools.partial(_attn_sample_kernel, n_new=n_new),
        grid=(ns // block,),
        in_specs=[_const_spec(sink.shape),
                  pl.BlockSpec((block, rows, hk), per_s),
                  pl.BlockSpec((block, lb, hk), per_s), pl.BlockSpec((block, lb, hk), per_s),
                  pl.BlockSpec((block,) + kn.shape[1:], per_s), pl.BlockSpec((block,) + vn.shape[1:], per_s),
                  _const_spec(bias.shape)],
        out_specs=[pl.BlockSpec((block, grp, hk), per_s),
                   pl.BlockSpec((block, lb, hk), per_s), pl.BlockSpec((block, lb, hk), per_s)],
        out_shape=[jax.ShapeDtypeStruct((ns, grp, hk), F32),
                   jax.ShapeDtypeStruct((ns, lb, hk), F32),
                   jax.ShapeDtypeStruct((ns, lb, hk), F32)],
        scratch_shapes=[pltpu.VMEM((lkp, hk), BF16), pltpu.VMEM((lkp, hk), BF16)],
        compiler_params=pltpu.CompilerParams(dimension_semantics=("arbitrary",),
                                             vmem_limit_bytes=VMEM_LIMIT),
        name="attn_sample",
    )(sink, q_blk, kc, vc, kn, vn, bias)


def _chunk_major(a, nc2):
    lead = a.shape[:-2]
    r = a.shape[-2]
    a = a.reshape(lead + (r, nc2, FF_CHUNK))
    return jnp.swapaxes(a, -3, -2)


def _row_major(a):
    a = jnp.swapaxes(a, -3, -2)
    return a.reshape(a.shape[:-2] + (a.shape[-2] * a.shape[-1],))


def kernel(x_prompt, x_sample, state_ssm_re, state_ssm_im, state_ffn_conv, cache_k_win, cache_v_win, p_prompt, p_sample, g_mix, g_ffn, g_ple, ssm_lam_re, ssm_lam_im, ssm_log_dt, ssm_b_re, ssm_b_im, ssm_c_re, ssm_c_im, ssm_d, w_glu, b_glu, g_kv, w_k, w_v, k_norm, w_q, q_norm, sinks, w_o, rel_bias, w_up, conv_w, conv_b, w_down, w_ple_in, w_ple_gate):
    n, l, d = x_prompt.shape
    ns, ls, _ = x_sample.shape
    n_groups, n_state = ssm_lam_re.shape[1:]
    nst = n_groups * n_state
    ff2 = w_up.shape[2]
    nc2 = ff2 // FF_CHUNK
    hk = w_k.shape[1]
    hq = w_q.shape[2]
    n_heads = hq // HEAD_DIM
    rep = n_heads // N_KV_HEADS
    lb = cache_k_win.shape[1]

    ssm_steps = min(32, l // SUBLANES)
    ffn_rows = min(512, l)
    head_rows_p = SUBLANES
    tm = lambda a: jnp.swapaxes(a, 0, 1)

    sp = _ssm_params(ssm_lam_re[0], ssm_lam_im[0], ssm_log_dt[0], ssm_b_re[0], ssm_b_im[0], ssm_c_re[0],
                     ssm_c_im[0], ssm_steps)
    gmix0 = g_mix[0].reshape(1, d)
    dskip = ssm_d[0].reshape(1, d)
    wglu = w_glu[0].astype(BF16)
    bglu = b_glu[0].reshape(1, 2 * d)
    hp, sre_p, sim_p = _ssm_prompt(x_prompt, sp, gmix0, dskip, wglu, bglu, ssm_steps)
    hp = hp.reshape(n * l, d)
    xs_tm = tm(x_sample).reshape(ls * ns, d)
    hs, sre_s, sim_s = _ssm_sample(xs_tm, state_ssm_re[0].reshape(ns, nst), state_ssm_im[0].reshape(ns, nst),
                                   sp, gmix0, dskip, wglu, bglu, ns, ls)

    pp = p_prompt.reshape(p_prompt.shape[0] * n * l, -1)
    ps = jnp.swapaxes(p_sample, 1, 2).reshape(p_sample.shape[0] * ls * ns, -1)
    zero_head = jnp.zeros((n, nc2, head_rows_p, FF_CHUNK), F32)
    conv_s_tm = jnp.swapaxes(state_ffn_conv, 1, 2).reshape(state_ffn_conv.shape[0], 1, (CONV_WIDTH - 1) * ns, ff2)

    fw = _ffn_weights(g_ffn, w_up, conv_w, conv_b, w_down, g_ple, w_ple_gate, w_ple_in)

    def ffn_layer(i, hp, hs, attn_p=None, attn_s=None):
        hp, tail_p = _ffn(hp, pp, i, zero_head, fw, ffn_rows, n, 1, attn_p)
        hs, tail_s = _ffn(hs, ps, i, _chunk_major(conv_s_tm[i], nc2), fw, ls * ns, 1, ns, attn_s)
        conv_p = _row_major(tail_p)[:, head_rows_p - (CONV_WIDTH - 1):, :]
        conv_s = jnp.swapaxes(_row_major(tail_s).reshape(CONV_WIDTH - 1, ns, ff2), 0, 1)
        return hp, hs, conv_p, conv_s

    hp, hs, conv_p0, conv_s0 = ffn_layer(0, hp, hs)

    qw = dict(g_kv=g_kv.reshape(1, d), g_mix=g_mix[1].reshape(1, d), w_k=w_k.astype(BF16), w_v=w_v.astype(BF16),
              w_q=w_q[0].astype(BF16), k_norm=jnp.tile(k_norm, hk // HEAD_DIM).reshape(1, hk),
              q_norm=jnp.tile(q_norm[0], n_heads).reshape(1, hq),
              k_pool=_head_pool_matrices(hk), q_pool=_head_pool_matrices(hq))
    q_p, k_p, v_p, kx_p, vx_p = _qkv(hp, qw, ffn_rows, True)
    q_s, k_s, v_s = _qkv(hs, qw, ls * ns, False)

    wo = w_o[0].astype(BF16)
    bias_p = _bias_table(rel_bias, WINDOW, WINDOW, 2 * WINDOW)
    bias_first = jnp.where(jnp.arange(2 * WINDOW) >= WINDOW, bias_p, -jnp.inf)
    o_p = _attn_prompt(q_p, kx_p, vx_p, jnp.stack([bias_first, bias_p]), sinks[0].astype(F32), n, l)

    lkp = 2 * WINDOW
    new_rows = 16
    pad_new = lambda a: jnp.pad(tm(a.reshape(ls, ns, hk)), ((0, 0), (0, new_rows - ls), (0, 0)))
    kc = cache_k_win.reshape(ns, lb, hk)
    vc = cache_v_win.reshape(ns, lb, hk)
    bias_s = _bias_table(rel_bias, lb, ls, lb + ls)
    bias_s = jnp.pad(bias_s, ((0, 0), (0, 0), (0, lkp - lb - ls)), constant_values=-jnp.inf)
    bias_s = bias_s.reshape(n_heads * ls, lkp)
    sink_s = jnp.repeat(sinks[0].astype(F32), ls).reshape(n_heads * ls, 1)
    q5 = jnp.transpose(q_s.reshape(ls, ns, N_KV_HEADS, rep, HEAD_DIM), (1, 2, 3, 0, 4))
    q_blk = jnp.einsum('sgrtd,gh->sgrthd', q5, jnp.eye(N_KV_HEADS, dtype=q5.dtype))
    q_blk = q_blk.reshape(ns, n_heads * ls, hk)
    o4, k_win_s, v_win_s = _attn_sample(q_blk, kc, vc, pad_new(k_s), pad_new(v_s), bias_s, sink_s, ls, 8)
    o_s = jnp.transpose(o4.reshape(ns, rep, ls, N_KV_HEADS, HEAD_DIM), (2, 0, 3, 1, 4)).reshape(ls * ns, hq)
    o_s = o_s.astype(BF16)

    hp, hs, conv_p1, conv_s1 = ffn_layer(1, hp, hs, (o_p, wo), (o_s, wo))

    y_prompt = hp.reshape(n, l, d)
    y_sample = tm(hs.reshape(ls, ns, d))
    ssm_shape = (1, -1, n_groups, n_state)
    kvh_shape = (-1, lb, N_KV_HEADS, HEAD_DIM)
    k_win_p = k_p.reshape(n, l, hk)[:, l - WINDOW:].reshape(n, WINDOW, N_KV_HEADS, HEAD_DIM)
    v_win_p = v_p.reshape(n, l, hk)[:, l - WINDOW:].reshape(n, WINDOW, N_KV_HEADS, HEAD_DIM)
    return (y_prompt, y_sample,
            sre_p.reshape(ssm_shape), sim_p.reshape(ssm_shape),
            sre_s.reshape(ssm_shape), sim_s.reshape(ssm_shape),
            jnp.stack([conv_p0, conv_p1]), jnp.stack([conv_s0, conv_s1]),
            k_win_p, v_win_p, k_win_s.reshape(kvh_shape), v_win_s.reshape(kvh_shape))
```

```python
import functools
import math

import jax
import jax.numpy as jnp
from jax import lax
from jax.experimental import pallas as pl
from jax.experimental.pallas import tpu as pltpu

F32 = jnp.float32
BF16 = jnp.bfloat16

EPS = 1e-6
SSM_GROUP = 16
SSM_STATE = 64
HEAD_DIM = 64
N_KV_HEADS = 4
WINDOW = 128
NUM_BUCKETS = 32
MAX_DISTANCE = 128
CONV_WIDTH = 3

LANES = 128
SUBLANES = 8
MXU_DIM = 256
VMEM_LIMIT = 56 * 1024 * 1024

SCAN_COLS = 1024
FF_CHUNK = 256


def _dot(a, b):
    return jnp.dot(a, b, preferred_element_type=F32)


def _dot_nt(a, b):
    return lax.dot_general(a, b, (((1,), (1,)), ((), ())), preferred_element_type=F32)


def _rms(x, g):
    ms = jnp.mean(x * x, axis=-1, keepdims=True)
    return x * lax.rsqrt(ms + EPS) * g


def _head_rms(x, pool, spread, g):
    ms = _dot((x * x).astype(BF16), pool)
    scale = lax.rsqrt(ms + EPS)
    hi = scale.astype(BF16)
    lo = (scale - hi.astype(F32)).astype(BF16)
    return x * (_dot(hi, spread) + _dot(lo, spread)) * g


def _const_spec(shape):
    nd = len(shape)
    return pl.BlockSpec(shape, lambda *_: (0,) * nd, pipeline_mode=pl.Buffered(1))


def _b_project(ub, bre_ref, bim_ref, bure, buim):
    n_kt = bre_ref.shape[0]
    kw = bre_ref.shape[1]
    nw = bre_ref.shape[2]
    for kt in range(n_kt):
        lhs = ub[:, kt * kw:(kt + 1) * kw]
        bure[:, kt * nw:(kt + 1) * nw] = _dot(lhs, bre_ref[kt])
        buim[:, kt * nw:(kt + 1) * nw] = _dot(lhs, bim_ref[kt])


def _scan(bure, buim, are_ref, aim_ref, n_seq, n_steps, init_fn, final_fn, store):
    nst = bure.shape[1]
    for cb in range(nst // SCAN_COLS):
        cs = slice(cb * SCAN_COLS, (cb + 1) * SCAN_COLS)
        ar = jnp.broadcast_to(are_ref[:, cs], (SUBLANES, SCAN_COLS))
        ai = jnp.broadcast_to(aim_ref[:, cs], (SUBLANES, SCAN_COLS))

        def group(g, _, cs=cs, ar=ar, ai=ai):
            r0 = pl.multiple_of(g * SUBLANES, SUBLANES)

            def step(k, carry):
                hr, hi = carry
                row = pl.multiple_of(k * n_seq + r0, SUBLANES)
                br = bure[pl.ds(row, SUBLANES), cs]
                bi = buim[pl.ds(row, SUBLANES), cs]
                nr = ar * hr - ai * hi + br
                ni = ar * hi + ai * hr + bi
                if store:
                    bure[pl.ds(row, SUBLANES), cs] = nr
                    buim[pl.ds(row, SUBLANES), cs] = ni
                return nr, ni

            hr, hi = lax.fori_loop(0, n_steps, step, init_fn(r0, cs), unroll=min(n_steps, 4))
            final_fn(r0, cs, hr, hi)
            return 0

        if n_seq == SUBLANES:
            group(0, 0)
        else:
            lax.fori_loop(0, n_seq // SUBLANES, group, 0)


def _c_project_glu(x, u, bure, buim, cre_ref, ncim_ref, dskip_ref, wglu_ref, bglu_ref):
    d = x.shape[1]
    n_blk = cre_ref.shape[0]
    kw = cre_ref.shape[1]
    ys = []
    for m in range(n_blk):
        hr = bure[:, m * kw:(m + 1) * kw].astype(BF16)
        hi = buim[:, m * kw:(m + 1) * kw].astype(BF16)
        ys.append(_dot(hr, cre_ref[m]) + _dot(hi, ncim_ref[m]))
    y = jnp.concatenate(ys, axis=1) + dskip_ref[...] * u
    z = jax.nn.gelu(y).astype(BF16)
    gl = _dot(z, wglu_ref[...]) + bglu_ref[...]
    return x + gl[:, :d] * jax.nn.sigmoid(gl[:, d:])


def _ssm_prompt_kernel(x_ref, gmix_ref, are_ref, aim_ref, apw_re_ref, apw_im_ref, bw_ref, cre_ref, ncim_ref,
                       dskip_ref, wglu_ref, bglu_ref,
                       out_ref, sre_ref, sim_ref,
                       slab, xp, ub, us, bu_a, bu_b, zb, gl, ends, hin, car, *, n_steps, pitch):
    g_step = pl.program_id(1)
    for parity, (bu_next, bu_cur) in enumerate(((bu_a, bu_b), (bu_b, bu_a))):
        pl.when(g_step % 2 == parity)(functools.partial(
            _ssm_prompt_step, parity, g_step, x_ref, gmix_ref, are_ref, aim_ref, apw_re_ref, apw_im_ref, bw_ref,
            cre_ref, ncim_ref, dskip_ref, wglu_ref, bglu_ref, out_ref, sre_ref, sim_ref,
            slab, xp, ub, us, bu_next, bu_cur, zb, gl, ends, hin, car, n_steps, pitch))


def _ssm_prompt_step(nxt, g_step, x_ref, gmix_ref, are_ref, aim_ref, apw_re_ref, apw_im_ref, bw_ref, cre_ref,
                     ncim_ref, dskip_ref, wglu_ref, bglu_ref, out_ref, sre_ref, sim_ref,
                     slab, xp, ub, us, bu_next, bu_cur, zb, gl, ends, hin, car, n_steps, pitch):
    cur = 1 - nxt
    n_slab = slab.shape[0]
    d = xp.shape[2]
    n_piece = bu_cur.shape[1]
    blk_pieces = SCAN_COLS // MXU_DIM
    n_blk = n_piece // blk_pieces
    trips = 4
    spt = n_steps // trips
    n_cblk = cre_ref.shape[0]
    n_gtile = wglu_ref.shape[0]
    assert 2 * n_piece == n_blk * trips * 2
    assert 2 * n_gtile == n_blk * trips
    assert bw_ref.shape[2] == MXU_DIM and 2 * n_cblk == n_piece

    def lanes(q):
        return slice(q * MXU_DIM, (q + 1) * MXU_DIM)

    for j in range(SUBLANES):
        for c in range(n_slab):
            slab[c, j * pitch:j * pitch + n_steps, :] = x_ref[j * n_steps:(j + 1) * n_steps,
                                                             c * LANES:(c + 1) * LANES]

    def gather(k, _):
        r0 = pl.multiple_of(k * SUBLANES, SUBLANES)
        for c in range(n_slab):
            xp[nxt, pl.ds(r0, SUBLANES), c * LANES:(c + 1) * LANES] = slab[c, pl.ds(k, SUBLANES, stride=pitch), :]
        return 0

    lax.fori_loop(0, n_steps, gather, 0, unroll=4)
    u = _rms(xp[nxt], gmix_ref[...])
    ub[nxt] = u.astype(BF16)
    for c in range(n_slab):
        us[nxt, c] = u[:, c * LANES:(c + 1) * LANES]

    def run_pass(bu, store, init_fn, end_fn, work_fn):
        for blk in range(n_blk):
            coef = [(jnp.broadcast_to(are_ref[:, lanes(blk * blk_pieces + nt)], (SUBLANES, MXU_DIM)),
                     jnp.broadcast_to(aim_ref[:, lanes(blk * blk_pieces + nt)], (SUBLANES, MXU_DIM)))
                    for nt in range(blk_pieces)]

            def trip(i, state, blk=blk, coef=coef):
                if work_fn is not None:
                    work_fn(blk, i)
                state = list(state)
                for s in range(spt):
                    row = (i * spt + s) * SUBLANES
                    for nt in range(blk_pieces):
                        q = blk * blk_pieces + nt
                        ar, ai = coef[nt]
                        hr, hi = state[2 * nt], state[2 * nt + 1]
                        nr = ar * hr - ai * hi + bu[0, q, pl.ds(row, SUBLANES), :]
                        ni = ar * hi + ai * hr + bu[1, q, pl.ds(row, SUBLANES), :]
                        if store:
                            bu[0, q, pl.ds(row, SUBLANES), :] = nr
                            bu[1, q, pl.ds(row, SUBLANES), :] = ni
                        state[2 * nt], state[2 * nt + 1] = nr, ni
                return tuple(state)

            state = init_fn(blk)
            for i in range(trips):
                state = trip(i, state)
            end_fn(blk, state)

    def zero_init(blk):
        return tuple(jnp.zeros((SUBLANES, MXU_DIM), F32) for _ in range(2 * blk_pieces))

    def keep_ends(blk, state):
        for nt in range(blk_pieces):
            ends[0, :, lanes(blk * blk_pieces + nt)] = state[2 * nt]
            ends[1, :, lanes(blk * blk_pieces + nt)] = state[2 * nt + 1]

    def true_init(blk):
        return tuple(hin[ri, :, lanes(blk * blk_pieces + nt)] for nt in range(blk_pieces) for ri in range(2))

    def b_project_slice(blk, i):
        lhs = ub[nxt, :, lanes(blk)]
        for w in range(2):
            j = 2 * i + w
            ri = j // blk_pieces
            q = blk * blk_pieces + j % blk_pieces
            bu_next[ri, q] = _dot(lhs, bw_ref[ri, q])

    def c_project_block(m):
        y = dskip_ref[m] * us[cur, m]
        for ri, c_ref in ((0, cre_ref), (1, ncim_ref)):
            for w in range(2):
                y = y + _dot(bu_cur[ri, 2 * m + w].astype(BF16), c_ref[m, w * MXU_DIM:(w + 1) * MXU_DIM, :])
        zb[m] = jax.nn.gelu(y).astype(BF16)

    def glu_slice(blk, i):
        t = blk * trips + i
        if t % 2 == 0:
            nt = t // 2
            z = jnp.concatenate([zb[c] for c in range(n_cblk)], axis=1)
            gl[nt] = _dot(z, wglu_ref[nt]) + bglu_ref[nt]

    def first_tile():
        car[...] = jnp.zeros_like(car)
        for blk in range(n_blk):
            lhs = ub[nxt, :, lanes(blk)]
            for ri in range(2):
                for nt in range(blk_pieces):
                    bu_next[ri, blk * blk_pieces + nt] = _dot(lhs, bw_ref[ri, blk * blk_pieces + nt])
        run_pass(bu_next, False, zero_init, keep_ends, None)

    if nxt == 0:
        pl.when(g_step == 0)(first_tile)

    @pl.when(g_step > 0)
    def _():
        hr = car[0]
        hi = car[1]
        apr = apw_re_ref[...]
        api = apw_im_ref[...]
        for j in range(SUBLANES):
            er = ends[0, j:j + 1, :]
            ei = ends[1, j:j + 1, :]
            hin[0, j:j + 1, :] = hr
            hin[1, j:j + 1, :] = hi
            hr, hi = apr * hr - api * hi + er, apr * hi + api * hr + ei
        car[0] = hr
        car[1] = hi
        sre_ref[...] = hr
        sim_ref[...] = hi

        run_pass(bu_cur, True, true_init, lambda *_: None, b_project_slice)

    @pl.when(g_step > 0)
    def _():
        for m in range(n_cblk):
            c_project_block(m)

    @pl.when(g_step > 0)
    def _():
        run_pass(bu_next, False, zero_init, keep_ends, glu_slice)

        half = n_gtile // 2
        for c in range(half):
            xp[cur, :, lanes(c)] = xp[cur, :, lanes(c)] + gl[c] * jax.nn.sigmoid(gl[half + c])

        def scatter(k, _):
            r0 = pl.multiple_of(k * SUBLANES, SUBLANES)
            for c in range(n_slab):
                slab[c, pl.ds(k, SUBLANES, stride=pitch), :] = xp[cur, pl.ds(r0, SUBLANES), c * LANES:(c + 1) * LANES]
            return 0

        lax.fori_loop(0, n_steps, scatter, 0, unroll=4)
        for j in range(SUBLANES):
            for c in range(n_slab):
                out_ref[j * n_steps:(j + 1) * n_steps, c * LANES:(c + 1) * LANES] = slab[c, j * pitch:j * pitch + n_steps, :]


def _ssm_sample_kernel(x_ref, h0re_ref, h0im_ref, gmix_ref, are_ref, aim_ref, bre_ref, bim_ref, cre_ref,
                       ncim_ref, dskip_ref, wglu_ref, bglu_ref,
                       out_ref, sre_ref, sim_ref, bure, buim, *, n_seq, n_steps):
    x = x_ref[...]
    u = _rms(x, gmix_ref[...])
    _b_project(u.astype(BF16), bre_ref, bim_ref, bure, buim)

    def init(r0, cs):
        return h0re_ref[pl.ds(r0, SUBLANES), cs], h0im_ref[pl.ds(r0, SUBLANES), cs]

    def final(r0, cs, hr, hi):
        sre_ref[pl.ds(r0, SUBLANES), cs] = hr
        sim_ref[pl.ds(r0, SUBLANES), cs] = hi

    _scan(bure, buim, are_ref, aim_ref, n_seq, n_steps, init, final, store=True)
    out_ref[...] = _c_project_glu(x, u, bure, buim, cre_ref, ncim_ref, dskip_ref, wglu_ref, bglu_ref)


def _ssm_params(lam_re, lam_im, log_dt, b_re, b_im, c_re, c_im, n_pow):
    g, p = lam_re.shape
    lr = lam_re.astype(F32)
    li = lam_im.astype(F32)
    dt = jnp.exp(log_dt.astype(F32))[:, None]
    mag = jnp.exp(lr * dt)
    ang = li * dt
    ab_re = mag * jnp.cos(ang)
    ab_im = mag * jnp.sin(ang)
    den = lr * lr + li * li
    nr = ab_re - 1.0
    f_re = (nr * lr + ab_im * li) / den
    f_im = (ab_im * lr - nr * li) / den
    br = b_re.astype(F32)
    bi = b_im.astype(F32)
    bb_re = f_re[..., None] * br - f_im[..., None] * bi
    bb_im = f_re[..., None] * bi + f_im[..., None] * br

    c = bb_re.shape[2]
    gk = MXU_DIM // c
    eye_k = jnp.eye(gk, dtype=F32)

    def b_blocks(bb):
        bt = jnp.transpose(bb, (0, 2, 1)).reshape(g // gk, gk, c, p)
        return jnp.einsum('tgcp,gh->tgchp', bt, eye_k).reshape(g // gk, gk * c, gk * p).astype(BF16)

    gc = LANES // c
    eye_c = jnp.eye(gc, dtype=F32)

    def c_blocks(cc):
        ct = jnp.transpose(cc.astype(F32), (0, 2, 1)).reshape(g // gc, gc, p, c)
        return jnp.einsum('tgpc,gh->tgphc', ct, eye_c).reshape(g // gc, gc * p, gc * c).astype(BF16)

    pw_re = ab_re.reshape(1, g * p)
    pw_im = ab_im.reshape(1, g * p)
    for _ in range(n_pow.bit_length() - 1):
        pw_re, pw_im = pw_re * pw_re - pw_im * pw_im, 2.0 * pw_re * pw_im
    return dict(a_re=ab_re.reshape(1, g * p), a_im=ab_im.reshape(1, g * p),
                ap_re=pw_re, ap_im=pw_im,
                b_re=b_blocks(bb_re), b_im=b_blocks(bb_im),
                c_re=c_blocks(c_re), nc_im=c_blocks(-c_im))


def _ssm_prompt(x, sp, gmix, dskip, wglu, bglu, n_steps):
    n, l, d = x.shape
    nst = sp['a_re'].shape[1]
    rows = SUBLANES * n_steps
    pitch = n_steps + SUBLANES
    tiles = l // rows
    n_piece = nst // MXU_DIM
    kt, kw, nw = sp['b_re'].shape
    to_tiles = lambda b: jnp.transpose(b.reshape(kt, kw, nw // MXU_DIM, MXU_DIM), (0, 2, 1, 3)).reshape(
        n_piece, kw, MXU_DIM)
    bw = jnp.stack([to_tiles(sp['b_re']), to_tiles(sp['b_im'])])
    n_gtile = wglu.shape[1] // MXU_DIM
    wglu_t = jnp.transpose(wglu.reshape(d, n_gtile, MXU_DIM), (1, 0, 2))
    consts = [gmix, sp['a_re'], sp['a_im'], sp['ap_re'], sp['ap_im'], bw, sp['c_re'], sp['nc_im'],
              dskip.reshape(d // LANES, 1, LANES), wglu_t, bglu.reshape(n_gtile, 1, MXU_DIM)]
    out, sre, sim = pl.pallas_call(
        functools.partial(_ssm_prompt_kernel, n_steps=n_steps, pitch=pitch),
        grid=(n, tiles + 1),
        in_specs=[pl.BlockSpec((None, rows, d), lambda i, g: (i, jnp.minimum(g, tiles - 1), 0))]
        + [_const_spec(c.shape) for c in consts],
        out_specs=[pl.BlockSpec((None, rows, d), lambda i, g: (i, jnp.maximum(g - 1, 0), 0)),
                   pl.BlockSpec((None, 1, nst), lambda i, g: (i, 0, 0)),
                   pl.BlockSpec((None, 1, nst), lambda i, g: (i, 0, 0))],
        out_shape=[jax.ShapeDtypeStruct((n, l, d), F32),
                   jax.ShapeDtypeStruct((n, 1, nst), F32),
                   jax.ShapeDtypeStruct((n, 1, nst), F32)],
        scratch_shapes=[pltpu.VMEM((d // LANES, SUBLANES * pitch, LANES), F32),
                        pltpu.VMEM((2, rows, d), F32),
                        pltpu.VMEM((2, rows, d), BF16),
                        pltpu.VMEM((2, d // LANES, rows, LANES), F32),
                        pltpu.VMEM((2, n_piece, rows, MXU_DIM), F32),
                        pltpu.VMEM((2, n_piece, rows, MXU_DIM), F32),
                        pltpu.VMEM((d // LANES, rows, LANES), BF16),
                        pltpu.VMEM((n_gtile, rows, MXU_DIM), F32),
                        pltpu.VMEM((2, SUBLANES, nst), F32),
                        pltpu.VMEM((2, SUBLANES, nst), F32),
                        pltpu.VMEM((2, 1, nst), F32)],
        compiler_params=pltpu.CompilerParams(dimension_semantics=("arbitrary", "arbitrary"),
                                             vmem_limit_bytes=VMEM_LIMIT),
        name="ssm_prompt",
    )(x, *consts)
    return out, sre[:, 0], sim[:, 0]


def _ssm_sample(x_tm, h0re, h0im, sp, gmix, dskip, wglu, bglu, n_seq, n_steps):
    rows, d = x_tm.shape
    nst = sp['a_re'].shape[1]
    args = [x_tm, h0re, h0im, gmix, sp['a_re'], sp['a_im'], sp['b_re'], sp['b_im'], sp['c_re'], sp['nc_im'],
            dskip, wglu, bglu]
    return pl.pallas_call(
        functools.partial(_ssm_sample_kernel, n_seq=n_seq, n_steps=n_steps),
        grid=(1,),
        in_specs=[_const_spec(a.shape) for a in args],
        out_specs=[pl.BlockSpec((rows, d), lambda i: (0, 0)), pl.BlockSpec((n_seq, nst), lambda i: (0, 0)),
                   pl.BlockSpec((n_seq, nst), lambda i: (0, 0))],
        out_shape=[jax.ShapeDtypeStruct((rows, d), F32),
                   jax.ShapeDtypeStruct((n_seq, nst), F32),
                   jax.ShapeDtypeStruct((n_seq, nst), F32)],
        scratch_shapes=[pltpu.VMEM((rows, nst), F32), pltpu.VMEM((rows, nst), F32)],
        compiler_params=pltpu.CompilerParams(dimension_semantics=("arbitrary",),
                                             vmem_limit_bytes=VMEM_LIMIT),
        name="ssm_sample",
    )(*args)


def _ffn_kernel(*refs, rows, head_rows, shift, has_attn):
    if has_attn:
        h_ref, o_ref, wo_ref = refs[:3]
        refs = refs[3:]
    else:
        h_ref = refs[0]
        refs = refs[1:]
    (p_ref, head_ref, gffn_ref, wup_ref, cw_ref, cb_ref, wdown_ref, gple_ref, wgate_ref, win_ref,
     out_ref, tail_ref, acc, xn_ref, ext_a, ext_b) = refs
    n_chunk = wdown_ref.shape[0]
    up0 = head_rows
    up1 = head_rows + rows

    @pl.when(pl.program_id(1) == 0)
    def _():
        tail_ref[...] = head_ref[...]

    x = h_ref[...]
    if has_attn:
        x = x + _dot(o_ref[...], wo_ref[...])
    xn_ref[...] = _rms(x, gffn_ref[...]).astype(BF16)
    acc[...] = jnp.zeros_like(acc)

    n_slab = ext_a.shape[1]

    def up_project(c, ext):
        for half, cc in enumerate((c, n_chunk + c)):
            up = _dot(xn_ref[...], wup_ref[cc])
            for s in range(n_slab):
                ext[half, s, 0:up0, :] = tail_ref[cc, :, s * LANES:(s + 1) * LANES]
                ext[half, s, up0:up1, :] = up[:, s * LANES:(s + 1) * LANES]

    def conv(ext, half, cc):
        w = cw_ref[cc]
        b = cb_ref[cc]
        parts = []
        for s in range(n_slab):
            ls = slice(s * LANES, (s + 1) * LANES)
            t2 = ext[half, s, up0 - 2 * shift:up1 - 2 * shift, :]
            t1 = ext[half, s, up0 - shift:up1 - shift, :]
            tail_ref[cc, :, ls] = ext[half, s, rows:up1, :]
            r = b[:, ls] + t2 * w[0:1, ls]
            r = r + t1 * w[1:2, ls]
            parts.append(r + ext[half, s, up0:up1, :] * w[2:3, ls])
        return jnp.concatenate(parts, axis=1)

    def down_project(c, ext):
        cg = conv(ext, 0, c)
        cv = conv(ext, 1, n_chunk + c)
        act = (cg * jax.nn.sigmoid(cg) * cv).astype(BF16)
        acc[...] += _dot(act, wdown_ref[c])

    up_project(0, ext_a)

    def pair(i, _):
        c = 2 * i
        up_project(c + 1, ext_b)
        down_project(c, ext_a)
        up_project(c + 2, ext_a)
        down_project(c + 1, ext_b)
        return 0

    lax.fori_loop(0, (n_chunk - 1) // 2, pair, 0)
    if n_chunk % 2:
        down_project(n_chunk - 1, ext_a)
    else:
        up_project(n_chunk - 1, ext_b)
        down_project(n_chunk - 2, ext_a)
        down_project(n_chunk - 1, ext_b)

    h2 = x + acc[...]
    gate = jax.nn.sigmoid(_dot(_rms(h2, gple_ref[...]).astype(BF16), wgate_ref[...]))
    pe = _dot(p_ref[...].astype(BF16), win_ref[...])
    out_ref[...] = h2 + pe * gate


def _ffn_weights(g_ffn, w_up, conv_w, conv_b, w_down, g_ple, w_gate, w_in):
    depth, d, ff2 = w_up.shape
    nc2 = ff2 // FF_CHUNK
    return dict(
        g_ffn=g_ffn.reshape(depth, 1, d),
        w_up=jnp.transpose(w_up.astype(BF16).reshape(depth, d, nc2, FF_CHUNK), (0, 2, 1, 3)),
        conv_w=jnp.transpose(conv_w.reshape(depth, CONV_WIDTH, nc2, FF_CHUNK), (0, 2, 1, 3)),
        conv_b=conv_b.reshape(depth, nc2, 1, FF_CHUNK),
        w_down=w_down.astype(BF16).reshape(depth, nc2 // 2, FF_CHUNK, d),
        g_ple=g_ple.reshape(depth, 1, d),
        w_gate=w_gate.astype(BF16),
        w_in=w_in.astype(BF16),
    )


def _layer_spec(shape, layer):
    nd = len(shape)
    return pl.BlockSpec((None,) + tuple(shape[1:]), lambda *_: (layer,) + (0,) * (nd - 1),
                        pipeline_mode=pl.Buffered(1))


def _ffn(h, p, layer, head, fw, rows, n_seq_tiles, shift, attn=None):
    total, d = h.shape
    tiles = total // rows // n_seq_tiles
    nc2, head_rows = head.shape[1], head.shape[2]
    row_map = lambda i, t: (i * tiles + t, 0)
    p_map = lambda i, t: (layer * (total // rows) + i * tiles + t, 0)
    consts = [fw['g_ffn'], fw['w_up'], fw['conv_w'], fw['conv_b'], fw['w_down'], fw['g_ple'], fw['w_gate'],
              fw['w_in']]
    args = [h]
    in_specs = [pl.BlockSpec((rows, d), row_map)]
    if attn is not None:
        o, wo = attn
        args += [o, wo]
        in_specs += [pl.BlockSpec((rows, o.shape[1]), row_map), _const_spec(wo.shape)]
    args += [p, head] + consts
    in_specs += [pl.BlockSpec((rows, p.shape[1]), p_map),
                 pl.BlockSpec((None, nc2, head_rows, FF_CHUNK), lambda i, t: (i, 0, 0, 0))]
    in_specs += [_layer_spec(c.shape, layer) for c in consts]
    out, tail = pl.pallas_call(
        functools.partial(_ffn_kernel, rows=rows, head_rows=head_rows, shift=shift, has_attn=attn is not None),
        grid=(n_seq_tiles, tiles),
        in_specs=in_specs,
        out_specs=[pl.BlockSpec((rows, d), row_map),
                   pl.BlockSpec((None, nc2, head_rows, FF_CHUNK), lambda i, t: (i, 0, 0, 0))],
        out_shape=[jax.ShapeDtypeStruct((total, d), F32),
                   jax.ShapeDtypeStruct(head.shape, F32)],
        scratch_shapes=[pltpu.VMEM((rows, d), F32),
                        pltpu.VMEM((rows, d), BF16),
                        pltpu.VMEM((2, FF_CHUNK // LANES, head_rows + rows, LANES), F32),
                        pltpu.VMEM((2, FF_CHUNK // LANES, head_rows + rows, LANES), F32)],
        compiler_params=pltpu.CompilerParams(dimension_semantics=("arbitrary", "arbitrary"),
                                             vmem_limit_bytes=VMEM_LIMIT),
        name="ffn_ple",
    )(*args)
    return out, tail


def _qkv_kernel(*refs, paired):
    (h_ref, gkv_ref, gmix_ref, wk_ref, wv_ref, wq_ref, knorm_ref, qnorm_ref, kpool_ref, kspread_ref,
     qpool_ref, qspread_ref) = refs[:12]
    refs = refs[12:]
    x = h_ref[...]
    s = _rms(x, gkv_ref[...]).astype(BF16)
    k = _head_rms(_dot(s, wk_ref[...]), kpool_ref[...], kspread_ref[...], knorm_ref[...])
    v = _dot(s, wv_ref[...])
    xn = _rms(x, gmix_ref[...]).astype(BF16)
    q = _head_rms(_dot(xn, wq_ref[...]), qpool_ref[...], qspread_ref[...], qnorm_ref[...])
    if paired:
        place_ref, fill_ref, q_ref, k_ref, v_ref, kx_ref, vx_ref = refs
        kx_ref[...] = _dot(k.astype(BF16), place_ref[...]).astype(BF16)
        vx_ref[...] = (_dot(v.astype(BF16), place_ref[...]) + fill_ref[...]).astype(BF16)
    else:
        q_ref, k_ref, v_ref = refs
    k_ref[...] = k
    v_ref[...] = v
    q_ref[...] = (q * (HEAD_DIM ** -0.5)).astype(q_ref.dtype)


def _head_pool_matrices(n):
    member = (jnp.arange(n)[:, None] // HEAD_DIM) == jnp.arange(LANES)[None, :]
    return member.astype(BF16) * (1.0 / HEAD_DIM), member.T.astype(BF16)


def _pair_placement(hk):
    src = jnp.arange(hk)
    dst = jnp.arange(4 * hk)
    same_head = (src[:, None] // HEAD_DIM) == (dst[None, :] // (4 * HEAD_DIM))
    sub = (dst % (4 * HEAD_DIM)) // HEAD_DIM
    same_dim = (src[:, None] % HEAD_DIM) == (dst[None, :] % HEAD_DIM)
    place = same_head & same_dim & ((sub == 0) | (sub == 3))[None, :]
    fill = ((sub == 1) | (sub == 2)).astype(F32).reshape(1, 4 * hk)
    return place.astype(BF16), fill


def _qkv(h, qw, rows, paired):
    total, d = h.shape
    hk = qw['w_k'].shape[1]
    hq = qw['w_q'].shape[1]
    consts = [qw['g_kv'], qw['g_mix'], qw['w_k'], qw['w_v'], qw['w_q'], qw['k_norm'], qw['q_norm'],
              *qw['k_pool'], *qw['q_pool']]
    row_map = lambda i: (i, 0)
    out_specs = [pl.BlockSpec((rows, hq), row_map), pl.BlockSpec((rows, hk), row_map),
                 pl.BlockSpec((rows, hk), row_map)]
    out_shape = [jax.ShapeDtypeStruct((total, hq), BF16),
                 jax.ShapeDtypeStruct((total, hk), F32),
                 jax.ShapeDtypeStruct((total, hk), F32)]
    if paired:
        consts += list(_pair_placement(hk))
        out_specs += [pl.BlockSpec((rows, 4 * hk), row_map)] * 2
        out_shape += [jax.ShapeDtypeStruct((total, 4 * hk), BF16)] * 2
    return pl.pallas_call(
        functools.partial(_qkv_kernel, paired=paired),
        grid=(total // rows,),
        in_specs=[pl.BlockSpec((rows, d), row_map)] + [_const_spec(c.shape) for c in consts],
        out_specs=out_specs,
        out_shape=out_shape,
        compiler_params=pltpu.CompilerParams(dimension_semantics=("arbitrary",),
                                             vmem_limit_bytes=VMEM_LIMIT),
        name="qkv",
    )(h, *consts)


def _t5_bucket(dist):
    max_exact = NUM_BUCKETS // 2
    df = jnp.maximum(dist, 1).astype(F32)
    large = max_exact + (jnp.log(df / max_exact) / math.log(MAX_DISTANCE / max_exact)
                         * (NUM_BUCKETS - max_exact)).astype(jnp.int32)
    return jnp.where(dist < max_exact, dist, jnp.minimum(large, NUM_BUCKETS - 1))


def _bias_table(rel_bias, lb, qb, lk):
    dist = lb + jnp.arange(qb)[:, None] - jnp.arange(lk)[None, :]
    per_dist = rel_bias[_t5_bucket(jnp.arange(WINDOW + 1))].astype(F32)
    onehot = (jnp.clip(dist, 0, WINDOW)[..., None] == jnp.arange(WINDOW + 1)).astype(F32)
    bias = jnp.einsum('qkd,dh->hqk', onehot, per_dist, precision=lax.Precision.HIGHEST)
    ok = (dist >= 0) & (dist <= WINDOW)
    return jnp.where(ok[None], bias, -jnp.inf)


def _softmax_pv(s, sink, v):
    m = jnp.maximum(jnp.max(s, axis=-1, keepdims=True), sink)
    pr = jnp.exp(s - m)
    den = jnp.sum(pr, axis=-1, keepdims=True) + jnp.exp(sink - m)
    return _dot(pr.astype(BF16), v) / den


def _attn_prompt_kernel(sink_ref, q_ref, kx_ref, vx_ref, bias_ref, o_ref, kx_prev, vx_prev):
    qb = q_ref.shape[0]
    n_heads = bias_ref.shape[0]
    rep = n_heads // N_KV_HEADS
    grp = 4 * HEAD_DIM

    @pl.when(pl.program_id(1) == 0)
    def _():
        kx_prev[...] = jnp.zeros_like(kx_prev)
        vx_prev[...] = jnp.zeros_like(vx_prev)

    low = lax.broadcasted_iota(jnp.int32, (qb, LANES), 1) < HEAD_DIM
    for g in range(N_KV_HEADS):
        gs = slice(g * grp, (g + 1) * grp)
        kx = jnp.concatenate([kx_prev[:, gs], kx_ref[:, gs]], axis=0)
        vx = jnp.concatenate([vx_prev[:, gs], vx_ref[:, gs]], axis=0)
        for pair in range(rep // 2):
            h0 = g * rep + 2 * pair
            ls = slice(h0 * HEAD_DIM, h0 * HEAD_DIM + LANES)
            qp = q_ref[:, ls]
            res = []
            for h, kh in ((h0, kx[:, :LANES]), (h0 + 1, kx[:, LANES:])):
                s = _dot_nt(qp, kh) + bias_ref[h]
                m = jnp.maximum(jnp.max(s, axis=-1, keepdims=True), sink_ref[h])
                res.append((_dot(jnp.exp(s - m).astype(BF16), vx), jnp.exp(sink_ref[h] - m)))
            (ra, ea), (rb, eb) = res
            num = jnp.where(low, ra[:, :LANES], rb[:, LANES:])
            den = jnp.where(low, ra[:, LANES:], rb[:, :LANES]) + jnp.where(low, ea, eb)
            o_ref[:, ls] = (num / den).astype(o_ref.dtype)
    kx_prev[...] = kx_ref[...]
    vx_prev[...] = vx_ref[...]


def _attn_prompt(q, kx, vx, bias, sinks, n, l):
    hq = q.shape[1]
    wx = kx.shape[1]
    qb = WINDOW
    nb = l // qb
    cur = lambda i, b, *_: (i * nb + b, 0)
    return pl.pallas_call(
        _attn_prompt_kernel,
        grid_spec=pltpu.PrefetchScalarGridSpec(
            num_scalar_prefetch=1,
            grid=(n, nb),
            in_specs=[pl.BlockSpec((qb, hq), cur), pl.BlockSpec((qb, wx), cur), pl.BlockSpec((qb, wx), cur),
                      pl.BlockSpec((None,) + bias.shape[1:], lambda i, b, *_: (jnp.minimum(b, 1), 0, 0, 0))],
            out_specs=pl.BlockSpec((qb, hq), cur),
            scratch_shapes=[pltpu.VMEM((qb, wx), BF16), pltpu.VMEM((qb, wx), BF16)]),
        out_shape=jax.ShapeDtypeStruct((n * l, hq), BF16),
        compiler_params=pltpu.CompilerParams(dimension_semantics=("arbitrary", "arbitrary"),
                                             vmem_limit_bytes=VMEM_LIMIT),
        name="attn_prompt",
    )(sinks, q, kx, vx, bias)


def _attn_sample_kernel(sink_ref, q_ref, kc_ref, vc_ref, kn_ref, vn_ref, bias_ref,
                        o_ref, kwin_ref, vwin_ref, kk, vv, *, n_new):
    lb = kc_ref.shape[1]
    new_rows = kn_ref.shape[1]
    rows = q_ref.shape[1]
    hk = kc_ref.shape[2]
    grp = rows // N_KV_HEADS
    kk[...] = jnp.zeros_like(kk)
    vv[...] = jnp.zeros_like(vv)
    lane_head = lax.broadcasted_iota(jnp.int32, (grp, hk), 1) // HEAD_DIM
    for s_i in range(q_ref.shape[0]):
        kc = kc_ref[s_i]
        vc = vc_ref[s_i]
        kn = kn_ref[s_i]
        vn = vn_ref[s_i]
        kwin_ref[s_i, 0:lb - n_new, :] = kc[n_new:, :]
        kwin_ref[s_i, lb - n_new:lb, :] = kn[0:n_new, :]
        vwin_ref[s_i, 0:lb - n_new, :] = vc[n_new:, :]
        vwin_ref[s_i, lb - n_new:lb, :] = vn[0:n_new, :]
        kk[0:lb, :] = kc.astype(BF16)
        kk[lb:lb + new_rows, :] = kn.astype(BF16)
        vv[0:lb, :] = vc.astype(BF16)
        vv[lb:lb + new_rows, :] = vn.astype(BF16)
        s = _dot_nt(q_ref[s_i], kk[...]) + bias_ref[...]
        pv = _softmax_pv(s, sink_ref[...], vv[...])
        o = jnp.zeros((grp, hk), F32)
        for g in range(N_KV_HEADS):
            o = o + jnp.where(lane_head == g, pv[g * grp:(g + 1) * grp, :], 0.0)
        o_ref[s_i] = o


def _attn_sample(q_blk, kc, vc, kn, vn, bias, sink, n_new, block):
    ns, rows, hk = q_blk.shape
    lb = kc.shape[1]
    lkp = bias.shape[1]
    grp = rows // N_KV_HEADS
    per_s = lambda i: (i, 0, 0)
    return pl.pallas_call(
        functools.partial(_attn_sample_kernel, n_new=n_new),
        grid=(ns // block,),
        in_specs=[_const_spec(sink.shape),
                  pl.BlockSpec((block, rows, hk), per_s),
                  pl.BlockSpec((block, lb, hk), per_s), pl.BlockSpec((block, lb, hk), per_s),
                  pl.BlockSpec((block,) + kn.shape[1:], per_s), pl.BlockSpec((block,) + vn.shape[1:], per_s),
                  _const_spec(bias.shape)],
        out_specs=[pl.BlockSpec((block, grp, hk), per_s),
                   pl.BlockSpec((block, lb, hk), per_s), pl.BlockSpec((block, lb, hk), per_s)],
        out_shape=[jax.ShapeDtypeStruct((ns, grp, hk), F32),
                   jax.ShapeDtypeStruct((ns, lb, hk), F32),
                   jax.ShapeDtypeStruct((ns, lb, hk), F32)],
        scratch_shapes=[pltpu.VMEM((lkp, hk), BF16), pltpu.VMEM((lkp, hk), BF16)],
        compiler_params=pltpu.CompilerParams(dimension_semantics=("arbitrary",),
                                             vmem_limit_bytes=VMEM_LIMIT),
        name="attn_sample",
    )(sink, q_blk, kc, vc, kn, vn, bias)


def _chunk_major(a, nc2):
    lead = a.shape[:-2]
    r = a.shape[-2]
    a = a.reshape(lead + (r, nc2, FF_CHUNK))
    return jnp.swapaxes(a, -3, -2)


def _row_major(a):
    a = jnp.swapaxes(a, -3, -2)
    return a.reshape(a.shape[:-2] + (a.shape[-2] * a.shape[-1],))


def kernel(x_prompt, x_sample, state_ssm_re, state_ssm_im, state_ffn_conv, cache_k_win, cache_v_win, p_prompt, p_sample, g_mix, g_ffn, g_ple, ssm_lam_re, ssm_lam_im, ssm_log_dt, ssm_b_re, ssm_b_im, ssm_c_re, ssm_c_im, ssm_d, w_glu, b_glu, g_kv, w_k, w_v, k_norm, w_q, q_norm, sinks, w_o, rel_bias, w_up, conv_w, conv_b, w_down, w_ple_in, w_ple_gate):
    n, l, d = x_prompt.shape
    ns, ls, _ = x_sample.shape
    n_groups, n_state = ssm_lam_re.shape[1:]
    nst = n_groups * n_state
    ff2 = w_up.shape[2]
    nc2 = ff2 // FF_CHUNK
    hk = w_k.shape[1]
    hq = w_q.shape[2]
    n_heads = hq // HEAD_DIM
    rep = n_heads // N_KV_HEADS
    lb = cache_k_win.shape[1]

    ssm_steps = min(32, l // SUBLANES)
    ffn_rows = min(512, l)
    head_rows_p = SUBLANES
    tm = lambda a: jnp.swapaxes(a, 0, 1)

    sp = _ssm_params(ssm_lam_re[0], ssm_lam_im[0], ssm_log_dt[0], ssm_b_re[0], ssm_b_im[0], ssm_c_re[0],
                     ssm_c_im[0], ssm_steps)
    gmix0 = g_mix[0].reshape(1, d)
    dskip = ssm_d[0].reshape(1, d)
    wglu = w_glu[0].astype(BF16)
    bglu = b_glu[0].reshape(1, 2 * d)
    hp, sre_p, sim_p = _ssm_prompt(x_prompt, sp, gmix0, dskip, wglu, bglu, ssm_steps)
    hp = hp.reshape(n * l, d)
    xs_tm = tm(x_sample).reshape(ls * ns, d)
    hs, sre_s, sim_s = _ssm_sample(xs_tm, state_ssm_re[0].reshape(ns, nst), state_ssm_im[0].reshape(ns, nst),
                                   sp, gmix0, dskip, wglu, bglu, ns, ls)

    pp = p_prompt.reshape(p_prompt.shape[0] * n * l, -1)
    ps = jnp.swapaxes(p_sample, 1, 2).reshape(p_sample.shape[0] * ls * ns, -1)
    zero_head = jnp.zeros((n, nc2, head_rows_p, FF_CHUNK), F32)
    conv_s_tm = jnp.swapaxes(state_ffn_conv, 1, 2).reshape(state_ffn_conv.shape[0], 1, (CONV_WIDTH - 1) * ns, ff2)

    fw = _ffn_weights(g_ffn, w_up, conv_w, conv_b, w_down, g_ple, w_ple_gate, w_ple_in)

    def ffn_layer(i, hp, hs, attn_p=None, attn_s=None):
        hp, tail_p = _ffn(hp, pp, i, zero_head, fw, ffn_rows, n, 1, attn_p)
        hs, tail_s = _ffn(hs, ps, i, _chunk_major(conv_s_tm[i], nc2), fw, ls * ns, 1, ns, attn_s)
        conv_p = _row_major(tail_p)[:, head_rows_p - (CONV_WIDTH - 1):, :]
        conv_s = jnp.swapaxes(_row_major(tail_s).reshape(CONV_WIDTH - 1, ns, ff2), 0, 1)
        return hp, hs, conv_p, conv_s

    hp, hs, conv_p0, conv_s0 = ffn_layer(0, hp, hs)

    qw = dict(g_kv=g_kv.reshape(1, d), g_mix=g_mix[1].reshape(1, d), w_k=w_k.astype(BF16), w_v=w_v.astype(BF16),
              w_q=w_q[0].astype(BF16), k_norm=jnp.tile(k_norm, hk // HEAD_DIM).reshape(1, hk),
              q_norm=jnp.tile(q_norm[0], n_heads).reshape(1, hq),
              k_pool=_head_pool_matrices(hk), q_pool=_head_pool_matrices(hq))
    q_p, k_p, v_p, kx_p, vx_p = _qkv(hp, qw, ffn_rows, True)
    q_s, k_s, v_s = _qkv(hs, qw, ls * ns, False)

    wo = w_o[0].astype(BF16)
    bias_p = _bias_table(rel_bias, WINDOW, WINDOW, 2 * WINDOW)
    bias_first = jnp.where(jnp.arange(2 * WINDOW) >= WINDOW, bias_p, -jnp.inf)
    o_p = _attn_prompt(q_p, kx_p, vx_p, jnp.stack([bias_first, bias_p]), sinks[0].astype(F32), n, l)

    lkp = 2 * WINDOW
    new_rows = 16
    pad_new = lambda a: jnp.pad(tm(a.reshape(ls, ns, hk)), ((0, 0), (0, new_rows - ls), (0, 0)))
    kc = cache_k_win.reshape(ns, lb, hk)
    vc = cache_v_win.reshape(ns, lb, hk)
    bias_s = _bias_table(rel_bias, lb, ls, lb + ls)
    bias_s = jnp.pad(bias_s, ((0, 0), (0, 0), (0, lkp - lb - ls)), constant_values=-jnp.inf)
    bias_s = bias_s.reshape(n_heads * ls, lkp)
    sink_s = jnp.repeat(sinks[0].astype(F32), ls).reshape(n_heads * ls, 1)
    q5 = jnp.transpose(q_s.reshape(ls, ns, N_KV_HEADS, rep, HEAD_DIM), (1, 2, 3, 0, 4))
    q_blk = jnp.einsum('sgrtd,gh->sgrthd', q5, jnp.eye(N_KV_HEADS, dtype=q5.dtype))
    q_blk = q_blk.reshape(ns, n_heads * ls, hk)
    o4, k_win_s, v_win_s = _attn_sample(q_blk, kc, vc, pad_new(k_s), pad_new(v_s), bias_s, sink_s, ls, 8)
    o_s = jnp.transpose(o4.reshape(ns, rep, ls, N_KV_HEADS, HEAD_DIM), (2, 0, 3, 1, 4)).reshape(ls * ns, hq)
    o_s = o_s.astype(BF16)

    hp, hs, conv_p1, conv_s1 = ffn_layer(1, hp, hs, (o_p, wo), (o_s, wo))

    y_prompt = hp.reshape(n, l, d)
    y_sample = tm(hs.reshape(ls, ns, d))
    ssm_shape = (1, -1, n_groups, n_state)
    kvh_shape = (-1, lb, N_KV_HEADS, HEAD_DIM)
    k_win_p = k_p.reshape(n, l, hk)[:, l - WINDOW:].reshape(n, WINDOW, N_KV_HEADS, HEAD_DIM)
    v_win_p = v_p.reshape(n, l, hk)[:, l - WINDOW:].reshape(n, WINDOW, N_KV_HEADS, HEAD_DIM)
    return (y_prompt, y_sample,
            sre_p.reshape(ssm_shape), sim_p.reshape(ssm_shape),
            sre_s.reshape(ssm_shape), sim_s.reshape(ssm_shape),
            jnp.stack([conv_p0, conv_p1]), jnp.stack([conv_s0, conv_s1]),
            k_win_p, v_win_p, k_win_s.reshape(kvh_shape), v_win_s.reshape(kvh_shape))
```

```python
import functools
import math

import jax
import jax.numpy as jnp
from jax import lax
from jax.experimental import pallas as pl
from jax.experimental.pallas import tpu as pltpu

F32 = jnp.float32
BF16 = jnp.bfloat16

EPS = 1e-6
SSM_GROUP = 16
SSM_STATE = 64
HEAD_DIM = 64
N_KV_HEADS = 4
WINDOW = 128
NUM_BUCKETS = 32
MAX_DISTANCE = 128
CONV_WIDTH = 3

LANES = 128
SUBLANES = 8
MXU_DIM = 256
VMEM_LIMIT = 56 * 1024 * 1024

SCAN_COLS = 1024
FF_CHUNK = 256


def _dot(a, b):
    return jnp.dot(a, b, preferred_element_type=F32)


def _dot_nt(a, b):
    return lax.dot_general(a, b, (((1,), (1,)), ((), ())), preferred_element_type=F32)


def _rms(x, g):
    ms = jnp.mean(x * x, axis=-1, keepdims=True)
    return x * lax.rsqrt(ms + EPS) * g


def _head_rms(x, pool, spread, g):
    ms = _dot((x * x).astype(BF16), pool)
    scale = lax.rsqrt(ms + EPS)
    hi = scale.astype(BF16)
    lo = (scale - hi.astype(F32)).astype(BF16)
    return x * (_dot(hi, spread) + _dot(lo, spread)) * g


def _const_spec(shape):
    nd = len(shape)
    return pl.BlockSpec(shape, lambda *_: (0,) * nd, pipeline_mode=pl.Buffered(1))


def _b_project(ub, bre_ref, bim_ref, bure, buim):
    n_kt = bre_ref.shape[0]
    kw = bre_ref.shape[1]
    nw = bre_ref.shape[2]
    for kt in range(n_kt):
        lhs = ub[:, kt * kw:(kt + 1) * kw]
        bure[:, kt * nw:(kt + 1) * nw] = _dot(lhs, bre_ref[kt])
        buim[:, kt * nw:(kt + 1) * nw] = _dot(lhs, bim_ref[kt])


def _scan(bure, buim, are_ref, aim_ref, n_seq, n_steps, init_fn, final_fn, store):
    nst = bure.shape[1]
    for cb in range(nst // SCAN_COLS):
        cs = slice(cb * SCAN_COLS, (cb + 1) * SCAN_COLS)
        ar = jnp.broadcast_to(are_ref[:, cs], (SUBLANES, SCAN_COLS))
        ai = jnp.broadcast_to(aim_ref[:, cs], (SUBLANES, SCAN_COLS))

        def group(g, _, cs=cs, ar=ar, ai=ai):
            r0 = pl.multiple_of(g * SUBLANES, SUBLANES)

            def step(k, carry):
                hr, hi = carry
                row = pl.multiple_of(k * n_seq + r0, SUBLANES)
                br = bure[pl.ds(row, SUBLANES), cs]
                bi = buim[pl.ds(row, SUBLANES), cs]
                nr = ar * hr - ai * hi + br
                ni = ar * hi + ai * hr + bi
                if store:
                    bure[pl.ds(row, SUBLANES), cs] = nr
                    buim[pl.ds(row, SUBLANES), cs] = ni
                return nr, ni

            hr, hi = lax.fori_loop(0, n_steps, step, init_fn(r0, cs), unroll=min(n_steps, 4))
            final_fn(r0, cs, hr, hi)
            return 0

        if n_seq == SUBLANES:
            group(0, 0)
        else:
            lax.fori_loop(0, n_seq // SUBLANES, group, 0)


def _c_project_glu(x, u, bure, buim, cre_ref, ncim_ref, dskip_ref, wglu_ref, bglu_ref):
    d = x.shape[1]
    n_blk = cre_ref.shape[0]
    kw = cre_ref.shape[1]
    ys = []
    for m in range(n_blk):
        hr = bure[:, m * kw:(m + 1) * kw].astype(BF16)
        hi = buim[:, m * kw:(m + 1) * kw].astype(BF16)
        ys.append(_dot(hr, cre_ref[m]) + _dot(hi, ncim_ref[m]))
    y = jnp.concatenate(ys, axis=1) + dskip_ref[...] * u
    z = jax.nn.gelu(y).astype(BF16)
    gl = _dot(z, wglu_ref[...]) + bglu_ref[...]
    return x + gl[:, :d] * jax.nn.sigmoid(gl[:, d:])


def _ssm_prompt_kernel(x_ref, gmix_ref, are_ref, aim_ref, apw_re_ref, apw_im_ref, bw_ref, cre_ref, ncim_ref,
                       dskip_ref, wglu_ref, bglu_ref,
                       out_ref, sre_ref, sim_ref,
                       slab, xp, ub, us, bu_a, bu_b, zb, gl, ends, hin, car, *, n_steps, pitch):
    g_step = pl.program_id(1)
    for parity, (bu_next, bu_cur) in enumerate(((bu_a, bu_b), (bu_b, bu_a))):
        pl.when(g_step % 2 == parity)(functools.partial(
            _ssm_prompt_step, parity, g_step, x_ref, gmix_ref, are_ref, aim_ref, apw_re_ref, apw_im_ref, bw_ref,
            cre_ref, ncim_ref, dskip_ref, wglu_ref, bglu_ref, out_ref, sre_ref, sim_ref,
            slab, xp, ub, us, bu_next, bu_cur, zb, gl, ends, hin, car, n_steps, pitch))


def _ssm_prompt_step(nxt, g_step, x_ref, gmix_ref, are_ref, aim_ref, apw_re_ref, apw_im_ref, bw_ref, cre_ref,
                     ncim_ref, dskip_ref, wglu_ref, bglu_ref, out_ref, sre_ref, sim_ref,
                     slab, xp, ub, us, bu_next, bu_cur, zb, gl, ends, hin, car, n_steps, pitch):
    cur = 1 - nxt
    n_slab = slab.shape[0]
    d = xp.shape[2]
    n_piece = bu_cur.shape[1]
    blk_pieces = SCAN_COLS // MXU_DIM
    n_blk = n_piece // blk_pieces
    trips = 4
    spt = n_steps // trips
    n_cblk = cre_ref.shape[0]
    n_gtile = wglu_ref.shape[0]
    assert 2 * n_piece == n_blk * trips * 2
    assert 2 * n_gtile == n_blk * trips
    assert bw_ref.shape[2] == MXU_DIM and 2 * n_cblk == n_piece

    def lanes(q):
        return slice(q * MXU_DIM, (q + 1) * MXU_DIM)

    for j in range(SUBLANES):
        for c in range(n_slab):
            slab[c, j * pitch:j * pitch + n_steps, :] = x_ref[j * n_steps:(j + 1) * n_steps,
                                                             c * LANES:(c + 1) * LANES]

    def gather(k, _):
        r0 = pl.multiple_of(k * SUBLANES, SUBLANES)
        for c in range(n_slab):
            xp[nxt, pl.ds(r0, SUBLANES), c * LANES:(c + 1) * LANES] = slab[c, pl.ds(k, SUBLANES, stride=pitch), :]
        return 0

    lax.fori_loop(0, n_steps, gather, 0, unroll=4)
    u = _rms(xp[nxt], gmix_ref[...])
    ub[nxt] = u.astype(BF16)
    for c in range(n_slab):
        us[nxt, c] = u[:, c * LANES:(c + 1) * LANES]

    def run_pass(bu, store, init_fn, end_fn, work_fn):
        for blk in range(n_blk):
            coef = [(jnp.broadcast_to(are_ref[:, lanes(blk * blk_pieces + nt)], (SUBLANES, MXU_DIM)),
                     jnp.broadcast_to(aim_ref[:, lanes(blk * blk_pieces + nt)], (SUBLANES, MXU_DIM)))
                    for nt in range(blk_pieces)]

            def trip(i, state, blk=blk, coef=coef):
                if work_fn is not None:
                    work_fn(blk, i)
                state = list(state)
                for s in range(spt):
                    row = (i * spt + s) * SUBLANES
                    for nt in range(blk_pieces):
                        q = blk * blk_pieces + nt
                        ar, ai = coef[nt]
                        hr, hi = state[2 * nt], state[2 * nt + 1]
                        nr = ar * hr - ai * hi + bu[0, q, pl.ds(row, SUBLANES), :]
                        ni = ar * hi + ai * hr + bu[1, q, pl.ds(row, SUBLANES), :]
                        if store:
                            bu[0, q, pl.ds(row, SUBLANES), :] = nr
                            bu[1, q, pl.ds(row, SUBLANES), :] = ni
                        state[2 * nt], state[2 * nt + 1] = nr, ni
                return tuple(state)

            state = init_fn(blk)
            for i in range(trips):
                state = trip(i, state)
            end_fn(blk, state)

    def zero_init(blk):
        return tuple(jnp.zeros((SUBLANES, MXU_DIM), F32) for _ in range(2 * blk_pieces))

    def keep_ends(blk, state):
        for nt in range(blk_pieces):
            ends[0, :, lanes(blk * blk_pieces + nt)] = state[2 * nt]
            ends[1, :, lanes(blk * blk_pieces + nt)] = state[2 * nt + 1]

    def true_init(blk):
        return tuple(hin[ri, :, lanes(blk * blk_pieces + nt)] for nt in range(blk_pieces) for ri in range(2))

    tiles_per_kt = 2 * blk_pieces

    def b_project_tile(t):
        kt, j = divmod(t, tiles_per_kt)
        ri, nt = divmod(j, blk_pieces)
        q = kt * blk_pieces + nt
        bu_next[ri, q] = _dot(ub[nxt, :, lanes(kt)], bw_ref[ri, q])

    def b_project_slice(blk, i):
        b_project_tile(blk * trips + i)

    def c_project_block(m):
        y = dskip_ref[m] * us[cur, m]
        for ri, c_ref in ((0, cre_ref), (1, ncim_ref)):
            for w in range(2):
                y = y + _dot(bu_cur[ri, 2 * m + w].astype(BF16), c_ref[m, w * MXU_DIM:(w + 1) * MXU_DIM, :])
        zb[m] = jax.nn.gelu(y).astype(BF16)

    def glu_slice(blk, i):
        t = blk * trips + i
        if t % 2 == 0:
            nt = t // 2
            z = jnp.concatenate([zb[c] for c in range(n_cblk)], axis=1)
            gl[nt] = _dot(z, wglu_ref[nt]) + bglu_ref[nt]

    def first_tile():
        car[...] = jnp.zeros_like(car)
        for blk in range(n_blk):
            lhs = ub[nxt, :, lanes(blk)]
            for ri in range(2):
                for nt in range(blk_pieces):
                    bu_next[ri, blk * blk_pieces + nt] = _dot(lhs, bw_ref[ri, blk * blk_pieces + nt])
        run_pass(bu_next, False, zero_init, keep_ends, None)

    if nxt == 0:
        pl.when(g_step == 0)(first_tile)

    @pl.when(g_step > 0)
    def _():
        hr = car[0]
        hi = car[1]
        apr = apw_re_ref[...]
        api = apw_im_ref[...]
        for j in range(SUBLANES):
            er = ends[0, j:j + 1, :]
            ei = ends[1, j:j + 1, :]
            hin[0, j:j + 1, :] = hr
            hin[1, j:j + 1, :] = hi
            hr, hi = apr * hr - api * hi + er, apr * hi + api * hr + ei
        car[0] = hr
        car[1] = hi
        sre_ref[...] = hr
        sim_ref[...] = hi

        run_pass(bu_cur, True, true_init, lambda *_: None, b_project_slice)

        done = n_blk * trips
        per_block = (2 * n_piece - done) // n_cblk
        for m in range(n_cblk):
            c_project_block(m)
            for w in range(per_block):
                b_project_tile(done + m * per_block + w)
        run_pass(bu_next, False, zero_init, keep_ends, glu_slice)

        half = n_gtile // 2
        for c in range(half):
            xp[cur, :, lanes(c)] = xp[cur, :, lanes(c)] + gl[c] * jax.nn.sigmoid(gl[half + c])

        def scatter(k, _):
            r0 = pl.multiple_of(k * SUBLANES, SUBLANES)
            for c in range(n_slab):
                slab[c, pl.ds(k, SUBLANES, stride=pitch), :] = xp[cur, pl.ds(r0, SUBLANES), c * LANES:(c + 1) * LANES]
            return 0

        lax.fori_loop(0, n_steps, scatter, 0, unroll=4)
        for j in range(SUBLANES):
            for c in range(n_slab):
                out_ref[j * n_steps:(j + 1) * n_steps, c * LANES:(c + 1) * LANES] = slab[c, j * pitch:j * pitch + n_steps, :]


def _ssm_sample_kernel(x_ref, h0re_ref, h0im_ref, gmix_ref, are_ref, aim_ref, bre_ref, bim_ref, cre_ref,
                       ncim_ref, dskip_ref, wglu_ref, bglu_ref,
                       out_ref, sre_ref, sim_ref, bure, buim, *, n_seq, n_steps):
    x = x_ref[...]
    u = _rms(x, gmix_ref[...])
    _b_project(u.astype(BF16), bre_ref, bim_ref, bure, buim)

    def init(r0, cs):
        return h0re_ref[pl.ds(r0, SUBLANES), cs], h0im_ref[pl.ds(r0, SUBLANES), cs]

    def final(r0, cs, hr, hi):
        sre_ref[pl.ds(r0, SUBLANES), cs] = hr
        sim_ref[pl.ds(r0, SUBLANES), cs] = hi

    _scan(bure, buim, are_ref, aim_ref, n_seq, n_steps, init, final, store=True)
    out_ref[...] = _c_project_glu(x, u, bure, buim, cre_ref, ncim_ref, dskip_ref, wglu_ref, bglu_ref)


def _ssm_params(lam_re, lam_im, log_dt, b_re, b_im, c_re, c_im, n_pow):
    g, p = lam_re.shape
    lr = lam_re.astype(F32)
    li = lam_im.astype(F32)
    dt = jnp.exp(log_dt.astype(F32))[:, None]
    mag = jnp.exp(lr * dt)
    ang = li * dt
    ab_re = mag * jnp.cos(ang)
    ab_im = mag * jnp.sin(ang)
    den = lr * lr + li * li
    nr = ab_re - 1.0
    f_re = (nr * lr + ab_im * li) / den
    f_im = (ab_im * lr - nr * li) / den
    br = b_re.astype(F32)
    bi = b_im.astype(F32)
    bb_re = f_re[..., None] * br - f_im[..., None] * bi
    bb_im = f_re[..., None] * bi + f_im[..., None] * br

    c = bb_re.shape[2]
    gk = MXU_DIM // c
    eye_k = jnp.eye(gk, dtype=F32)

    def b_blocks(bb):
        bt = jnp.transpose(bb, (0, 2, 1)).reshape(g // gk, gk, c, p)
        return jnp.einsum('tgcp,gh->tgchp', bt, eye_k).reshape(g // gk, gk * c, gk * p).astype(BF16)

    gc = LANES // c
    eye_c = jnp.eye(gc, dtype=F32)

    def c_blocks(cc):
        ct = jnp.transpose(cc.astype(F32), (0, 2, 1)).reshape(g // gc, gc, p, c)
        return jnp.einsum('tgpc,gh->tgphc', ct, eye_c).reshape(g // gc, gc * p, gc * c).astype(BF16)

    pw_re = ab_re.reshape(1, g * p)
    pw_im = ab_im.reshape(1, g * p)
    for _ in range(n_pow.bit_length() - 1):
        pw_re, pw_im = pw_re * pw_re - pw_im * pw_im, 2.0 * pw_re * pw_im
    return dict(a_re=ab_re.reshape(1, g * p), a_im=ab_im.reshape(1, g * p),
                ap_re=pw_re, ap_im=pw_im,
                b_re=b_blocks(bb_re), b_im=b_blocks(bb_im),
                c_re=c_blocks(c_re), nc_im=c_blocks(-c_im))


def _ssm_prompt(x, sp, gmix, dskip, wglu, bglu, n_steps):
    n, l, d = x.shape
    nst = sp['a_re'].shape[1]
    rows = SUBLANES * n_steps
    pitch = n_steps + SUBLANES
    tiles = l // rows
    n_piece = nst // MXU_DIM
    kt, kw, nw = sp['b_re'].shape
    to_tiles = lambda b: jnp.transpose(b.reshape(kt, kw, nw // MXU_DIM, MXU_DIM), (0, 2, 1, 3)).reshape(
        n_piece, kw, MXU_DIM)
    bw = jnp.stack([to_tiles(sp['b_re']), to_tiles(sp['b_im'])])
    n_gtile = wglu.shape[1] // MXU_DIM
    wglu_t = jnp.transpose(wglu.reshape(d, n_gtile, MXU_DIM), (1, 0, 2))
    consts = [gmix, sp['a_re'], sp['a_im'], sp['ap_re'], sp['ap_im'], bw, sp['c_re'], sp['nc_im'],
              dskip.reshape(d // LANES, 1, LANES), wglu_t, bglu.reshape(n_gtile, 1, MXU_DIM)]
    out, sre, sim = pl.pallas_call(
        functools.partial(_ssm_prompt_kernel, n_steps=n_steps, pitch=pitch),
        grid=(n, tiles + 1),
        in_specs=[pl.BlockSpec((None, rows, d), lambda i, g: (i, jnp.minimum(g, tiles - 1), 0))]
        + [_const_spec(c.shape) for c in consts],
        out_specs=[pl.BlockSpec((None, rows, d), lambda i, g: (i, jnp.maximum(g - 1, 0), 0)),
                   pl.BlockSpec((None, 1, nst), lambda i, g: (i, 0, 0)),
                   pl.BlockSpec((None, 1, nst), lambda i, g: (i, 0, 0))],
        out_shape=[jax.ShapeDtypeStruct((n, l, d), F32),
                   jax.ShapeDtypeStruct((n, 1, nst), F32),
                   jax.ShapeDtypeStruct((n, 1, nst), F32)],
        scratch_shapes=[pltpu.VMEM((d // LANES, SUBLANES * pitch, LANES), F32),
                        pltpu.VMEM((2, rows, d), F32),
                        pltpu.VMEM((2, rows, d), BF16),
                        pltpu.VMEM((2, d // LANES, rows, LANES), F32),
                        pltpu.VMEM((2, n_piece, rows, MXU_DIM), F32),
                        pltpu.VMEM((2, n_piece, rows, MXU_DIM), F32),
                        pltpu.VMEM((d // LANES, rows, LANES), BF16),
                        pltpu.VMEM((n_gtile, rows, MXU_DIM), F32),
                        pltpu.VMEM((2, SUBLANES, nst), F32),
                        pltpu.VMEM((2, SUBLANES, nst), F32),
                        pltpu.VMEM((2, 1, nst), F32)],
        compiler_params=pltpu.CompilerParams(dimension_semantics=("arbitrary", "arbitrary"),
                                             vmem_limit_bytes=VMEM_LIMIT),
        name="ssm_prompt",
    )(x, *consts)
    return out, sre[:, 0], sim[:, 0]


def _ssm_sample(x_tm, h0re, h0im, sp, gmix, dskip, wglu, bglu, n_seq, n_steps):
    rows, d = x_tm.shape
    nst = sp['a_re'].shape[1]
    args = [x_tm, h0re, h0im, gmix, sp['a_re'], sp['a_im'], sp['b_re'], sp['b_im'], sp['c_re'], sp['nc_im'],
            dskip, wglu, bglu]
    return pl.pallas_call(
        functools.partial(_ssm_sample_kernel, n_seq=n_seq, n_steps=n_steps),
        grid=(1,),
        in_specs=[_const_spec(a.shape) for a in args],
        out_specs=[pl.BlockSpec((rows, d), lambda i: (0, 0)), pl.BlockSpec((n_seq, nst), lambda i: (0, 0)),
                   pl.BlockSpec((n_seq, nst), lambda i: (0, 0))],
        out_shape=[jax.ShapeDtypeStruct((rows, d), F32),
                   jax.ShapeDtypeStruct((n_seq, nst), F32),
                   jax.ShapeDtypeStruct((n_seq, nst), F32)],
        scratch_shapes=[pltpu.VMEM((rows, nst), F32), pltpu.VMEM((rows, nst), F32)],
        compiler_params=pltpu.CompilerParams(dimension_semantics=("arbitrary",),
                                             vmem_limit_bytes=VMEM_LIMIT),
        name="ssm_sample",
    )(*args)


def _ffn_kernel(*refs, rows, head_rows, shift, has_attn):
    if has_attn:
        h_ref, o_ref, wo_ref = refs[:3]
        refs = refs[3:]
    else:
        h_ref = refs[0]
        refs = refs[1:]
    (p_ref, head_ref, gffn_ref, wup_ref, cw_ref, cb_ref, wdown_ref, gple_ref, wgate_ref, win_ref,
     out_ref, tail_ref, acc, xn_ref, ext_a, ext_b) = refs
    ff = wdown_ref.shape[0]
    n_chunk = ff // FF_CHUNK
    up0 = head_rows
    up1 = head_rows + rows

    @pl.when(pl.program_id(1) == 0)
    def _():
        tail_ref[...] = head_ref[...]

    x = h_ref[...]
    if has_attn:
        x = x + _dot(o_ref[...], wo_ref[...])
    xn_ref[...] = _rms(x, gffn_ref[...]).astype(BF16)
    acc[...] = jnp.zeros_like(acc)

    n_slab = ext_a.shape[1]

    def slab_cols(c, half, s):
        c0 = half * ff + c * FF_CHUNK + s * LANES
        return slice(c0, c0 + LANES)

    def up_project(c, ext):
        for half in range(2):
            c0 = half * ff + c * FF_CHUNK
            up = _dot(xn_ref[...], wup_ref[:, c0:c0 + FF_CHUNK])
            for s in range(n_slab):
                ext[half, s, 0:up0, :] = tail_ref[:, slab_cols(c, half, s)]
                ext[half, s, up0:up1, :] = up[:, s * LANES:(s + 1) * LANES]

    def conv(ext, half, c):
        parts = []
        for s in range(n_slab):
            ls = slab_cols(c, half, s)
            t2 = ext[half, s, up0 - 2 * shift:up1 - 2 * shift, :]
            t1 = ext[half, s, up0 - shift:up1 - shift, :]
            tail_ref[:, ls] = ext[half, s, rows:up1, :]
            r = cb_ref[:, ls] + t2 * cw_ref[0:1, ls]
            r = r + t1 * cw_ref[1:2, ls]
            parts.append(r + ext[half, s, up0:up1, :] * cw_ref[2:3, ls])
        return jnp.concatenate(parts, axis=1)

    def down_project(c, ext):
        cg = conv(ext, 0, c)
        cv = conv(ext, 1, c)
        act = (cg * jax.nn.sigmoid(cg) * cv).astype(BF16)
        acc[...] += _dot(act, wdown_ref[c * FF_CHUNK:(c + 1) * FF_CHUNK, :])

    bufs = (ext_a, ext_b)
    up_project(0, bufs[0])
    for c in range(n_chunk):
        if c + 1 < n_chunk:
            up_project(c + 1, bufs[(c + 1) % 2])
        down_project(c, bufs[c % 2])

    h2 = x + acc[...]
    gate = jax.nn.sigmoid(_dot(_rms(h2, gple_ref[...]).astype(BF16), wgate_ref[...]))
    pe = _dot(p_ref[...].astype(BF16), win_ref[...])
    out_ref[...] = h2 + pe * gate


def _ffn_weights(g_ffn, w_up, conv_w, conv_b, w_down, g_ple, w_gate, w_in):
    depth, d, ff2 = w_up.shape
    return dict(
        g_ffn=g_ffn.reshape(depth, 1, d),
        w_up=w_up.astype(BF16),
        conv_w=conv_w,
        conv_b=conv_b.reshape(depth, 1, ff2),
        w_down=w_down.astype(BF16),
        g_ple=g_ple.reshape(depth, 1, d),
        w_gate=w_gate.astype(BF16),
        w_in=w_in.astype(BF16),
    )


def _layer_spec(shape, layer):
    nd = len(shape)
    return pl.BlockSpec((None,) + tuple(shape[1:]), lambda *_: (layer,) + (0,) * (nd - 1),
                        pipeline_mode=pl.Buffered(1))


def _ffn(h, p, layer, head, fw, rows, n_seq_tiles, shift, attn=None):
    total, d = h.shape
    tiles = total // rows // n_seq_tiles
    head_rows, ff2 = head.shape[1], head.shape[2]
    row_map = lambda i, t: (i * tiles + t, 0)
    p_map = lambda i, t: (layer * (total // rows) + i * tiles + t, 0)
    consts = [fw['g_ffn'], fw['w_up'], fw['conv_w'], fw['conv_b'], fw['w_down'], fw['g_ple'], fw['w_gate'],
              fw['w_in']]
    args = [h]
    in_specs = [pl.BlockSpec((rows, d), row_map)]
    if attn is not None:
        o, wo = attn
        args += [o, wo]
        in_specs += [pl.BlockSpec((rows, o.shape[1]), row_map), _const_spec(wo.shape)]
    args += [p, head] + consts
    in_specs += [pl.BlockSpec((rows, p.shape[1]), p_map),
                 pl.BlockSpec((None, head_rows, ff2), lambda i, t: (i, 0, 0))]
    in_specs += [_layer_spec(c.shape, layer) for c in consts]
    out, tail = pl.pallas_call(
        functools.partial(_ffn_kernel, rows=rows, head_rows=head_rows, shift=shift, has_attn=attn is not None),
        grid=(n_seq_tiles, tiles),
        in_specs=in_specs,
        out_specs=[pl.BlockSpec((rows, d), row_map),
                   pl.BlockSpec((None, head_rows, ff2), lambda i, t: (i, 0, 0))],
        out_shape=[jax.ShapeDtypeStruct((total, d), F32),
                   jax.ShapeDtypeStruct(head.shape, F32)],
        scratch_shapes=[pltpu.VMEM((rows, d), F32),
                        pltpu.VMEM((rows, d), BF16),
                        pltpu.VMEM((2, FF_CHUNK // LANES, head_rows + rows, LANES), F32),
                        pltpu.VMEM((2, FF_CHUNK // LANES, head_rows + rows, LANES), F32)],
        compiler_params=pltpu.CompilerParams(dimension_semantics=("arbitrary", "arbitrary"),
                                             vmem_limit_bytes=VMEM_LIMIT),
        name="ffn_ple",
    )(*args)
    return out, tail


def _qkv_kernel(*refs, paired):
    (h_ref, gkv_ref, gmix_ref, wk_ref, wv_ref, wq_ref, knorm_ref, qnorm_ref, kpool_ref, kspread_ref,
     qpool_ref, qspread_ref) = refs[:12]
    refs = refs[12:]
    x = h_ref[...]
    s = _rms(x, gkv_ref[...]).astype(BF16)
    k = _head_rms(_dot(s, wk_ref[...]), kpool_ref[...], kspread_ref[...], knorm_ref[...])
    v = _dot(s, wv_ref[...])
    xn = _rms(x, gmix_ref[...]).astype(BF16)
    q = _head_rms(_dot(xn, wq_ref[...]), qpool_ref[...], qspread_ref[...], qnorm_ref[...])
    if paired:
        place_ref, fill_ref, q_ref, k_ref, v_ref, kx_ref, vx_ref = refs
        kx_ref[...] = _dot(k.astype(BF16), place_ref[...]).astype(BF16)
        vx_ref[...] = (_dot(v.astype(BF16), place_ref[...]) + fill_ref[...]).astype(BF16)
    else:
        q_ref, k_ref, v_ref = refs
    k_ref[...] = k
    v_ref[...] = v
    q_ref[...] = (q * (HEAD_DIM ** -0.5)).astype(q_ref.dtype)


def _head_pool_matrices(n):
    member = (jnp.arange(n)[:, None] // HEAD_DIM) == jnp.arange(LANES)[None, :]
    return member.astype(BF16) * (1.0 / HEAD_DIM), member.T.astype(BF16)


def _pair_placement(hk):
    src = jnp.arange(hk)
    dst = jnp.arange(4 * hk)
    same_head = (src[:, None] // HEAD_DIM) == (dst[None, :] // (4 * HEAD_DIM))
    sub = (dst % (4 * HEAD_DIM)) // HEAD_DIM
    same_dim = (src[:, None] % HEAD_DIM) == (dst[None, :] % HEAD_DIM)
    place = same_head & same_dim & ((sub == 0) | (sub == 3))[None, :]
    fill = ((sub == 1) | (sub == 2)).astype(F32).reshape(1, 4 * hk)
    return place.astype(BF16), fill


def _qkv(h, qw, rows, paired):
    total, d = h.shape
    hk = qw['w_k'].shape[1]
    hq = qw['w_q'].shape[1]
    consts = [qw['g_kv'], qw['g_mix'], qw['w_k'], qw['w_v'], qw['w_q'], qw['k_norm'], qw['q_norm'],
              *qw['k_pool'], *qw['q_pool']]
    row_map = lambda i: (i, 0)
    out_specs = [pl.BlockSpec((rows, hq), row_map), pl.BlockSpec((rows, hk), row_map),
                 pl.BlockSpec((rows, hk), row_map)]
    out_shape = [jax.ShapeDtypeStruct((total, hq), BF16),
                 jax.ShapeDtypeStruct((total, hk), F32),
                 jax.ShapeDtypeStruct((total, hk), F32)]
    if paired:
        consts += list(_pair_placement(hk))
        out_specs += [pl.BlockSpec((rows, 4 * hk), row_map)] * 2
        out_shape += [jax.ShapeDtypeStruct((total, 4 * hk), BF16)] * 2
    return pl.pallas_call(
        functools.partial(_qkv_kernel, paired=paired),
        grid=(total // rows,),
        in_specs=[pl.BlockSpec((rows, d), row_map)] + [_const_spec(c.shape) for c in consts],
        out_specs=out_specs,
        out_shape=out_shape,
        compiler_params=pltpu.CompilerParams(dimension_semantics=("arbitrary",),
                                             vmem_limit_bytes=VMEM_LIMIT),
        name="qkv",
    )(h, *consts)


def _t5_bucket(dist):
    max_exact = NUM_BUCKETS // 2
    df = jnp.maximum(dist, 1).astype(F32)
    large = max_exact + (jnp.log(df / max_exact) / math.log(MAX_DISTANCE / max_exact)
                         * (NUM_BUCKETS - max_exact)).astype(jnp.int32)
    return jnp.where(dist < max_exact, dist, jnp.minimum(large, NUM_BUCKETS - 1))


def _bias_table(rel_bias, lb, qb, lk):
    dist = lb + jnp.arange(qb)[:, None] - jnp.arange(lk)[None, :]
    per_dist = rel_bias[_t5_bucket(jnp.arange(WINDOW + 1))].astype(F32)
    onehot = (jnp.clip(dist, 0, WINDOW)[..., None] == jnp.arange(WINDOW + 1)).astype(F32)
    bias = jnp.einsum('qkd,dh->hqk', onehot, per_dist, precision=lax.Precision.HIGHEST)
    ok = (dist >= 0) & (dist <= WINDOW)
    return jnp.where(ok[None], bias, -jnp.inf)


def _softmax_pv(s, sink, v):
    m = jnp.maximum(jnp.max(s, axis=-1, keepdims=True), sink)
    pr = jnp.exp(s - m)
    den = jnp.sum(pr, axis=-1, keepdims=True) + jnp.exp(sink - m)
    return _dot(pr.astype(BF16), v) / den


def _attn_prompt_kernel(sink_ref, q_ref, kx_ref, vx_ref, bias_ref, o_ref, kx_prev, vx_prev):
    qb = q_ref.shape[0]
    n_heads = bias_ref.shape[0]
    rep = n_heads // N_KV_HEADS
    grp = 4 * HEAD_DIM

    @pl.when(pl.program_id(1) == 0)
    def _():
        kx_prev[...] = jnp.zeros_like(kx_prev)
        vx_prev[...] = jnp.zeros_like(vx_prev)

    low = lax.broadcasted_iota(jnp.int32, (qb, LANES), 1) < HEAD_DIM
    for g in range(N_KV_HEADS):
        gs = slice(g * grp, (g + 1) * grp)
        kx = jnp.concatenate([kx_prev[:, gs], kx_ref[:, gs]], axis=0)
        vx = jnp.concatenate([vx_prev[:, gs], vx_ref[:, gs]], axis=0)
        for pair in range(rep // 2):
            h0 = g * rep + 2 * pair
            ls = slice(h0 * HEAD_DIM, h0 * HEAD_DIM + LANES)
            qp = q_ref[:, ls]
            res = []
            for h, kh in ((h0, kx[:, :LANES]), (h0 + 1, kx[:, LANES:])):
                s = _dot_nt(qp, kh) + bias_ref[h]
                m = jnp.maximum(jnp.max(s, axis=-1, keepdims=True), sink_ref[h])
                res.append((_dot(jnp.exp(s - m).astype(BF16), vx), jnp.exp(sink_ref[h] - m)))
            (ra, ea), (rb, eb) = res
            num = jnp.where(low, ra[:, :LANES], rb[:, LANES:])
            den = jnp.where(low, ra[:, LANES:], rb[:, :LANES]) + jnp.where(low, ea, eb)
            o_ref[:, ls] = (num / den).astype(o_ref.dtype)
    kx_prev[...] = kx_ref[...]
    vx_prev[...] = vx_ref[...]


def _attn_prompt(q, kx, vx, bias, sinks, n, l):
    hq = q.shape[1]
    wx = kx.shape[1]
    qb = WINDOW
    nb = l // qb
    cur = lambda i, b, *_: (i * nb + b, 0)
    return pl.pallas_call(
        _attn_prompt_kernel,
        grid_spec=pltpu.PrefetchScalarGridSpec(
            num_scalar_prefetch=1,
            grid=(n, nb),
            in_specs=[pl.BlockSpec((qb, hq), cur), pl.BlockSpec((qb, wx), cur), pl.BlockSpec((qb, wx), cur),
                      pl.BlockSpec((None,) + bias.shape[1:], lambda i, b, *_: (jnp.minimum(b, 1), 0, 0, 0))],
            out_specs=pl.BlockSpec((qb, hq), cur),
            scratch_shapes=[pltpu.VMEM((qb, wx), BF16), pltpu.VMEM((qb, wx), BF16)]),
        out_shape=jax.ShapeDtypeStruct((n * l, hq), BF16),
        compiler_params=pltpu.CompilerParams(dimension_semantics=("arbitrary", "arbitrary"),
                                             vmem_limit_bytes=VMEM_LIMIT),
        name="attn_prompt",
    )(sinks, q, kx, vx, bias)


def _attn_sample_kernel(sink_ref, q_ref, kc_ref, vc_ref, kn_ref, vn_ref, bias_ref,
                        o_ref, kwin_ref, vwin_ref, kk, vv, *, n_new):
    lb = kc_ref.shape[1]
    new_rows = kn_ref.shape[1]
    rows = q_ref.shape[1]
    hk = kc_ref.shape[2]
    grp = rows // N_KV_HEADS
    kk[...] = jnp.zeros_like(kk)
    vv[...] = jnp.zeros_like(vv)
    lane_head = lax.broadcasted_iota(jnp.int32, (grp, hk), 1) // HEAD_DIM
    for s_i in range(q_ref.shape[0]):
        kc = kc_ref[s_i]
        vc = vc_ref[s_i]
        kn = kn_ref[s_i]
        vn = vn_ref[s_i]
        kwin_ref[s_i, 0:lb - n_new, :] = kc[n_new:, :]
        kwin_ref[s_i, lb - n_new:lb, :] = kn[0:n_new, :]
        vwin_ref[s_i, 0:lb - n_new, :] = vc[n_new:, :]
        vwin_ref[s_i, lb - n_new:lb, :] = vn[0:n_new, :]
        kk[0:lb, :] = kc.astype(BF16)
        kk[lb:lb + new_rows, :] = kn.astype(BF16)
        vv[0:lb, :] = vc.astype(BF16)
        vv[lb:lb + new_rows, :] = vn.astype(BF16)
        s = _dot_nt(q_ref[s_i], kk[...]) + bias_ref[...]
        pv = _softmax_pv(s, sink_ref[...], vv[...])
        o = jnp.zeros((grp, hk), F32)
        for g in range(N_KV_HEADS):
            o = o + jnp.where(lane_head == g, pv[g * grp:(g + 1) * grp, :], 0.0)
        o_ref[s_i] = o


def _attn_sample(q_blk, kc, vc, kn, vn, bias, sink, n_new, block):
    ns, rows, hk = q_blk.shape
    lb = kc.shape[1]
    lkp = bias.shape[1]
    grp = rows // N_KV_HEADS
    per_s = lambda i: (i, 0, 0)
    return pl.pallas_call(
        functools.partial(_attn_sample_kernel, n_new=n_new),
        grid=(ns // block,),
        in_specs=[_const_spec(sink.shape),
                  pl.BlockSpec((block, rows, hk), per_s),
                  pl.BlockSpec((block, lb, hk), per_s), pl.BlockSpec((block, lb, hk), per_s),
                  pl.BlockSpec((block,) + kn.shape[1:], per_s), pl.BlockSpec((block,) + vn.shape[1:], per_s),
                  _const_spec(bias.shape)],
        out_specs=[pl.BlockSpec((block, grp, hk), per_s),
                   pl.BlockSpec((block, lb, hk), per_s), pl.BlockSpec((block, lb, hk), per_s)],
        out_shape=[jax.ShapeDtypeStruct((ns, grp, hk), F32),
                   jax.ShapeDtypeStruct((ns, lb, hk), F32),
                   jax.ShapeDtypeStruct((ns, lb, hk), F32)],
        scratch_shapes=[pltpu.VMEM((lkp, hk), BF16), pltpu.VMEM((lkp, hk), BF16)],
        compiler_params=pltpu.CompilerParams(dimension_semantics=("arbitrary",),
                                             vmem_limit_bytes=VMEM_LIMIT),
        name="attn_sample",
    )(sink, q_blk, kc, vc, kn, vn, bias)


def kernel(x_prompt, x_sample, state_ssm_re, state_ssm_im, state_ffn_conv, cache_k_win, cache_v_win, p_prompt, p_sample, g_mix, g_ffn, g_ple, ssm_lam_re, ssm_lam_im, ssm_log_dt, ssm_b_re, ssm_b_im, ssm_c_re, ssm_c_im, ssm_d, w_glu, b_glu, g_kv, w_k, w_v, k_norm, w_q, q_norm, sinks, w_o, rel_bias, w_up, conv_w, conv_b, w_down, w_ple_in, w_ple_gate):
    n, l, d = x_prompt.shape
    ns, ls, _ = x_sample.shape
    n_groups, n_state = ssm_lam_re.shape[1:]
    nst = n_groups * n_state
    ff2 = w_up.shape[2]
    hk = w_k.shape[1]
    hq = w_q.shape[2]
    n_heads = hq // HEAD_DIM
    rep = n_heads // N_KV_HEADS
    lb = cache_k_win.shape[1]

    ssm_steps = min(32, l // SUBLANES)
    ffn_rows = min(512, l)
    head_rows_p = SUBLANES
    tm = lambda a: jnp.swapaxes(a, 0, 1)

    sp = _ssm_params(ssm_lam_re[0], ssm_lam_im[0], ssm_log_dt[0], ssm_b_re[0], ssm_b_im[0], ssm_c_re[0],
                     ssm_c_im[0], ssm_steps)
    gmix0 = g_mix[0].reshape(1, d)
    dskip = ssm_d[0].reshape(1, d)
    wglu = w_glu[0].astype(BF16)
    bglu = b_glu[0].reshape(1, 2 * d)
    hp, sre_p, sim_p = _ssm_prompt(x_prompt, sp, gmix0, dskip, wglu, bglu, ssm_steps)
    hp = hp.reshape(n * l, d)
    xs_tm = tm(x_sample).reshape(ls * ns, d)
    hs, sre_s, sim_s = _ssm_sample(xs_tm, state_ssm_re[0].reshape(ns, nst), state_ssm_im[0].reshape(ns, nst),
                                   sp, gmix0, dskip, wglu, bglu, ns, ls)

    pp = p_prompt.reshape(p_prompt.shape[0] * n * l, -1)
    ps = jnp.swapaxes(p_sample, 1, 2).reshape(p_sample.shape[0] * ls * ns, -1)
    zero_head = jnp.zeros((n, head_rows_p, ff2), F32)
    conv_s_tm = jnp.swapaxes(state_ffn_conv, 1, 2).reshape(state_ffn_conv.shape[0], 1, (CONV_WIDTH - 1) * ns, ff2)

    fw = _ffn_weights(g_ffn, w_up, conv_w, conv_b, w_down, g_ple, w_ple_gate, w_ple_in)

    def ffn_layer(i, hp, hs, attn_p=None, attn_s=None):
        hp, tail_p = _ffn(hp, pp, i, zero_head, fw, ffn_rows, n, 1, attn_p)
        hs, tail_s = _ffn(hs, ps, i, conv_s_tm[i], fw, ls * ns, 1, ns, attn_s)
        conv_p = tail_p[:, head_rows_p - (CONV_WIDTH - 1):, :]
        conv_s = jnp.swapaxes(tail_s.reshape(CONV_WIDTH - 1, ns, ff2), 0, 1)
        return hp, hs, conv_p, conv_s

    hp, hs, conv_p0, conv_s0 = ffn_layer(0, hp, hs)

    qw = dict(g_kv=g_kv.reshape(1, d), g_mix=g_mix[1].reshape(1, d), w_k=w_k.astype(BF16), w_v=w_v.astype(BF16),
              w_q=w_q[0].astype(BF16), k_norm=jnp.tile(k_norm, hk // HEAD_DIM).reshape(1, hk),
              q_norm=jnp.tile(q_norm[0], n_heads).reshape(1, hq),
              k_pool=_head_pool_matrices(hk), q_pool=_head_pool_matrices(hq))
    q_p, k_p, v_p, kx_p, vx_p = _qkv(hp, qw, ffn_rows, True)
    q_s, k_s, v_s = _qkv(hs, qw, ls * ns, False)

    wo = w_o[0].astype(BF16)
    bias_p = _bias_table(rel_bias, WINDOW, WINDOW, 2 * WINDOW)
    bias_first = jnp.where(jnp.arange(2 * WINDOW) >= WINDOW, bias_p, -jnp.inf)
    o_p = _attn_prompt(q_p, kx_p, vx_p, jnp.stack([bias_first, bias_p]), sinks[0].astype(F32), n, l)

    lkp = 2 * WINDOW
    new_rows = 16
    pad_new = lambda a: jnp.pad(tm(a.reshape(ls, ns, hk)), ((0, 0), (0, new_rows - ls), (0, 0)))
    kc = cache_k_win.reshape(ns, lb, hk)
    vc = cache_v_win.reshape(ns, lb, hk)
    bias_s = _bias_table(rel_bias, lb, ls, lb + ls)
    bias_s = jnp.pad(bias_s, ((0, 0), (0, 0), (0, lkp - lb - ls)), constant_values=-jnp.inf)
    bias_s = bias_s.reshape(n_heads * ls, lkp)
    sink_s = jnp.repeat(sinks[0].astype(F32), ls).reshape(n_heads * ls, 1)
    q5 = jnp.transpose(q_s.reshape(ls, ns, N_KV_HEADS, rep, HEAD_DIM), (1, 2, 3, 0, 4))
    q_blk = jnp.einsum('sgrtd,gh->sgrthd', q5, jnp.eye(N_KV_HEADS, dtype=q5.dtype))
    q_blk = q_blk.reshape(ns, n_heads * ls, hk)
    o4, k_win_s, v_win_s = _attn_sample(q_blk, kc, vc, pad_new(k_s), pad_new(v_s), bias_s, sink_s, ls, 8)
    o_s = jnp.transpose(o4.reshape(ns, rep, ls, N_KV_HEADS, HEAD_DIM), (2, 0, 3, 1, 4)).reshape(ls * ns, hq)
    o_s = o_s.astype(BF16)

    hp, hs, conv_p1, conv_s1 = ffn_layer(1, hp, hs, (o_p, wo), (o_s, wo))

    y_prompt = hp.reshape(n, l, d)
    y_sample = tm(hs.reshape(ls, ns, d))
    ssm_shape = (1, -1, n_groups, n_state)
    kvh_shape = (-1, lb, N_KV_HEADS, HEAD_DIM)
    k_win_p = k_p.reshape(n, l, hk)[:, l - WINDOW:].reshape(n, WINDOW, N_KV_HEADS, HEAD_DIM)
    v_win_p = v_p.reshape(n, l, hk)[:, l - WINDOW:].reshape(n, WINDOW, N_KV_HEADS, HEAD_DIM)
    return (y_prompt, y_sample,
            sre_p.reshape(ssm_shape), sim_p.reshape(ssm_shape),
            sre_s.reshape(ssm_shape), sim_s.reshape(ssm_shape),
            jnp.stack([conv_p0, conv_p1]), jnp.stack([conv_s0, conv_s1]),
            k_win_p, v_win_p, k_win_s.reshape(kvh_shape), v_win_s.reshape(kvh_shape))
```

```python
import functools
import math

import jax
import jax.numpy as jnp
from jax import lax
from jax.experimental import pallas as pl
from jax.experimental.pallas import tpu as pltpu

F32 = jnp.float32
BF16 = jnp.bfloat16

EPS = 1e-6
SSM_GROUP = 16
SSM_STATE = 64
HEAD_DIM = 64
N_KV_HEADS = 4
WINDOW = 128
NUM_BUCKETS = 32
MAX_DISTANCE = 128
CONV_WIDTH = 3

LANES = 128
SUBLANES = 8
MXU_DIM = 256
VMEM_LIMIT = 56 * 1024 * 1024

SCAN_COLS = 1024
FF_CHUNK = 256


def _dot(a, b):
    return jnp.dot(a, b, preferred_element_type=F32)


def _dot_nt(a, b):
    return lax.dot_general(a, b, (((1,), (1,)), ((), ())), preferred_element_type=F32)


def _rms(x, g):
    ms = jnp.mean(x * x, axis=-1, keepdims=True)
    return x * lax.rsqrt(ms + EPS) * g


def _head_rms(x, pool, spread, g):
    ms = _dot((x * x).astype(BF16), pool)
    scale = lax.rsqrt(ms + EPS)
    hi = scale.astype(BF16)
    lo = (scale - hi.astype(F32)).astype(BF16)
    return x * (_dot(hi, spread) + _dot(lo, spread)) * g


def _const_spec(shape):
    nd = len(shape)
    return pl.BlockSpec(shape, lambda *_: (0,) * nd, pipeline_mode=pl.Buffered(1))


def _b_project(ub, bre_ref, bim_ref, bure, buim):
    n_kt = bre_ref.shape[0]
    kw = bre_ref.shape[1]
    nw = bre_ref.shape[2]
    for kt in range(n_kt):
        lhs = ub[:, kt * kw:(kt + 1) * kw]
        bure[:, kt * nw:(kt + 1) * nw] = _dot(lhs, bre_ref[kt])
        buim[:, kt * nw:(kt + 1) * nw] = _dot(lhs, bim_ref[kt])


def _scan(bure, buim, are_ref, aim_ref, n_seq, n_steps, init_fn, final_fn, store):
    nst = bure.shape[1]
    for cb in range(nst // SCAN_COLS):
        cs = slice(cb * SCAN_COLS, (cb + 1) * SCAN_COLS)
        ar = jnp.broadcast_to(are_ref[:, cs], (SUBLANES, SCAN_COLS))
        ai = jnp.broadcast_to(aim_ref[:, cs], (SUBLANES, SCAN_COLS))

        def group(g, _, cs=cs, ar=ar, ai=ai):
            r0 = pl.multiple_of(g * SUBLANES, SUBLANES)

            def step(k, carry):
                hr, hi = carry
                row = pl.multiple_of(k * n_seq + r0, SUBLANES)
                br = bure[pl.ds(row, SUBLANES), cs]
                bi = buim[pl.ds(row, SUBLANES), cs]
                nr = ar * hr - ai * hi + br
                ni = ar * hi + ai * hr + bi
                if store:
                    bure[pl.ds(row, SUBLANES), cs] = nr
                    buim[pl.ds(row, SUBLANES), cs] = ni
                return nr, ni

            hr, hi = lax.fori_loop(0, n_steps, step, init_fn(r0, cs), unroll=min(n_steps, 4))
            final_fn(r0, cs, hr, hi)
            return 0

        if n_seq == SUBLANES:
            group(0, 0)
        else:
            lax.fori_loop(0, n_seq // SUBLANES, group, 0)


def _c_project_glu(x, u, bure, buim, cre_ref, ncim_ref, dskip_ref, wglu_ref, bglu_ref):
    d = x.shape[1]
    n_blk = cre_ref.shape[0]
    kw = cre_ref.shape[1]
    ys = []
    for m in range(n_blk):
        hr = bure[:, m * kw:(m + 1) * kw].astype(BF16)
        hi = buim[:, m * kw:(m + 1) * kw].astype(BF16)
        ys.append(_dot(hr, cre_ref[m]) + _dot(hi, ncim_ref[m]))
    y = jnp.concatenate(ys, axis=1) + dskip_ref[...] * u
    z = jax.nn.gelu(y).astype(BF16)
    gl = _dot(z, wglu_ref[...]) + bglu_ref[...]
    return x + gl[:, :d] * jax.nn.sigmoid(gl[:, d:])


def _ssm_prompt_kernel(x_ref, gmix_ref, are_ref, aim_ref, apw_re_ref, apw_im_ref, bw_ref, cre_ref, ncim_ref,
                       dskip_ref, wglu_ref, bglu_ref,
                       out_ref, sre_ref, sim_ref,
                       slab, xp, ub, us, bu_a, bu_b, zb, gl, ends, hin, car, *, n_steps, pitch):
    g_step = pl.program_id(1)
    for parity, (bu_next, bu_cur) in enumerate(((bu_a, bu_b), (bu_b, bu_a))):
        pl.when(g_step % 2 == parity)(functools.partial(
            _ssm_prompt_step, parity, g_step, x_ref, gmix_ref, are_ref, aim_ref, apw_re_ref, apw_im_ref, bw_ref,
            cre_ref, ncim_ref, dskip_ref, wglu_ref, bglu_ref, out_ref, sre_ref, sim_ref,
            slab, xp, ub, us, bu_next, bu_cur, zb, gl, ends, hin, car, n_steps, pitch))


def _ssm_prompt_step(nxt, g_step, x_ref, gmix_ref, are_ref, aim_ref, apw_re_ref, apw_im_ref, bw_ref, cre_ref,
                     ncim_ref, dskip_ref, wglu_ref, bglu_ref, out_ref, sre_ref, sim_ref,
                     slab, xp, ub, us, bu_next, bu_cur, zb, gl, ends, hin, car, n_steps, pitch):
    cur = 1 - nxt
    n_slab = slab.shape[0]
    d = xp.shape[2]
    n_piece = bu_cur.shape[1]
    blk_pieces = SCAN_COLS // MXU_DIM
    n_blk = n_piece // blk_pieces
    trips = 4
    spt = n_steps // trips
    n_cblk = cre_ref.shape[0]
    n_gtile = wglu_ref.shape[0]
    assert 2 * n_piece == n_blk * trips * 2
    assert 2 * n_gtile == n_blk * trips
    assert bw_ref.shape[2] == MXU_DIM and 2 * n_cblk == n_piece

    def lanes(q):
        return slice(q * MXU_DIM, (q + 1) * MXU_DIM)

    for j in range(SUBLANES):
        for c in range(n_slab):
            slab[c, j * pitch:j * pitch + n_steps, :] = x_ref[j * n_steps:(j + 1) * n_steps,
                                                             c * LANES:(c + 1) * LANES]

    def gather(k, _):
        r0 = pl.multiple_of(k * SUBLANES, SUBLANES)
        for c in range(n_slab):
            xp[nxt, pl.ds(r0, SUBLANES), c * LANES:(c + 1) * LANES] = slab[c, pl.ds(k, SUBLANES, stride=pitch), :]
        return 0

    lax.fori_loop(0, n_steps, gather, 0, unroll=4)
    u = _rms(xp[nxt], gmix_ref[...])
    ub[nxt] = u.astype(BF16)
    for c in range(n_slab):
        us[nxt, c] = u[:, c * LANES:(c + 1) * LANES]

    def run_pass(bu, store, init_fn, end_fn, work_fn):
        for blk in range(n_blk):
            coef = [(jnp.broadcast_to(are_ref[:, lanes(blk * blk_pieces + nt)], (SUBLANES, MXU_DIM)),
                     jnp.broadcast_to(aim_ref[:, lanes(blk * blk_pieces + nt)], (SUBLANES, MXU_DIM)))
                    for nt in range(blk_pieces)]

            def trip(i, state, blk=blk, coef=coef):
                if work_fn is not None:
                    work_fn(blk, i)
                state = list(state)
                for s in range(spt):
                    row = (i * spt + s) * SUBLANES
                    for nt in range(blk_pieces):
                        q = blk * blk_pieces + nt
                        ar, ai = coef[nt]
                        hr, hi = state[2 * nt], state[2 * nt + 1]
                        nr = ar * hr - ai * hi + bu[0, q, pl.ds(row, SUBLANES), :]
                        ni = ar * hi + ai * hr + bu[1, q, pl.ds(row, SUBLANES), :]
                        if store:
                            bu[0, q, pl.ds(row, SUBLANES), :] = nr
                            bu[1, q, pl.ds(row, SUBLANES), :] = ni
                        state[2 * nt], state[2 * nt + 1] = nr, ni
                return tuple(state)

            state = init_fn(blk)
            for i in range(trips):
                state = trip(i, state)
            end_fn(blk, state)

    def zero_init(blk):
        return tuple(jnp.zeros((SUBLANES, MXU_DIM), F32) for _ in range(2 * blk_pieces))

    def keep_ends(blk, state):
        for nt in range(blk_pieces):
            ends[0, :, lanes(blk * blk_pieces + nt)] = state[2 * nt]
            ends[1, :, lanes(blk * blk_pieces + nt)] = state[2 * nt + 1]

    def true_init(blk):
        return tuple(hin[ri, :, lanes(blk * blk_pieces + nt)] for nt in range(blk_pieces) for ri in range(2))

    tiles_per_kt = 2 * blk_pieces

    def b_project_tile(t):
        kt, j = divmod(t, tiles_per_kt)
        ri, nt = divmod(j, blk_pieces)
        q = kt * blk_pieces + nt
        bu_next[ri, q] = _dot(ub[nxt, :, lanes(kt)], bw_ref[ri, q])

    def b_project_slice(blk, i):
        b_project_tile(2 * (blk * trips + i))
        b_project_tile(2 * (blk * trips + i) + 1)

    def c_project_block(m):
        y = dskip_ref[m] * us[cur, m]
        for ri, c_ref in ((0, cre_ref), (1, ncim_ref)):
            for w in range(2):
                y = y + _dot(bu_cur[ri, 2 * m + w].astype(BF16), c_ref[m, w * MXU_DIM:(w + 1) * MXU_DIM, :])
        zb[m] = jax.nn.gelu(y).astype(BF16)

    def glu_slice(blk, i):
        t = blk * trips + i
        if t % 2 == 0:
            nt = t // 2
            z = jnp.concatenate([zb[c] for c in range(n_cblk)], axis=1)
            gl[nt] = _dot(z, wglu_ref[nt]) + bglu_ref[nt]

    def first_tile():
        car[...] = jnp.zeros_like(car)
        for blk in range(n_blk):
            lhs = ub[nxt, :, lanes(blk)]
            for ri in range(2):
                for nt in range(blk_pieces):
                    bu_next[ri, blk * blk_pieces + nt] = _dot(lhs, bw_ref[ri, blk * blk_pieces + nt])
        run_pass(bu_next, False, zero_init, keep_ends, None)

    if nxt == 0:
        pl.when(g_step == 0)(first_tile)

    @pl.when(g_step > 0)
    def _():
        hr = car[0]
        hi = car[1]
        apr = apw_re_ref[...]
        api = apw_im_ref[...]
        for j in range(SUBLANES):
            er = ends[0, j:j + 1, :]
            ei = ends[1, j:j + 1, :]
            hin[0, j:j + 1, :] = hr
            hin[1, j:j + 1, :] = hi
            hr, hi = apr * hr - api * hi + er, apr * hi + api * hr + ei
        car[0] = hr
        car[1] = hi
        sre_ref[...] = hr
        sim_ref[...] = hi

        run_pass(bu_cur, True, true_init, lambda *_: None, b_project_slice)

    @pl.when(g_step != 0)
    def _():
        for m in range(n_cblk):
            c_project_block(m)

    @pl.when(jnp.logical_and(g_step >= 1, pl.program_id(0) >= 0))
    def _():
        run_pass(bu_next, False, zero_init, keep_ends, glu_slice)

        half = n_gtile // 2
        for c in range(half):
            xp[cur, :, lanes(c)] = xp[cur, :, lanes(c)] + gl[c] * jax.nn.sigmoid(gl[half + c])

        def scatter(k, _):
            r0 = pl.multiple_of(k * SUBLANES, SUBLANES)
            for c in range(n_slab):
                slab[c, pl.ds(k, SUBLANES, stride=pitch), :] = xp[cur, pl.ds(r0, SUBLANES), c * LANES:(c + 1) * LANES]
            return 0

        lax.fori_loop(0, n_steps, scatter, 0, unroll=4)
        for j in range(SUBLANES):
            for c in range(n_slab):
                out_ref[j * n_steps:(j + 1) * n_steps, c * LANES:(c + 1) * LANES] = slab[c, j * pitch:j * pitch + n_steps, :]


def _ssm_sample_kernel(x_ref, h0re_ref, h0im_ref, gmix_ref, are_ref, aim_ref, bre_ref, bim_ref, cre_ref,
                       ncim_ref, dskip_ref, wglu_ref, bglu_ref,
                       out_ref, sre_ref, sim_ref, bure, buim, *, n_seq, n_steps):
    x = x_ref[...]
    u = _rms(x, gmix_ref[...])
    _b_project(u.astype(BF16), bre_ref, bim_ref, bure, buim)

    def init(r0, cs):
        return h0re_ref[pl.ds(r0, SUBLANES), cs], h0im_ref[pl.ds(r0, SUBLANES), cs]

    def final(r0, cs, hr, hi):
        sre_ref[pl.ds(r0, SUBLANES), cs] = hr
        sim_ref[pl.ds(r0, SUBLANES), cs] = hi

    _scan(bure, buim, are_ref, aim_ref, n_seq, n_steps, init, final, store=True)
    out_ref[...] = _c_project_glu(x, u, bure, buim, cre_ref, ncim_ref, dskip_ref, wglu_ref, bglu_ref)


def _ssm_params(lam_re, lam_im, log_dt, b_re, b_im, c_re, c_im, n_pow):
    g, p = lam_re.shape
    lr = lam_re.astype(F32)
    li = lam_im.astype(F32)
    dt = jnp.exp(log_dt.astype(F32))[:, None]
    mag = jnp.exp(lr * dt)
    ang = li * dt
    ab_re = mag * jnp.cos(ang)
    ab_im = mag * jnp.sin(ang)
    den = lr * lr + li * li
    nr = ab_re - 1.0
    f_re = (nr * lr + ab_im * li) / den
    f_im = (ab_im * lr - nr * li) / den
    br = b_re.astype(F32)
    bi = b_im.astype(F32)
    bb_re = f_re[..., None] * br - f_im[..., None] * bi
    bb_im = f_re[..., None] * bi + f_im[..., None] * br

    c = bb_re.shape[2]
    gk = MXU_DIM // c
    eye_k = jnp.eye(gk, dtype=F32)

    def b_blocks(bb):
        bt = jnp.transpose(bb, (0, 2, 1)).reshape(g // gk, gk, c, p)
        return jnp.einsum('tgcp,gh->tgchp', bt, eye_k).reshape(g // gk, gk * c, gk * p).astype(BF16)

    gc = LANES // c
    eye_c = jnp.eye(gc, dtype=F32)

    def c_blocks(cc):
        ct = jnp.transpose(cc.astype(F32), (0, 2, 1)).reshape(g // gc, gc, p, c)
        return jnp.einsum('tgpc,gh->tgphc', ct, eye_c).reshape(g // gc, gc * p, gc * c).astype(BF16)

    pw_re = ab_re.reshape(1, g * p)
    pw_im = ab_im.reshape(1, g * p)
    for _ in range(n_pow.bit_length() - 1):
        pw_re, pw_im = pw_re * pw_re - pw_im * pw_im, 2.0 * pw_re * pw_im
    return dict(a_re=ab_re.reshape(1, g * p), a_im=ab_im.reshape(1, g * p),
                ap_re=pw_re, ap_im=pw_im,
                b_re=b_blocks(bb_re), b_im=b_blocks(bb_im),
                c_re=c_blocks(c_re), nc_im=c_blocks(-c_im))


def _ssm_prompt(x, sp, gmix, dskip, wglu, bglu, n_steps):
    n, l, d = x.shape
    nst = sp['a_re'].shape[1]
    rows = SUBLANES * n_steps
    pitch = n_steps + SUBLANES
    tiles = l // rows
    n_piece = nst // MXU_DIM
    kt, kw, nw = sp['b_re'].shape
    to_tiles = lambda b: jnp.transpose(b.reshape(kt, kw, nw // MXU_DIM, MXU_DIM), (0, 2, 1, 3)).reshape(
        n_piece, kw, MXU_DIM)
    bw = jnp.stack([to_tiles(sp['b_re']), to_tiles(sp['b_im'])])
    n_gtile = wglu.shape[1] // MXU_DIM
    wglu_t = jnp.transpose(wglu.reshape(d, n_gtile, MXU_DIM), (1, 0, 2))
    consts = [gmix, sp['a_re'], sp['a_im'], sp['ap_re'], sp['ap_im'], bw, sp['c_re'], sp['nc_im'],
              dskip.reshape(d // LANES, 1, LANES), wglu_t, bglu.reshape(n_gtile, 1, MXU_DIM)]
    out, sre, sim = pl.pallas_call(
        functools.partial(_ssm_prompt_kernel, n_steps=n_steps, pitch=pitch),
        grid=(n, tiles + 1),
        in_specs=[pl.BlockSpec((None, rows, d), lambda i, g: (i, jnp.minimum(g, tiles - 1), 0))]
        + [_const_spec(c.shape) for c in consts],
        out_specs=[pl.BlockSpec((None, rows, d), lambda i, g: (i, jnp.maximum(g - 1, 0), 0)),
                   pl.BlockSpec((None, 1, nst), lambda i, g: (i, 0, 0)),
                   pl.BlockSpec((None, 1, nst), lambda i, g: (i, 0, 0))],
        out_shape=[jax.ShapeDtypeStruct((n, l, d), F32),
                   jax.ShapeDtypeStruct((n, 1, nst), F32),
                   jax.ShapeDtypeStruct((n, 1, nst), F32)],
        scratch_shapes=[pltpu.VMEM((d // LANES, SUBLANES * pitch, LANES), F32),
                        pltpu.VMEM((2, rows, d), F32),
                        pltpu.VMEM((2, rows, d), BF16),
                        pltpu.VMEM((2, d // LANES, rows, LANES), F32),
                        pltpu.VMEM((2, n_piece, rows, MXU_DIM), F32),
                        pltpu.VMEM((2, n_piece, rows, MXU_DIM), F32),
                        pltpu.VMEM((d // LANES, rows, LANES), BF16),
                        pltpu.VMEM((n_gtile, rows, MXU_DIM), F32),
                        pltpu.VMEM((2, SUBLANES, nst), F32),
                        pltpu.VMEM((2, SUBLANES, nst), F32),
                        pltpu.VMEM((2, 1, nst), F32)],
        compiler_params=pltpu.CompilerParams(dimension_semantics=("arbitrary", "arbitrary"),
                                             vmem_limit_bytes=VMEM_LIMIT),
        name="ssm_prompt",
    )(x, *consts)
    return out, sre[:, 0], sim[:, 0]


def _ssm_sample(x_tm, h0re, h0im, sp, gmix, dskip, wglu, bglu, n_seq, n_steps):
    rows, d = x_tm.shape
    nst = sp['a_re'].shape[1]
    args = [x_tm, h0re, h0im, gmix, sp['a_re'], sp['a_im'], sp['b_re'], sp['b_im'], sp['c_re'], sp['nc_im'],
            dskip, wglu, bglu]
    return pl.pallas_call(
        functools.partial(_ssm_sample_kernel, n_seq=n_seq, n_steps=n_steps),
        grid=(1,),
        in_specs=[_const_spec(a.shape) for a in args],
        out_specs=[pl.BlockSpec((rows, d), lambda i: (0, 0)), pl.BlockSpec((n_seq, nst), lambda i: (0, 0)),
                   pl.BlockSpec((n_seq, nst), lambda i: (0, 0))],
        out_shape=[jax.ShapeDtypeStruct((rows, d), F32),
                   jax.ShapeDtypeStruct((n_seq, nst), F32),
                   jax.ShapeDtypeStruct((n_seq, nst), F32)],
        scratch_shapes=[pltpu.VMEM((rows, nst), F32), pltpu.VMEM((rows, nst), F32)],
        compiler_params=pltpu.CompilerParams(dimension_semantics=("arbitrary",),
                                             vmem_limit_bytes=VMEM_LIMIT),
        name="ssm_sample",
    )(*args)


def _ffn_kernel(*refs, rows, head_rows, shift, has_attn):
    if has_attn:
        h_ref, o_ref, wo_ref = refs[:3]
        refs = refs[3:]
    else:
        h_ref = refs[0]
        refs = refs[1:]
    (p_ref, head_ref, gffn_ref, wup_ref, cw_ref, cb_ref, wdown_ref, gple_ref, wgate_ref, win_ref,
     out_ref, tail_ref, acc, xn_ref, ext_a, ext_b) = refs
    ff = wdown_ref.shape[0]
    n_chunk = ff // FF_CHUNK
    up0 = head_rows
    up1 = head_rows + rows

    @pl.when(pl.program_id(1) == 0)
    def _():
        tail_ref[...] = head_ref[...]

    x = h_ref[...]
    if has_attn:
        x = x + _dot(o_ref[...], wo_ref[...])
    xn_ref[...] = _rms(x, gffn_ref[...]).astype(BF16)
    acc[...] = jnp.zeros_like(acc)

    n_slab = ext_a.shape[1]

    def slab_cols(c, half, s):
        c0 = half * ff + c * FF_CHUNK + s * LANES
        return slice(c0, c0 + LANES)

    def up_project(c, ext):
        for half in range(2):
            c0 = half * ff + c * FF_CHUNK
            up = _dot(xn_ref[...], wup_ref[:, c0:c0 + FF_CHUNK])
            for s in range(n_slab):
                ext[half, s, 0:up0, :] = tail_ref[:, slab_cols(c, half, s)]
                ext[half, s, up0:up1, :] = up[:, s * LANES:(s + 1) * LANES]

    def conv(ext, half, c):
        parts = []
        for s in range(n_slab):
            ls = slab_cols(c, half, s)
            t2 = ext[half, s, up0 - 2 * shift:up1 - 2 * shift, :]
            t1 = ext[half, s, up0 - shift:up1 - shift, :]
            tail_ref[:, ls] = ext[half, s, rows:up1, :]
            r = cb_ref[:, ls] + t2 * cw_ref[0:1, ls]
            r = r + t1 * cw_ref[1:2, ls]
            parts.append(r + ext[half, s, up0:up1, :] * cw_ref[2:3, ls])
        return jnp.concatenate(parts, axis=1)

    def down_project(c, ext):
        cg = conv(ext, 0, c)
        cv = conv(ext, 1, c)
        act = (cg * jax.nn.sigmoid(cg) * cv).astype(BF16)
        acc[...] += _dot(act, wdown_ref[c * FF_CHUNK:(c + 1) * FF_CHUNK, :])

    bufs = (ext_a, ext_b)
    up_project(0, bufs[0])
    for c in range(n_chunk):
        if c + 1 < n_chunk:
            up_project(c + 1, bufs[(c + 1) % 2])
        down_project(c, bufs[c % 2])

    h2 = x + acc[...]
    gate = jax.nn.sigmoid(_dot(_rms(h2, gple_ref[...]).astype(BF16), wgate_ref[...]))
    pe = _dot(p_ref[...].astype(BF16), win_ref[...])
    out_ref[...] = h2 + pe * gate


def _ffn_weights(g_ffn, w_up, conv_w, conv_b, w_down, g_ple, w_gate, w_in):
    depth, d, ff2 = w_up.shape
    return dict(
        g_ffn=g_ffn.reshape(depth, 1, d),
        w_up=w_up.astype(BF16),
        conv_w=conv_w,
        conv_b=conv_b.reshape(depth, 1, ff2),
        w_down=w_down.astype(BF16),
        g_ple=g_ple.reshape(depth, 1, d),
        w_gate=w_gate.astype(BF16),
        w_in=w_in.astype(BF16),
    )


def _layer_spec(shape, layer):
    nd = len(shape)
    return pl.BlockSpec((None,) + tuple(shape[1:]), lambda *_: (layer,) + (0,) * (nd - 1),
                        pipeline_mode=pl.Buffered(1))


def _ffn(h, p, layer, head, fw, rows, n_seq_tiles, shift, attn=None):
    total, d = h.shape
    tiles = total // rows // n_seq_tiles
    head_rows, ff2 = head.shape[1], head.shape[2]
    row_map = lambda i, t: (i * tiles + t, 0)
    p_map = lambda i, t: (layer * (total // rows) + i * tiles + t, 0)
    consts = [fw['g_ffn'], fw['w_up'], fw['conv_w'], fw['conv_b'], fw['w_down'], fw['g_ple'], fw['w_gate'],
              fw['w_in']]
    args = [h]
    in_specs = [pl.BlockSpec((rows, d), row_map)]
    if attn is not None:
        o, wo = attn
        args += [o, wo]
        in_specs += [pl.BlockSpec((rows, o.shape[1]), row_map), _const_spec(wo.shape)]
    args += [p, head] + consts
    in_specs += [pl.BlockSpec((rows, p.shape[1]), p_map),
                 pl.BlockSpec((None, head_rows, ff2), lambda i, t: (i, 0, 0))]
    in_specs += [_layer_spec(c.shape, layer) for c in consts]
    out, tail = pl.pallas_call(
        functools.partial(_ffn_kernel, rows=rows, head_rows=head_rows, shift=shift, has_attn=attn is not None),
        grid=(n_seq_tiles, tiles),
        in_specs=in_specs,
        out_specs=[pl.BlockSpec((rows, d), row_map),
                   pl.BlockSpec((None, head_rows, ff2), lambda i, t: (i, 0, 0))],
        out_shape=[jax.ShapeDtypeStruct((total, d), F32),
                   jax.ShapeDtypeStruct(head.shape, F32)],
        scratch_shapes=[pltpu.VMEM((rows, d), F32),
                        pltpu.VMEM((rows, d), BF16),
                        pltpu.VMEM((2, FF_CHUNK // LANES, head_rows + rows, LANES), F32),
                        pltpu.VMEM((2, FF_CHUNK // LANES, head_rows + rows, LANES), F32)],
        compiler_params=pltpu.CompilerParams(dimension_semantics=("arbitrary", "arbitrary"),
                                             vmem_limit_bytes=VMEM_LIMIT),
        name="ffn_ple",
    )(*args)
    return out, tail


def _qkv_kernel(*refs, paired):
    (h_ref, gkv_ref, gmix_ref, wk_ref, wv_ref, wq_ref, knorm_ref, qnorm_ref, kpool_ref, kspread_ref,
     qpool_ref, qspread_ref) = refs[:12]
    refs = refs[12:]
    x = h_ref[...]
    s = _rms(x, gkv_ref[...]).astype(BF16)
    k = _head_rms(_dot(s, wk_ref[...]), kpool_ref[...], kspread_ref[...], knorm_ref[...])
    v = _dot(s, wv_ref[...])
    xn = _rms(x, gmix_ref[...]).astype(BF16)
    q = _head_rms(_dot(xn, wq_ref[...]), qpool_ref[...], qspread_ref[...], qnorm_ref[...])
    if paired:
        place_ref, fill_ref, q_ref, k_ref, v_ref, kx_ref, vx_ref = refs
        kx_ref[...] = _dot(k.astype(BF16), place_ref[...]).astype(BF16)
        vx_ref[...] = (_dot(v.astype(BF16), place_ref[...]) + fill_ref[...]).astype(BF16)
    else:
        q_ref, k_ref, v_ref = refs
    k_ref[...] = k
    v_ref[...] = v
    q_ref[...] = (q * (HEAD_DIM ** -0.5)).astype(q_ref.dtype)


def _head_pool_matrices(n):
    member = (jnp.arange(n)[:, None] // HEAD_DIM) == jnp.arange(LANES)[None, :]
    return member.astype(BF16) * (1.0 / HEAD_DIM), member.T.astype(BF16)


def _pair_placement(hk):
    src = jnp.arange(hk)
    dst = jnp.arange(4 * hk)
    same_head = (src[:, None] // HEAD_DIM) == (dst[None, :] // (4 * HEAD_DIM))
    sub = (dst % (4 * HEAD_DIM)) // HEAD_DIM
    same_dim = (src[:, None] % HEAD_DIM) == (dst[None, :] % HEAD_DIM)
    place = same_head & same_dim & ((sub == 0) | (sub == 3))[None, :]
    fill = ((sub == 1) | (sub == 2)).astype(F32).reshape(1, 4 * hk)
    return place.astype(BF16), fill


def _qkv(h, qw, rows, paired):
    total, d = h.shape
    hk = qw['w_k'].shape[1]
    hq = qw['w_q'].shape[1]
    consts = [qw['g_kv'], qw['g_mix'], qw['w_k'], qw['w_v'], qw['w_q'], qw['k_norm'], qw['q_norm'],
              *qw['k_pool'], *qw['q_pool']]
    row_map = lambda i: (i, 0)
    out_specs = [pl.BlockSpec((rows, hq), row_map), pl.BlockSpec((rows, hk), row_map),
                 pl.BlockSpec((rows, hk), row_map)]
    out_shape = [jax.ShapeDtypeStruct((total, hq), BF16),
                 jax.ShapeDtypeStruct((total, hk), F32),
                 jax.ShapeDtypeStruct((total, hk), F32)]
    if paired:
        consts += list(_pair_placement(hk))
        out_specs += [pl.BlockSpec((rows, 4 * hk), row_map)] * 2
        out_shape += [jax.ShapeDtypeStruct((total, 4 * hk), BF16)] * 2
    return pl.pallas_call(
        functools.partial(_qkv_kernel, paired=paired),
        grid=(total // rows,),
        in_specs=[pl.BlockSpec((rows, d), row_map)] + [_const_spec(c.shape) for c in consts],
        out_specs=out_specs,
        out_shape=out_shape,
        compiler_params=pltpu.CompilerParams(dimension_semantics=("arbitrary",),
                                             vmem_limit_bytes=VMEM_LIMIT),
        name="qkv",
    )(h, *consts)


def _t5_bucket(dist):
    max_exact = NUM_BUCKETS // 2
    df = jnp.maximum(dist, 1).astype(F32)
    large = max_exact + (jnp.log(df / max_exact) / math.log(MAX_DISTANCE / max_exact)
                         * (NUM_BUCKETS - max_exact)).astype(jnp.int32)
    return jnp.where(dist < max_exact, dist, jnp.minimum(large, NUM_BUCKETS - 1))


def _bias_table(rel_bias, lb, qb, lk):
    dist = lb + jnp.arange(qb)[:, None] - jnp.arange(lk)[None, :]
    per_dist = rel_bias[_t5_bucket(jnp.arange(WINDOW + 1))].astype(F32)
    onehot = (jnp.clip(dist, 0, WINDOW)[..., None] == jnp.arange(WINDOW + 1)).astype(F32)
    bias = jnp.einsum('qkd,dh->hqk', onehot, per_dist, precision=lax.Precision.HIGHEST)
    ok = (dist >= 0) & (dist <= WINDOW)
    return jnp.where(ok[None], bias, -jnp.inf)


def _softmax_pv(s, sink, v):
    m = jnp.maximum(jnp.max(s, axis=-1, keepdims=True), sink)
    pr = jnp.exp(s - m)
    den = jnp.sum(pr, axis=-1, keepdims=True) + jnp.exp(sink - m)
    return _dot(pr.astype(BF16), v) / den


def _attn_prompt_kernel(sink_ref, q_ref, kx_ref, vx_ref, bias_ref, o_ref, kx_prev, vx_prev):
    qb = q_ref.shape[0]
    n_heads = bias_ref.shape[0]
    rep = n_heads // N_KV_HEADS
    grp = 4 * HEAD_DIM

    @pl.when(pl.program_id(1) == 0)
    def _():
        kx_prev[...] = jnp.zeros_like(kx_prev)
        vx_prev[...] = jnp.zeros_like(vx_prev)

    low = lax.broadcasted_iota(jnp.int32, (qb, LANES), 1) < HEAD_DIM
    for g in range(N_KV_HEADS):
        gs = slice(g * grp, (g + 1) * grp)
        kx = jnp.concatenate([kx_prev[:, gs], kx_ref[:, gs]], axis=0)
        vx = jnp.concatenate([vx_prev[:, gs], vx_ref[:, gs]], axis=0)
        for pair in range(rep // 2):
            h0 = g * rep + 2 * pair
            ls = slice(h0 * HEAD_DIM, h0 * HEAD_DIM + LANES)
            qp = q_ref[:, ls]
            res = []
            for h, kh in ((h0, kx[:, :LANES]), (h0 + 1, kx[:, LANES:])):
                s = _dot_nt(qp, kh) + bias_ref[h]
                m = jnp.maximum(jnp.max(s, axis=-1, keepdims=True), sink_ref[h])
                res.append((_dot(jnp.exp(s - m).astype(BF16), vx), jnp.exp(sink_ref[h] - m)))
            (ra, ea), (rb, eb) = res
            num = jnp.where(low, ra[:, :LANES], rb[:, LANES:])
            den = jnp.where(low, ra[:, LANES:], rb[:, :LANES]) + jnp.where(low, ea, eb)
            o_ref[:, ls] = (num / den).astype(o_ref.dtype)
    kx_prev[...] = kx_ref[...]
    vx_prev[...] = vx_ref[...]


def _attn_prompt(q, kx, vx, bias, sinks, n, l):
    hq = q.shape[1]
    wx = kx.shape[1]
    qb = WINDOW
    nb = l // qb
    cur = lambda i, b, *_: (i * nb + b, 0)
    return pl.pallas_call(
        _attn_prompt_kernel,
        grid_spec=pltpu.PrefetchScalarGridSpec(
            num_scalar_prefetch=1,
            grid=(n, nb),
            in_specs=[pl.BlockSpec((qb, hq), cur), pl.BlockSpec((qb, wx), cur), pl.BlockSpec((qb, wx), cur),
                      pl.BlockSpec((None,) + bias.shape[1:], lambda i, b, *_: (jnp.minimum(b, 1), 0, 0, 0))],
            out_specs=pl.BlockSpec((qb, hq), cur),
            scratch_shapes=[pltpu.VMEM((qb, wx), BF16), pltpu.VMEM((qb, wx), BF16)]),
        out_shape=jax.ShapeDtypeStruct((n * l, hq), BF16),
        compiler_params=pltpu.CompilerParams(dimension_semantics=("arbitrary", "arbitrary"),
                                             vmem_limit_bytes=VMEM_LIMIT),
        name="attn_prompt",
    )(sinks, q, kx, vx, bias)


def _attn_sample_kernel(sink_ref, q_ref, kc_ref, vc_ref, kn_ref, vn_ref, bias_ref,
                        o_ref, kwin_ref, vwin_ref, *, n_new):
    lb = kc_ref.shape[1]
    new_rows = kn_ref.shape[1]
    rows = q_ref.shape[1]
    hk = kc_ref.shape[2]
    grp = rows // N_KV_HEADS
    zero_rows = jnp.zeros((bias_ref.shape[1] - lb - new_rows, hk), BF16)
    lane_head = lax.broadcasted_iota(jnp.int32, (grp, hk), 1) // HEAD_DIM
    samples = range(q_ref.shape[0])
    scores = []
    for s_i in samples:
        kk = jnp.concatenate([kc_ref[s_i].astype(BF16), kn_ref[s_i].astype(BF16), zero_rows], axis=0)
        scores.append(_dot_nt(q_ref[s_i], kk) + bias_ref[...])
    probs = []
    for s in scores:
        m = jnp.maximum(jnp.max(s, axis=-1, keepdims=True), sink_ref[...])
        pr = jnp.exp(s - m)
        probs.append((pr.astype(BF16), jnp.sum(pr, axis=-1, keepdims=True) + jnp.exp(sink_ref[...] - m)))
    for s_i, (pr, den) in zip(samples, probs):
        vv = jnp.concatenate([vc_ref[s_i].astype(BF16), vn_ref[s_i].astype(BF16), zero_rows], axis=0)
        pv = _dot(pr, vv) / den
        o = jnp.zeros((grp, hk), F32)
        for g in range(N_KV_HEADS):
            o = o + jnp.where(lane_head == g, pv[g * grp:(g + 1) * grp, :], 0.0)
        o_ref[s_i] = o
    for s_i in samples:
        kwin_ref[s_i, 0:lb - n_new, :] = kc_ref[s_i, n_new:lb, :]
        kwin_ref[s_i, lb - n_new:lb, :] = kn_ref[s_i, 0:n_new, :]
        vwin_ref[s_i, 0:lb - n_new, :] = vc_ref[s_i, n_new:lb, :]
        vwin_ref[s_i, lb - n_new:lb, :] = vn_ref[s_i, 0:n_new, :]


def _attn_sample(q_blk, kc, vc, kn, vn, bias, sink, n_new, block):
    ns, rows, hk = q_blk.shape
    lb = kc.shape[1]
    lkp = bias.shape[1]
    grp = rows // N_KV_HEADS
    per_s = lambda i: (i, 0, 0)
    return pl.pallas_call(
        functools.partial(_attn_sample_kernel, n_new=n_new),
        grid=(ns // block,),
        in_specs=[_const_spec(sink.shape),
                  pl.BlockSpec((block, rows, hk), per_s),
                  pl.BlockSpec((block, lb, hk), per_s), pl.BlockSpec((block, lb, hk), per_s),
                  pl.BlockSpec((block,) + kn.shape[1:], per_s), pl.BlockSpec((block,) + vn.shape[1:], per_s),
                  _const_spec(bias.shape)],
        out_specs=[pl.BlockSpec((block, grp, hk), per_s),
                   pl.BlockSpec((block, lb, hk), per_s), pl.BlockSpec((block, lb, hk), per_s)],
        out_shape=[jax.ShapeDtypeStruct((ns, grp, hk), F32),
                   jax.ShapeDtypeStruct((ns, lb, hk), F32),
                   jax.ShapeDtypeStruct((ns, lb, hk), F32)],
        compiler_params=pltpu.CompilerParams(dimension_semantics=("arbitrary",),
                                             vmem_limit_bytes=VMEM_LIMIT),
        name="attn_sample",
    )(sink, q_blk, kc, vc, kn, vn, bias)


def kernel(x_prompt, x_sample, state_ssm_re, state_ssm_im, state_ffn_conv, cache_k_win, cache_v_win, p_prompt, p_sample, g_mix, g_ffn, g_ple, ssm_lam_re, ssm_lam_im, ssm_log_dt, ssm_b_re, ssm_b_im, ssm_c_re, ssm_c_im, ssm_d, w_glu, b_glu, g_kv, w_k, w_v, k_norm, w_q, q_norm, sinks, w_o, rel_bias, w_up, conv_w, conv_b, w_down, w_ple_in, w_ple_gate):
    n, l, d = x_prompt.shape
    ns, ls, _ = x_sample.shape
    n_groups, n_state = ssm_lam_re.shape[1:]
    nst = n_groups * n_state
    ff2 = w_up.shape[2]
    hk = w_k.shape[1]
    hq = w_q.shape[2]
    n_heads = hq // HEAD_DIM
    rep = n_heads // N_KV_HEADS
    lb = cache_k_win.shape[1]

    ssm_steps = min(32, l // SUBLANES)
    ffn_rows = min(512, l)
    head_rows_p = SUBLANES
    tm = lambda a: jnp.swapaxes(a, 0, 1)

    sp = _ssm_params(ssm_lam_re[0], ssm_lam_im[0], ssm_log_dt[0], ssm_b_re[0], ssm_b_im[0], ssm_c_re[0],
                     ssm_c_im[0], ssm_steps)
    gmix0 = g_mix[0].reshape(1, d)
    dskip = ssm_d[0].reshape(1, d)
    wglu = w_glu[0].astype(BF16)
    bglu = b_glu[0].reshape(1, 2 * d)
    hp, sre_p, sim_p = _ssm_prompt(x_prompt, sp, gmix0, dskip, wglu, bglu, ssm_steps)
    hp = hp.reshape(n * l, d)
    xs_tm = tm(x_sample).reshape(ls * ns, d)
    hs, sre_s, sim_s = _ssm_sample(xs_tm, state_ssm_re[0].reshape(ns, nst), state_ssm_im[0].reshape(ns, nst),
                                   sp, gmix0, dskip, wglu, bglu, ns, ls)

    pp = p_prompt.reshape(p_prompt.shape[0] * n * l, -1)
    ps = jnp.swapaxes(p_sample, 1, 2).reshape(p_sample.shape[0] * ls * ns, -1)
    zero_head = jnp.zeros((n, head_rows_p, ff2), F32)
    conv_s_tm = jnp.swapaxes(state_ffn_conv, 1, 2).reshape(state_ffn_conv.shape[0], 1, (CONV_WIDTH - 1) * ns, ff2)

    fw = _ffn_weights(g_ffn, w_up, conv_w, conv_b, w_down, g_ple, w_ple_gate, w_ple_in)

    def ffn_layer(i, hp, hs, attn_p=None, attn_s=None):
        hp, tail_p = _ffn(hp, pp, i, zero_head, fw, ffn_rows, n, 1, attn_p)
        hs, tail_s = _ffn(hs, ps, i, conv_s_tm[i], fw, ls * ns, 1, ns, attn_s)
        conv_p = tail_p[:, head_rows_p - (CONV_WIDTH - 1):, :]
        conv_s = jnp.swapaxes(tail_s.reshape(CONV_WIDTH - 1, ns, ff2), 0, 1)
        return hp, hs, conv_p, conv_s

    hp, hs, conv_p0, conv_s0 = ffn_layer(0, hp, hs)

    qw = dict(g_kv=g_kv.reshape(1, d), g_mix=g_mix[1].reshape(1, d), w_k=w_k.astype(BF16), w_v=w_v.astype(BF16),
              w_q=w_q[0].astype(BF16), k_norm=jnp.tile(k_norm, hk // HEAD_DIM).reshape(1, hk),
              q_norm=jnp.tile(q_norm[0], n_heads).reshape(1, hq),
              k_pool=_head_pool_matrices(hk), q_pool=_head_pool_matrices(hq))
    q_p, k_p, v_p, kx_p, vx_p = _qkv(hp, qw, ffn_rows, True)
    q_s, k_s, v_s = _qkv(hs, qw, ls * ns, False)

    wo = w_o[0].astype(BF16)
    bias_p = _bias_table(rel_bias, WINDOW, WINDOW, 2 * WINDOW)
    bias_first = jnp.where(jnp.arange(2 * WINDOW) >= WINDOW, bias_p, -jnp.inf)
    o_p = _attn_prompt(q_p, kx_p, vx_p, jnp.stack([bias_first, bias_p]), sinks[0].astype(F32), n, l)

    lkp = 2 * WINDOW
    new_rows = 16
    pad_new = lambda a: jnp.pad(tm(a.reshape(ls, ns, hk)), ((0, 0), (0, new_rows - ls), (0, 0)))
    kc = cache_k_win.reshape(ns, lb, hk)
    vc = cache_v_win.reshape(ns, lb, hk)
    bias_s = _bias_table(rel_bias, lb, ls, lb + ls)
    bias_s = jnp.pad(bias_s, ((0, 0), (0, 0), (0, lkp - lb - ls)), constant_values=-jnp.inf)
    bias_s = bias_s.reshape(n_heads * ls, lkp)
    sink_s = jnp.repeat(sinks[0].astype(F32), ls).reshape(n_heads * ls, 1)
    q5 = jnp.transpose(q_s.reshape(ls, ns, N_KV_HEADS, rep, HEAD_DIM), (1, 2, 3, 0, 4))
    q_blk = jnp.einsum('sgrtd,gh->sgrthd', q5, jnp.eye(N_KV_HEADS, dtype=q5.dtype))
    q_blk = q_blk.reshape(ns, n_heads * ls, hk)
    o4, k_win_s, v_win_s = _attn_sample(q_blk, kc, vc, pad_new(k_s), pad_new(v_s), bias_s, sink_s, ls, 8)
    o_s = jnp.transpose(o4.reshape(ns, rep, ls, N_KV_HEADS, HEAD_DIM), (2, 0, 3, 1, 4)).reshape(ls * ns, hq)
    o_s = o_s.astype(BF16)

    hp, hs, conv_p1, conv_s1 = ffn_layer(1, hp, hs, (o_p, wo), (o_s, wo))

    y_prompt = hp.reshape(n, l, d)
    y_sample = tm(hs.reshape(ls, ns, d))
    ssm_shape = (1, -1, n_groups, n_state)
    kvh_shape = (-1, lb, N_KV_HEADS, HEAD_DIM)
    k_win_p = k_p.reshape(n, l, hk)[:, l - WINDOW:].reshape(n, WINDOW, N_KV_HEADS, HEAD_DIM)
    v_win_p = v_p.reshape(n, l, hk)[:, l - WINDOW:].reshape(n, WINDOW, N_KV_HEADS, HEAD_DIM)
    return (y_prompt, y_sample,
            sre_p.reshape(ssm_shape), sim_p.reshape(ssm_shape),
            sre_s.reshape(ssm_shape), sim_s.reshape(ssm_shape),
            jnp.stack([conv_p0, conv_p1]), jnp.stack([conv_s0, conv_s1]),
            k_win_p, v_win_p, k_win_s.reshape(kvh_shape), v_win_s.reshape(kvh_shape))
```

```python
import functools
import math

import jax
import jax.numpy as jnp
from jax import lax
from jax.experimental import pallas as pl
from jax.experimental.pallas import tpu as pltpu

F32 = jnp.float32
BF16 = jnp.bfloat16

EPS = 1e-6
SSM_GROUP = 16
SSM_STATE = 64
HEAD_DIM = 64
N_KV_HEADS = 4
WINDOW = 128
NUM_BUCKETS = 32
MAX_DISTANCE = 128
CONV_WIDTH = 3

LANES = 128
SUBLANES = 8
MXU_DIM = 256
VMEM_LIMIT = 56 * 1024 * 1024

SCAN_COLS = 1024
FF_CHUNK = 256


def _dot(a, b):
    return jnp.dot(a, b, preferred_element_type=F32)


def _dot_nt(a, b):
    return lax.dot_general(a, b, (((1,), (1,)), ((), ())), preferred_element_type=F32)


def _rms(x, g):
    ms = jnp.mean(x * x, axis=-1, keepdims=True)
    return x * lax.rsqrt(ms + EPS) * g


def _head_rms(x, pool, spread, g):
    ms = _dot((x * x).astype(BF16), pool)
    scale = lax.rsqrt(ms + EPS)
    hi = scale.astype(BF16)
    lo = (scale - hi.astype(F32)).astype(BF16)
    return x * (_dot(hi, spread) + _dot(lo, spread)) * g


def _const_spec(shape):
    nd = len(shape)
    return pl.BlockSpec(shape, lambda *_: (0,) * nd, pipeline_mode=pl.Buffered(1))


def _b_project(ub, bre_ref, bim_ref, bure, buim):
    n_kt = bre_ref.shape[0]
    kw = bre_ref.shape[1]
    nw = bre_ref.shape[2]
    for kt in range(n_kt):
        lhs = ub[:, kt * kw:(kt + 1) * kw]
        bure[:, kt * nw:(kt + 1) * nw] = _dot(lhs, bre_ref[kt])
        buim[:, kt * nw:(kt + 1) * nw] = _dot(lhs, bim_ref[kt])


def _scan(bure, buim, are_ref, aim_ref, n_seq, n_steps, init_fn, final_fn, store):
    nst = bure.shape[1]
    for cb in range(nst // SCAN_COLS):
        cs = slice(cb * SCAN_COLS, (cb + 1) * SCAN_COLS)
        ar = jnp.broadcast_to(are_ref[:, cs], (SUBLANES, SCAN_COLS))
        ai = jnp.broadcast_to(aim_ref[:, cs], (SUBLANES, SCAN_COLS))

        def group(g, _, cs=cs, ar=ar, ai=ai):
            r0 = pl.multiple_of(g * SUBLANES, SUBLANES)

            def step(k, carry):
                hr, hi = carry
                row = pl.multiple_of(k * n_seq + r0, SUBLANES)
                br = bure[pl.ds(row, SUBLANES), cs]
                bi = buim[pl.ds(row, SUBLANES), cs]
                nr = ar * hr - ai * hi + br
                ni = ar * hi + ai * hr + bi
                if store:
                    bure[pl.ds(row, SUBLANES), cs] = nr
                    buim[pl.ds(row, SUBLANES), cs] = ni
                return nr, ni

            hr, hi = lax.fori_loop(0, n_steps, step, init_fn(r0, cs), unroll=min(n_steps, 4))
            final_fn(r0, cs, hr, hi)
            return 0

        if n_seq == SUBLANES:
            group(0, 0)
        else:
            lax.fori_loop(0, n_seq // SUBLANES, group, 0)


def _c_project_glu(x, u, bure, buim, cre_ref, ncim_ref, dskip_ref, wglu_ref, bglu_ref):
    d = x.shape[1]
    n_blk = cre_ref.shape[0]
    kw = cre_ref.shape[1]
    ys = []
    for m in range(n_blk):
        hr = bure[:, m * kw:(m + 1) * kw].astype(BF16)
        hi = buim[:, m * kw:(m + 1) * kw].astype(BF16)
        ys.append(_dot(hr, cre_ref[m]) + _dot(hi, ncim_ref[m]))
    y = jnp.concatenate(ys, axis=1) + dskip_ref[...] * u
    z = jax.nn.gelu(y).astype(BF16)
    gl = jnp.concatenate([_dot(z, wglu_ref[nt]) for nt in range(wglu_ref.shape[0])], axis=1) + bglu_ref[...]
    return x + gl[:, :d] * jax.nn.sigmoid(gl[:, d:])


def _ssm_prompt_kernel(x_ref, gmix_ref, are_ref, aim_ref, apw_re_ref, apw_im_ref, bw_ref, cre_ref, ncim_ref,
                       dskip_ref, wglu_ref, bglu_ref,
                       out_ref, sre_ref, sim_ref,
                       slab, xp, ub, us, bu_a, bu_b, zb, gl, ends, hin, car, *, n_steps, pitch):
    g_step = pl.program_id(1)
    for parity, (bu_next, bu_cur) in enumerate(((bu_a, bu_b), (bu_b, bu_a))):
        pl.when(g_step % 2 == parity)(functools.partial(
            _ssm_prompt_step, parity, g_step, x_ref, gmix_ref, are_ref, aim_ref, apw_re_ref, apw_im_ref, bw_ref,
            cre_ref, ncim_ref, dskip_ref, wglu_ref, bglu_ref, out_ref, sre_ref, sim_ref,
            slab, xp, ub, us, bu_next, bu_cur, zb, gl, ends, hin, car, n_steps, pitch))


def _ssm_prompt_step(nxt, g_step, x_ref, gmix_ref, are_ref, aim_ref, apw_re_ref, apw_im_ref, bw_ref, cre_ref,
                     ncim_ref, dskip_ref, wglu_ref, bglu_ref, out_ref, sre_ref, sim_ref,
                     slab, xp, ub, us, bu_next, bu_cur, zb, gl, ends, hin, car, n_steps, pitch):
    cur = 1 - nxt
    n_slab = slab.shape[0]
    d = xp.shape[2]
    n_piece = bu_cur.shape[1]
    blk_pieces = SCAN_COLS // MXU_DIM
    n_blk = n_piece // blk_pieces
    trips = 4
    spt = n_steps // trips
    n_cblk = cre_ref.shape[0]
    n_gtile = wglu_ref.shape[0]
    assert 2 * n_piece == n_blk * trips * 2
    assert 2 * n_gtile == n_blk * trips
    assert bw_ref.shape[2] == MXU_DIM and 2 * n_cblk == n_piece

    def lanes(q):
        return slice(q * MXU_DIM, (q + 1) * MXU_DIM)

    for j in range(SUBLANES):
        for c in range(n_slab):
            slab[c, j * pitch:j * pitch + n_steps, :] = x_ref[j * n_steps:(j + 1) * n_steps,
                                                             c * LANES:(c + 1) * LANES]

    def gather(k, _):
        r0 = pl.multiple_of(k * SUBLANES, SUBLANES)
        for c in range(n_slab):
            xp[nxt, pl.ds(r0, SUBLANES), c * LANES:(c + 1) * LANES] = slab[c, pl.ds(k, SUBLANES, stride=pitch), :]
        return 0

    lax.fori_loop(0, n_steps, gather, 0, unroll=4)
    u = _rms(xp[nxt], gmix_ref[...])
    ub[nxt] = u.astype(BF16)
    for c in range(n_slab):
        us[nxt, c] = u[:, c * LANES:(c + 1) * LANES]

    def run_pass(bu, store, init_fn, end_fn, work_fn):
        for blk in range(n_blk):
            coef = [(jnp.broadcast_to(are_ref[:, lanes(blk * blk_pieces + nt)], (SUBLANES, MXU_DIM)),
                     jnp.broadcast_to(aim_ref[:, lanes(blk * blk_pieces + nt)], (SUBLANES, MXU_DIM)))
                    for nt in range(blk_pieces)]

            def trip(i, state, blk=blk, coef=coef):
                if work_fn is not None:
                    work_fn(blk, i)
                state = list(state)
                for s in range(spt):
                    row = (i * spt + s) * SUBLANES
                    for nt in range(blk_pieces):
                        q = blk * blk_pieces + nt
                        ar, ai = coef[nt]
                        hr, hi = state[2 * nt], state[2 * nt + 1]
                        nr = ar * hr - ai * hi + bu[0, q, pl.ds(row, SUBLANES), :]
                        ni = ar * hi + ai * hr + bu[1, q, pl.ds(row, SUBLANES), :]
                        if store:
                            bu[0, q, pl.ds(row, SUBLANES), :] = nr
                            bu[1, q, pl.ds(row, SUBLANES), :] = ni
                        state[2 * nt], state[2 * nt + 1] = nr, ni
                return tuple(state)

            state = init_fn(blk)
            for i in range(trips):
                state = trip(i, state)
            end_fn(blk, state)

    def zero_init(blk):
        return tuple(jnp.zeros((SUBLANES, MXU_DIM), F32) for _ in range(2 * blk_pieces))

    def keep_ends(blk, state):
        for nt in range(blk_pieces):
            ends[0, :, lanes(blk * blk_pieces + nt)] = state[2 * nt]
            ends[1, :, lanes(blk * blk_pieces + nt)] = state[2 * nt + 1]

    def true_init(blk):
        return tuple(hin[ri, :, lanes(blk * blk_pieces + nt)] for nt in range(blk_pieces) for ri in range(2))

    tiles_per_kt = 2 * blk_pieces

    def b_project_tile(t):
        kt, j = divmod(t, tiles_per_kt)
        ri, nt = divmod(j, blk_pieces)
        q = kt * blk_pieces + nt
        bu_next[ri, q] = _dot(ub[nxt, :, lanes(kt)], bw_ref[ri, q])

    def b_project_slice(blk, i):
        b_project_tile(2 * (blk * trips + i))
        b_project_tile(2 * (blk * trips + i) + 1)

    def c_project_block(m):
        y = dskip_ref[m] * us[cur, m]
        for ri, c_ref in ((0, cre_ref), (1, ncim_ref)):
            for w in range(2):
                y = y + _dot(bu_cur[ri, 2 * m + w].astype(BF16), c_ref[m, w * MXU_DIM:(w + 1) * MXU_DIM, :])
        zb[m] = jax.nn.gelu(y).astype(BF16)

    def glu_slice(blk, i):
        t = blk * trips + i
        if t % 2 == 0:
            nt = t // 2
            z = jnp.concatenate([zb[c] for c in range(n_cblk)], axis=1)
            gl[nt] = _dot(z, wglu_ref[nt]) + bglu_ref[nt]

    def first_tile():
        car[...] = jnp.zeros_like(car)
        for blk in range(n_blk):
            lhs = ub[nxt, :, lanes(blk)]
            for ri in range(2):
                for nt in range(blk_pieces):
                    bu_next[ri, blk * blk_pieces + nt] = _dot(lhs, bw_ref[ri, blk * blk_pieces + nt])
        run_pass(bu_next, False, zero_init, keep_ends, None)

    if nxt == 0:
        pl.when(g_step == 0)(first_tile)

    @pl.when(g_step > 0)
    def _():
        hr = car[0]
        hi = car[1]
        apr = apw_re_ref[...]
        api = apw_im_ref[...]
        for j in range(SUBLANES):
            er = ends[0, j:j + 1, :]
            ei = ends[1, j:j + 1, :]
            hin[0, j:j + 1, :] = hr
            hin[1, j:j + 1, :] = hi
            hr, hi = apr * hr - api * hi + er, apr * hi + api * hr + ei
        car[0] = hr
        car[1] = hi
        sre_ref[...] = hr
        sim_ref[...] = hi

        run_pass(bu_cur, True, true_init, lambda *_: None, b_project_slice)

    @pl.when(g_step != 0)
    def _():
        for m in range(n_cblk):
            c_project_block(m)

    @pl.when(jnp.logical_and(g_step >= 1, pl.program_id(0) >= 0))
    def _():
        run_pass(bu_next, False, zero_init, keep_ends, glu_slice)

        half = n_gtile // 2
        for c in range(half):
            xp[cur, :, lanes(c)] = xp[cur, :, lanes(c)] + gl[c] * jax.nn.sigmoid(gl[half + c])

        def scatter(k, _):
            r0 = pl.multiple_of(k * SUBLANES, SUBLANES)
            for c in range(n_slab):
                slab[c, pl.ds(k, SUBLANES, stride=pitch), :] = xp[cur, pl.ds(r0, SUBLANES), c * LANES:(c + 1) * LANES]
            return 0

        lax.fori_loop(0, n_steps, scatter, 0, unroll=4)
        for j in range(SUBLANES):
            for c in range(n_slab):
                out_ref[j * n_steps:(j + 1) * n_steps, c * LANES:(c + 1) * LANES] = slab[c, j * pitch:j * pitch + n_steps, :]


def _ssm_sample_kernel(x_ref, h0re_ref, h0im_ref, gmix_ref, are_ref, aim_ref, bre_ref, bim_ref, cre_ref,
                       ncim_ref, dskip_ref, wglu_ref, bglu_ref,
                       out_ref, sre_ref, sim_ref, bure, buim, *, n_seq, n_steps):
    x = x_ref[...]
    u = _rms(x, gmix_ref[...])
    _b_project(u.astype(BF16), bre_ref, bim_ref, bure, buim)

    def init(r0, cs):
        return h0re_ref[pl.ds(r0, SUBLANES), cs], h0im_ref[pl.ds(r0, SUBLANES), cs]

    def final(r0, cs, hr, hi):
        sre_ref[pl.ds(r0, SUBLANES), cs] = hr
        sim_ref[pl.ds(r0, SUBLANES), cs] = hi

    _scan(bure, buim, are_ref, aim_ref, n_seq, n_steps, init, final, store=True)
    out_ref[...] = _c_project_glu(x, u, bure, buim, cre_ref, ncim_ref, dskip_ref, wglu_ref, bglu_ref)


def _ssm_params(lam_re, lam_im, log_dt, b_re, b_im, c_re, c_im, n_pow):
    g, p = lam_re.shape
    lr = lam_re.astype(F32)
    li = lam_im.astype(F32)
    dt = jnp.exp(log_dt.astype(F32))[:, None]
    mag = jnp.exp(lr * dt)
    ang = li * dt
    ab_re = mag * jnp.cos(ang)
    ab_im = mag * jnp.sin(ang)
    den = lr * lr + li * li
    nr = ab_re - 1.0
    f_re = (nr * lr + ab_im * li) / den
    f_im = (ab_im * lr - nr * li) / den
    br = b_re.astype(F32)
    bi = b_im.astype(F32)
    bb_re = f_re[..., None] * br - f_im[..., None] * bi
    bb_im = f_re[..., None] * bi + f_im[..., None] * br

    c = bb_re.shape[2]
    gk = MXU_DIM // c
    eye_k = jnp.eye(gk, dtype=F32)

    def b_blocks(bb):
        bt = jnp.transpose(bb, (0, 2, 1)).reshape(g // gk, gk, c, p)
        return jnp.einsum('tgcp,gh->tgchp', bt, eye_k).reshape(g // gk, gk * c, gk * p).astype(BF16)

    def b_tiles(bb):
        rows_kt = jnp.transpose(bb, (0, 2, 1)).reshape(g // gk, gk * c, p)
        per_tile = MXU_DIM // p
        wide = jnp.tile(rows_kt, (1, 1, per_tile))[:, None]
        row_group = jnp.arange(gk * c)[:, None] // c
        col_group = jnp.arange(MXU_DIM)[None, :] // p
        nt = jnp.arange(gk // per_tile)[:, None, None]
        own = (row_group[None] == nt * per_tile + col_group[None]).astype(F32)
        return (wide * own[None]).reshape(g * p // MXU_DIM, gk * c, MXU_DIM).astype(BF16)

    gc = LANES // c
    eye_c = jnp.eye(gc, dtype=F32)

    def c_blocks(cc):
        ct = jnp.transpose(cc.astype(F32), (0, 2, 1)).reshape(g // gc, gc, p, c)
        return jnp.einsum('tgpc,gh->tgphc', ct, eye_c).reshape(g // gc, gc * p, gc * c).astype(BF16)

    pw_re = ab_re.reshape(1, g * p)
    pw_im = ab_im.reshape(1, g * p)
    for _ in range(n_pow.bit_length() - 1):
        pw_re, pw_im = pw_re * pw_re - pw_im * pw_im, 2.0 * pw_re * pw_im
    return dict(a_re=ab_re.reshape(1, g * p), a_im=ab_im.reshape(1, g * p),
                ap_re=pw_re, ap_im=pw_im,
                b_re=b_blocks(bb_re), b_im=b_blocks(bb_im),
                b_tiles=jnp.stack([b_tiles(bb_re), b_tiles(bb_im)]),
                c_re=c_blocks(c_re), nc_im=c_blocks(-c_im))


def _ssm_prompt(x, sp, gmix, dskip, wglu, bglu, n_steps):
    n, l, d = x.shape
    nst = sp['a_re'].shape[1]
    rows = SUBLANES * n_steps
    pitch = n_steps + SUBLANES
    tiles = l // rows
    n_piece = nst // MXU_DIM
    n_gtile = wglu.shape[0]
    consts = [gmix, sp['a_re'], sp['a_im'], sp['ap_re'], sp['ap_im'], sp['b_tiles'], sp['c_re'], sp['nc_im'],
              dskip.reshape(d // LANES, 1, LANES), wglu, bglu.reshape(n_gtile, 1, MXU_DIM)]
    out, sre, sim = pl.pallas_call(
        functools.partial(_ssm_prompt_kernel, n_steps=n_steps, pitch=pitch),
        grid=(n, tiles + 1),
        in_specs=[pl.BlockSpec((None, rows, d), lambda i, g: (i, jnp.minimum(g, tiles - 1), 0))]
        + [_const_spec(c.shape) for c in consts],
        out_specs=[pl.BlockSpec((None, rows, d), lambda i, g: (i, jnp.maximum(g - 1, 0), 0)),
                   pl.BlockSpec((None, 1, nst), lambda i, g: (i, 0, 0)),
                   pl.BlockSpec((None, 1, nst), lambda i, g: (i, 0, 0))],
        out_shape=[jax.ShapeDtypeStruct((n, l, d), F32),
                   jax.ShapeDtypeStruct((n, 1, nst), F32),
                   jax.ShapeDtypeStruct((n, 1, nst), F32)],
        scratch_shapes=[pltpu.VMEM((d // LANES, SUBLANES * pitch, LANES), F32),
                        pltpu.VMEM((2, rows, d), F32),
                        pltpu.VMEM((2, rows, d), BF16),
                        pltpu.VMEM((2, d // LANES, rows, LANES), F32),
                        pltpu.VMEM((2, n_piece, rows, MXU_DIM), F32),
                        pltpu.VMEM((2, n_piece, rows, MXU_DIM), F32),
                        pltpu.VMEM((d // LANES, rows, LANES), BF16),
                        pltpu.VMEM((n_gtile, rows, MXU_DIM), F32),
                        pltpu.VMEM((2, SUBLANES, nst), F32),
                        pltpu.VMEM((2, SUBLANES, nst), F32),
                        pltpu.VMEM((2, 1, nst), F32)],
        compiler_params=pltpu.CompilerParams(dimension_semantics=("arbitrary", "arbitrary"),
                                             vmem_limit_bytes=VMEM_LIMIT),
        name="ssm_prompt",
    )(x, *consts)
    return out, sre[:, 0], sim[:, 0]


def _ssm_sample(x_tm, h0re, h0im, sp, gmix, dskip, wglu, bglu, n_seq, n_steps):
    rows, d = x_tm.shape
    nst = sp['a_re'].shape[1]
    args = [x_tm, h0re, h0im, gmix, sp['a_re'], sp['a_im'], sp['b_re'], sp['b_im'], sp['c_re'], sp['nc_im'],
            dskip, wglu, bglu]
    return pl.pallas_call(
        functools.partial(_ssm_sample_kernel, n_seq=n_seq, n_steps=n_steps),
        grid=(1,),
        in_specs=[_const_spec(a.shape) for a in args],
        out_specs=[pl.BlockSpec((rows, d), lambda i: (0, 0)), pl.BlockSpec((n_seq, nst), lambda i: (0, 0)),
                   pl.BlockSpec((n_seq, nst), lambda i: (0, 0))],
        out_shape=[jax.ShapeDtypeStruct((rows, d), F32),
                   jax.ShapeDtypeStruct((n_seq, nst), F32),
                   jax.ShapeDtypeStruct((n_seq, nst), F32)],
        scratch_shapes=[pltpu.VMEM((rows, nst), F32), pltpu.VMEM((rows, nst), F32)],
        compiler_params=pltpu.CompilerParams(dimension_semantics=("arbitrary",),
                                             vmem_limit_bytes=VMEM_LIMIT),
        name="ssm_sample",
    )(*args)


def _ffn_kernel(*refs, rows, head_rows, shift, has_attn):
    if has_attn:
        h_ref, o_ref, wo_ref = refs[:3]
        refs = refs[3:]
    else:
        h_ref = refs[0]
        refs = refs[1:]
    (p_ref, head_ref, gffn_ref, wup_ref, cw_ref, cb_ref, wdown_ref, gple_ref, wgate_ref, win_ref,
     out_ref, tail_ref, acc, xn_ref, ext_a, ext_b) = refs
    ff = wdown_ref.shape[0]
    n_chunk = ff // FF_CHUNK
    up0 = head_rows
    up1 = head_rows + rows

    @pl.when(pl.program_id(1) == 0)
    def _():
        tail_ref[...] = head_ref[...]

    x = h_ref[...]
    if has_attn:
        x = x + _dot(o_ref[...], wo_ref[...])
    xn_ref[...] = _rms(x, gffn_ref[...]).astype(BF16)
    acc[...] = jnp.zeros_like(acc)

    n_slab = ext_a.shape[1]

    def slab_cols(c, half, s):
        c0 = half * ff + c * FF_CHUNK + s * LANES
        return slice(c0, c0 + LANES)

    def up_project(c, ext):
        for half in range(2):
            c0 = half * ff + c * FF_CHUNK
            up = _dot(xn_ref[...], wup_ref[:, c0:c0 + FF_CHUNK])
            for s in range(n_slab):
                ext[half, s, 0:up0, :] = tail_ref[:, slab_cols(c, half, s)]
                ext[half, s, up0:up1, :] = up[:, s * LANES:(s + 1) * LANES]

    def conv(ext, half, c):
        parts = []
        for s in range(n_slab):
            ls = slab_cols(c, half, s)
            t2 = ext[half, s, up0 - 2 * shift:up1 - 2 * shift, :]
            t1 = ext[half, s, up0 - shift:up1 - shift, :]
            tail_ref[:, ls] = ext[half, s, rows:up1, :]
            r = cb_ref[:, ls] + t2 * cw_ref[0:1, ls]
            r = r + t1 * cw_ref[1:2, ls]
            parts.append(r + ext[half, s, up0:up1, :] * cw_ref[2:3, ls])
        return jnp.concatenate(parts, axis=1)

    def down_project(c, ext):
        cg = conv(ext, 0, c)
        cv = conv(ext, 1, c)
        act = (cg * jax.nn.sigmoid(cg) * cv).astype(BF16)
        acc[...] += _dot(act, wdown_ref[c * FF_CHUNK:(c + 1) * FF_CHUNK, :])

    bufs = (ext_a, ext_b)
    up_project(0, bufs[0])
    for c in range(n_chunk):
        if c + 1 < n_chunk:
            up_project(c + 1, bufs[(c + 1) % 2])
        down_project(c, bufs[c % 2])

    h2 = x + acc[...]
    gate = jax.nn.sigmoid(_dot(_rms(h2, gple_ref[...]).astype(BF16), wgate_ref[...]))
    pe = _dot(p_ref[...].astype(BF16), win_ref[...])
    out_ref[...] = h2 + pe * gate


def _ffn_weights(g_ffn, w_up, conv_w, conv_b, w_down, g_ple, w_gate, w_in):
    depth, d, ff2 = w_up.shape
    return dict(
        g_ffn=g_ffn.reshape(depth, 1, d),
        w_up=w_up.astype(BF16),
        conv_w=conv_w,
        conv_b=conv_b.reshape(depth, 1, ff2),
        w_down=w_down.astype(BF16),
        g_ple=g_ple.reshape(depth, 1, d),
        w_gate=w_gate.astype(BF16),
        w_in=w_in.astype(BF16),
    )


def _layer_spec(shape, layer):
    nd = len(shape)
    return pl.BlockSpec((None,) + tuple(shape[1:]), lambda *_: (layer,) + (0,) * (nd - 1),
                        pipeline_mode=pl.Buffered(1))


def _ffn(h, p, layer, head, head_off, fw, rows, n_seq_tiles, shift, attn=None):
    total, d = h.shape
    tiles = total // rows // n_seq_tiles
    head_rows, ff2 = head.shape[1], head.shape[2]
    row_map = lambda i, t: (i * tiles + t, 0)
    p_map = lambda i, t: (layer * (total // rows) + i * tiles + t, 0)
    consts = [fw['g_ffn'], fw['w_up'], fw['conv_w'], fw['conv_b'], fw['w_down'], fw['g_ple'], fw['w_gate'],
              fw['w_in']]
    args = [h]
    in_specs = [pl.BlockSpec((rows, d), row_map)]
    if attn is not None:
        o, wo = attn
        args += [o, wo]
        in_specs += [pl.BlockSpec((rows, o.shape[1]), row_map), _const_spec(wo.shape)]
    args += [p, head] + consts
    in_specs += [pl.BlockSpec((rows, p.shape[1]), p_map),
                 pl.BlockSpec((None, head_rows, ff2), lambda i, t: (head_off + i, 0, 0))]
    in_specs += [_layer_spec(c.shape, layer) for c in consts]
    out, tail = pl.pallas_call(
        functools.partial(_ffn_kernel, rows=rows, head_rows=head_rows, shift=shift, has_attn=attn is not None),
        grid=(n_seq_tiles, tiles),
        in_specs=in_specs,
        out_specs=[pl.BlockSpec((rows, d), row_map),
                   pl.BlockSpec((None, head_rows, ff2), lambda i, t: (i, 0, 0))],
        out_shape=[jax.ShapeDtypeStruct((total, d), F32),
                   jax.ShapeDtypeStruct((n_seq_tiles, head_rows, ff2), F32)],
        scratch_shapes=[pltpu.VMEM((rows, d), F32),
                        pltpu.VMEM((rows, d), BF16),
                        pltpu.VMEM((2, FF_CHUNK // LANES, head_rows + rows, LANES), F32),
                        pltpu.VMEM((2, FF_CHUNK // LANES, head_rows + rows, LANES), F32)],
        compiler_params=pltpu.CompilerParams(dimension_semantics=("arbitrary", "arbitrary"),
                                             vmem_limit_bytes=VMEM_LIMIT),
        name="ffn_ple",
    )(*args)
    return out, tail


def _pair_spread(x, fill):
    assert x.shape[1] == 2 * LANES and HEAD_DIM * 2 == LANES
    r = pltpu.roll(x, HEAD_DIM, 1)
    low = lax.broadcasted_iota(jnp.int32, (x.shape[0], LANES), 1) < HEAD_DIM
    x01, x23 = x[:, :LANES], x[:, LANES:]
    r30, r12 = r[:, :LANES], r[:, LANES:]
    first = lambda t: jnp.where(low, t, fill)
    second = lambda t: jnp.where(low, fill, t)
    return jnp.concatenate([first(x01), second(r30), first(r12), second(x01),
                            first(x23), second(r12), first(r30), second(x23)], axis=1)


def _qkv_kernel(*refs, paired):
    (h_ref, gkv_ref, gmix_ref, wk_ref, wv_ref, wq_ref, knorm_ref, qnorm_ref, kpool_ref, kspread_ref,
     qpool_ref, qspread_ref, q_ref, k_ref, v_ref) = refs[:15]
    x = h_ref[...]
    s = _rms(x, gkv_ref[...]).astype(BF16)
    k = _head_rms(_dot(s, wk_ref[...]), kpool_ref[...], kspread_ref[...], knorm_ref[...])
    v = _dot(s, wv_ref[...])
    xn = _rms(x, gmix_ref[...]).astype(BF16)
    q = _head_rms(_dot(xn, wq_ref[...]), qpool_ref[...], qspread_ref[...], qnorm_ref[...])
    if paired:
        kx_ref, vx_ref = refs[15:]
        kx_ref[...] = _pair_spread(k, 0.0).astype(BF16)
        vx_ref[...] = _pair_spread(v, 1.0).astype(BF16)
    k_ref[...] = k
    v_ref[...] = v
    q_ref[...] = (q * (HEAD_DIM ** -0.5)).astype(q_ref.dtype)


def _head_pool_matrices(n):
    member = (jnp.arange(n)[:, None] // HEAD_DIM) == jnp.arange(LANES)[None, :]
    return member.astype(BF16) * (1.0 / HEAD_DIM), member.T.astype(BF16)


def _qkv(h, qw, rows, paired):
    total, d = h.shape
    hk = qw['w_k'].shape[1]
    hq = qw['w_q'].shape[1]
    consts = [qw['g_kv'], qw['g_mix'], qw['w_k'], qw['w_v'], qw['w_q'], qw['k_norm'], qw['q_norm'],
              *qw['k_pool'], *qw['q_pool']]
    row_map = lambda i: (i, 0)
    out_specs = [pl.BlockSpec((rows, hq), row_map), pl.BlockSpec((rows, hk), row_map),
                 pl.BlockSpec((rows, hk), row_map)]
    out_shape = [jax.ShapeDtypeStruct((total, hq), BF16),
                 jax.ShapeDtypeStruct((total, hk), F32),
                 jax.ShapeDtypeStruct((total, hk), F32)]
    if paired:
        out_specs += [pl.BlockSpec((rows, 4 * hk), row_map)] * 2
        out_shape += [jax.ShapeDtypeStruct((total, 4 * hk), BF16)] * 2
    return pl.pallas_call(
        functools.partial(_qkv_kernel, paired=paired),
        grid=(total // rows,),
        in_specs=[pl.BlockSpec((rows, d), row_map)] + [_const_spec(c.shape) for c in consts],
        out_specs=out_specs,
        out_shape=out_shape,
        compiler_params=pltpu.CompilerParams(dimension_semantics=("arbitrary",),
                                             vmem_limit_bytes=VMEM_LIMIT),
        name="qkv",
    )(h, *consts)


def _t5_bucket(dist):
    max_exact = NUM_BUCKETS // 2
    df = jnp.maximum(dist, 1).astype(F32)
    large = max_exact + (jnp.log(df / max_exact) / math.log(MAX_DISTANCE / max_exact)
                         * (NUM_BUCKETS - max_exact)).astype(jnp.int32)
    return jnp.where(dist < max_exact, dist, jnp.minimum(large, NUM_BUCKETS - 1))


def _bias_table(rel_bias, lb, qb, lk):
    dist = lb + jnp.arange(qb)[:, None] - jnp.arange(lk)[None, :]
    per_dist = rel_bias[_t5_bucket(jnp.arange(WINDOW + 1))].astype(F32)
    onehot = (jnp.clip(dist, 0, WINDOW)[..., None] == jnp.arange(WINDOW + 1)).astype(F32)
    bias = jnp.einsum('qkd,dh->hqk', onehot, per_dist, precision=lax.Precision.HIGHEST)
    ok = (dist >= 0) & (dist <= WINDOW)
    return jnp.where(ok[None], bias, -jnp.inf)


def _softmax_pv(s, sink, v):
    m = jnp.maximum(jnp.max(s, axis=-1, keepdims=True), sink)
    pr = jnp.exp(s - m)
    den = jnp.sum(pr, axis=-1, keepdims=True) + jnp.exp(sink - m)
    return _dot(pr.astype(BF16), v) / den


def _attn_prompt_kernel(sink_ref, q_ref, kx_ref, vx_ref, bias_ref, o_ref, kx_prev, vx_prev):
    qb = q_ref.shape[0]
    n_heads = bias_ref.shape[0]
    rep = n_heads // N_KV_HEADS
    grp = 4 * HEAD_DIM

    @pl.when(pl.program_id(1) == 0)
    def _():
        kx_prev[...] = jnp.zeros_like(kx_prev)
        vx_prev[...] = jnp.zeros_like(vx_prev)

    low = lax.broadcasted_iota(jnp.int32, (qb, LANES), 1) < HEAD_DIM
    for g in range(N_KV_HEADS):
        gs = slice(g * grp, (g + 1) * grp)
        kx = jnp.concatenate([kx_prev[:, gs], kx_ref[:, gs]], axis=0)
        vx = jnp.concatenate([vx_prev[:, gs], vx_ref[:, gs]], axis=0)
        for pair in range(rep // 2):
            h0 = g * rep + 2 * pair
            ls = slice(h0 * HEAD_DIM, h0 * HEAD_DIM + LANES)
            qp = q_ref[:, ls]
            res = []
            for h, kh in ((h0, kx[:, :LANES]), (h0 + 1, kx[:, LANES:])):
                s = _dot_nt(qp, kh) + bias_ref[h]
                m = jnp.maximum(jnp.max(s, axis=-1, keepdims=True), sink_ref[h])
                res.append((_dot(jnp.exp(s - m).astype(BF16), vx), jnp.exp(sink_ref[h] - m)))
            (ra, ea), (rb, eb) = res
            num = jnp.where(low, ra[:, :LANES], rb[:, LANES:])
            den = jnp.where(low, ra[:, LANES:], rb[:, :LANES]) + jnp.where(low, ea, eb)
            o_ref[:, ls] = (num / den).astype(o_ref.dtype)
    kx_prev[...] = kx_ref[...]
    vx_prev[...] = vx_ref[...]


def _attn_prompt(q, kx, vx, bias, sinks, n, l):
    hq = q.shape[1]
    wx = kx.shape[1]
    qb = WINDOW
    nb = l // qb
    cur = lambda i, b, *_: (i * nb + b, 0)
    return pl.pallas_call(
        _attn_prompt_kernel,
        grid_spec=pltpu.PrefetchScalarGridSpec(
            num_scalar_prefetch=1,
            grid=(n, nb),
            in_specs=[pl.BlockSpec((qb, hq), cur), pl.BlockSpec((qb, wx), cur), pl.BlockSpec((qb, wx), cur),
                      pl.BlockSpec((None,) + bias.shape[1:], lambda i, b, *_: (jnp.minimum(b, 1), 0, 0, 0))],
            out_specs=pl.BlockSpec((qb, hq), cur),
            scratch_shapes=[pltpu.VMEM((qb, wx), BF16), pltpu.VMEM((qb, wx), BF16)]),
        out_shape=jax.ShapeDtypeStruct((n * l, hq), BF16),
        compiler_params=pltpu.CompilerParams(dimension_semantics=("arbitrary", "arbitrary"),
                                             vmem_limit_bytes=VMEM_LIMIT),
        name="attn_prompt",
    )(sinks, q, kx, vx, bias)


def _attn_sample_kernel(sink_ref, q_ref, kc_ref, vc_ref, kn_ref, vn_ref, bias_ref,
                        o_ref, kwin_ref, vwin_ref, *, n_new):
    lb = kc_ref.shape[1]
    new_rows = kn_ref.shape[1]
    rows = q_ref.shape[1]
    hk = kc_ref.shape[2]
    grp = rows // N_KV_HEADS
    zero_rows = jnp.zeros((bias_ref.shape[1] - lb - new_rows, hk), BF16)
    lane_head = lax.broadcasted_iota(jnp.int32, (grp, hk), 1) // HEAD_DIM
    samples = range(q_ref.shape[0])
    scores = []
    for s_i in samples:
        kk = jnp.concatenate([kc_ref[s_i].astype(BF16), kn_ref[s_i].astype(BF16), zero_rows], axis=0)
        scores.append(_dot_nt(q_ref[s_i], kk) + bias_ref[...])
    probs = []
    for s in scores:
        m = jnp.maximum(jnp.max(s, axis=-1, keepdims=True), sink_ref[...])
        pr = jnp.exp(s - m)
        probs.append((pr.astype(BF16), jnp.sum(pr, axis=-1, keepdims=True) + jnp.exp(sink_ref[...] - m)))
    for s_i, (pr, den) in zip(samples, probs):
        vv = jnp.concatenate([vc_ref[s_i].astype(BF16), vn_ref[s_i].astype(BF16), zero_rows], axis=0)
        pv = _dot(pr, vv) / den
        o = jnp.zeros((grp, hk), F32)
        for g in range(N_KV_HEADS):
            o = o + jnp.where(lane_head == g, pv[g * grp:(g + 1) * grp, :], 0.0)
        o_ref[s_i] = o
    for s_i in samples:
        kwin_ref[s_i, 0:lb - n_new, :] = kc_ref[s_i, n_new:lb, :]
        kwin_ref[s_i, lb - n_new:lb, :] = kn_ref[s_i, 0:n_new, :]
        vwin_ref[s_i, 0:lb - n_new, :] = vc_ref[s_i, n_new:lb, :]
        vwin_ref[s_i, lb - n_new:lb, :] = vn_ref[s_i, 0:n_new, :]


def _attn_sample(q_blk, kc, vc, kn, vn, bias, sink, n_new, block):
    ns, rows, hk = q_blk.shape
    lb = kc.shape[1]
    lkp = bias.shape[1]
    grp = rows // N_KV_HEADS
    per_s = lambda i: (i, 0, 0)
    return pl.pallas_call(
        functools.partial(_attn_sample_kernel, n_new=n_new),
        grid=(ns // block,),
        in_specs=[_const_spec(sink.shape),
                  pl.BlockSpec((block, rows, hk), per_s),
                  pl.BlockSpec((block, lb, hk), per_s), pl.BlockSpec((block, lb, hk), per_s),
                  pl.BlockSpec((block,) + kn.shape[1:], per_s), pl.BlockSpec((block,) + vn.shape[1:], per_s),
                  _const_spec(bias.shape)],
        out_specs=[pl.BlockSpec((block, grp, hk), per_s),
                   pl.BlockSpec((block, lb, hk), per_s), pl.BlockSpec((block, lb, hk), per_s)],
        out_shape=[jax.ShapeDtypeStruct((ns, grp, hk), F32),
                   jax.ShapeDtypeStruct((ns, lb, hk), F32),
                   jax.ShapeDtypeStruct((ns, lb, hk), F32)],
        compiler_params=pltpu.CompilerParams(dimension_semantics=("arbitrary",),
                                             vmem_limit_bytes=VMEM_LIMIT),
        name="attn_sample",
    )(sink, q_blk, kc, vc, kn, vn, bias)


def kernel(x_prompt, x_sample, state_ssm_re, state_ssm_im, state_ffn_conv, cache_k_win, cache_v_win, p_prompt, p_sample, g_mix, g_ffn, g_ple, ssm_lam_re, ssm_lam_im, ssm_log_dt, ssm_b_re, ssm_b_im, ssm_c_re, ssm_c_im, ssm_d, w_glu, b_glu, g_kv, w_k, w_v, k_norm, w_q, q_norm, sinks, w_o, rel_bias, w_up, conv_w, conv_b, w_down, w_ple_in, w_ple_gate):
    n, l, d = x_prompt.shape
    ns, ls, _ = x_sample.shape
    n_groups, n_state = ssm_lam_re.shape[1:]
    nst = n_groups * n_state
    ff2 = w_up.shape[2]
    hk = w_k.shape[1]
    hq = w_q.shape[2]
    n_heads = hq // HEAD_DIM
    rep = n_heads // N_KV_HEADS
    lb = cache_k_win.shape[1]

    ssm_steps = min(32, l // SUBLANES)
    ffn_rows = min(512, l)
    head_rows_p = SUBLANES
    tm = lambda a: jnp.swapaxes(a, 0, 1)

    sp = _ssm_params(ssm_lam_re[0], ssm_lam_im[0], ssm_log_dt[0], ssm_b_re[0], ssm_b_im[0], ssm_c_re[0],
                     ssm_c_im[0], ssm_steps)
    gmix0 = g_mix[0].reshape(1, d)
    dskip = ssm_d[0].reshape(1, d)
    wglu = jnp.transpose(w_glu[0].astype(BF16).reshape(d, 2 * d // MXU_DIM, MXU_DIM), (1, 0, 2))
    bglu = b_glu[0].reshape(1, 2 * d)
    hp, sre_p, sim_p = _ssm_prompt(x_prompt, sp, gmix0, dskip, wglu, bglu, ssm_steps)
    hp = hp.reshape(n * l, d)
    xs_tm = tm(x_sample).reshape(ls * ns, d)
    hs, sre_s, sim_s = _ssm_sample(xs_tm, state_ssm_re[0].reshape(ns, nst), state_ssm_im[0].reshape(ns, nst),
                                   sp, gmix0, dskip, wglu, bglu, ns, ls)

    pp = p_prompt.reshape(p_prompt.shape[0] * n * l, -1)
    ps = jnp.swapaxes(p_sample, 1, 2).reshape(p_sample.shape[0] * ls * ns, -1)
    zero_head = jnp.zeros((n, head_rows_p, ff2), F32)
    conv_s_tm = jnp.swapaxes(state_ffn_conv, 1, 2).reshape(state_ffn_conv.shape[0], (CONV_WIDTH - 1) * ns, ff2)

    fw = _ffn_weights(g_ffn, w_up, conv_w, conv_b, w_down, g_ple, w_ple_gate, w_ple_in)

    def ffn_layer(i, hp, hs, attn_p=None, attn_s=None):
        hp, tail_p = _ffn(hp, pp, i, zero_head, 0, fw, ffn_rows, n, 1, attn_p)
        hs, tail_s = _ffn(hs, ps, i, conv_s_tm, i, fw, ls * ns, 1, ns, attn_s)
        conv_p = tail_p[:, head_rows_p - (CONV_WIDTH - 1):, :]
        return hp, hs, conv_p, tail_s.reshape(CONV_WIDTH - 1, ns, ff2)

    hp, hs, conv_p0, conv_s0 = ffn_layer(0, hp, hs)

    qw = dict(g_kv=g_kv.reshape(1, d), g_mix=g_mix[1].reshape(1, d), w_k=w_k.astype(BF16), w_v=w_v.astype(BF16),
              w_q=w_q[0].astype(BF16), k_norm=jnp.tile(k_norm, hk // HEAD_DIM).reshape(1, hk),
              q_norm=jnp.tile(q_norm[0], n_heads).reshape(1, hq),
              k_pool=_head_pool_matrices(hk), q_pool=_head_pool_matrices(hq))
    q_p, k_p, v_p, kx_p, vx_p = _qkv(hp, qw, ffn_rows, True)
    q_s, k_s, v_s = _qkv(hs, qw, ls * ns, False)

    wo = w_o[0].astype(BF16)
    bias_p = _bias_table(rel_bias, WINDOW, WINDOW, 2 * WINDOW)
    bias_first = jnp.where(jnp.arange(2 * WINDOW) >= WINDOW, bias_p, -jnp.inf)
    o_p = _attn_prompt(q_p, kx_p, vx_p, jnp.stack([bias_first, bias_p]), sinks[0].astype(F32), n, l)

    lkp = 2 * WINDOW
    new_rows = 16
    pad_new = lambda a: jnp.pad(tm(a.reshape(ls, ns, hk)), ((0, 0), (0, new_rows - ls), (0, 0)))
    kc = cache_k_win.reshape(ns, lb, hk)
    vc = cache_v_win.reshape(ns, lb, hk)
    bias_s = _bias_table(rel_bias, lb, ls, lb + ls)
    bias_s = jnp.pad(bias_s, ((0, 0), (0, 0), (0, lkp - lb - ls)), constant_values=-jnp.inf)
    bias_s = bias_s.reshape(n_heads * ls, lkp)
    sink_s = jnp.repeat(sinks[0].astype(F32), ls).reshape(n_heads * ls, 1)
    q5 = jnp.transpose(q_s.reshape(ls, ns, N_KV_HEADS, rep, HEAD_DIM), (1, 2, 3, 0, 4))
    q_blk = jnp.einsum('sgrtd,gh->sgrthd', q5, jnp.eye(N_KV_HEADS, dtype=q5.dtype))
    q_blk = q_blk.reshape(ns, n_heads * ls, hk)
    o4, k_win_s, v_win_s = _attn_sample(q_blk, kc, vc, pad_new(k_s), pad_new(v_s), bias_s, sink_s, ls, 8)
    o_s = jnp.transpose(o4.reshape(ns, rep, ls, N_KV_HEADS, HEAD_DIM), (2, 0, 3, 1, 4)).reshape(ls * ns, hq)
    o_s = o_s.astype(BF16)

    hp, hs, conv_p1, conv_s1 = ffn_layer(1, hp, hs, (o_p, wo), (o_s, wo))

    y_prompt = hp.reshape(n, l, d)
    y_sample = tm(hs.reshape(ls, ns, d))
    ssm_shape = (1, -1, n_groups, n_state)
    kvh_shape = (-1, lb, N_KV_HEADS, HEAD_DIM)
    k_win_p = k_p.reshape(n, l, hk)[:, l - WINDOW:].reshape(n, WINDOW, N_KV_HEADS, HEAD_DIM)
    v_win_p = v_p.reshape(n, l, hk)[:, l - WINDOW:].reshape(n, WINDOW, N_KV_HEADS, HEAD_DIM)
    return (y_prompt, y_sample,
            sre_p.reshape(ssm_shape), sim_p.reshape(ssm_shape),
            sre_s.reshape(ssm_shape), sim_s.reshape(ssm_shape),
            jnp.stack([conv_p0, conv_p1]), jnp.swapaxes(jnp.stack([conv_s0, conv_s1]), 1, 2),
            k_win_p, v_win_p, k_win_s.reshape(kvh_shape), v_win_s.reshape(kvh_shape))
```

```python
import functools
import math

import jax
import jax.numpy as jnp
from jax import lax
from jax.experimental import pallas as pl
from jax.experimental.pallas import tpu as pltpu

F32 = jnp.float32
BF16 = jnp.bfloat16

EPS = 1e-6
SSM_GROUP = 16
SSM_STATE = 64
HEAD_DIM = 64
N_KV_HEADS = 4
WINDOW = 128
NUM_BUCKETS = 32
MAX_DISTANCE = 128
CONV_WIDTH = 3

LANES = 128
SUBLANES = 8
MXU_DIM = 256
VMEM_LIMIT = 56 * 1024 * 1024

SCAN_COLS = 1024
FF_CHUNK = 256


def _dot(a, b):
    return jnp.dot(a, b, preferred_element_type=F32)


def _dot_nt(a, b):
    return lax.dot_general(a, b, (((1,), (1,)), ((), ())), preferred_element_type=F32)


def _rms(x, g):
    ms = jnp.mean(x * x, axis=-1, keepdims=True)
    return x * lax.rsqrt(ms + EPS) * g


def _head_rms(x, pool, spread, g):
    ms = _dot((x * x).astype(BF16), pool)
    scale = lax.rsqrt(ms + EPS)
    hi = scale.astype(BF16)
    lo = (scale - hi.astype(F32)).astype(BF16)
    return x * (_dot(hi, spread) + _dot(lo, spread)) * g


def _const_spec(shape):
    nd = len(shape)
    return pl.BlockSpec(shape, lambda *_: (0,) * nd, pipeline_mode=pl.Buffered(1))


def _b_project(ub, bw_ref, bure, buim):
    n_piece, kw, nw = bw_ref.shape[1:]
    per_kt = n_piece * kw // ub.shape[1]
    for q in range(n_piece):
        lhs = ub[:, (q // per_kt) * kw:(q // per_kt + 1) * kw]
        bure[:, q * nw:(q + 1) * nw] = _dot(lhs, bw_ref[0, q])
        buim[:, q * nw:(q + 1) * nw] = _dot(lhs, bw_ref[1, q])


def _scan(bure, buim, are_ref, aim_ref, n_seq, n_steps, init_fn, final_fn, store):
    nst = bure.shape[1]
    for cb in range(nst // SCAN_COLS):
        cs = slice(cb * SCAN_COLS, (cb + 1) * SCAN_COLS)
        ar = jnp.broadcast_to(are_ref[:, cs], (SUBLANES, SCAN_COLS))
        ai = jnp.broadcast_to(aim_ref[:, cs], (SUBLANES, SCAN_COLS))

        def group(g, _, cs=cs, ar=ar, ai=ai):
            r0 = pl.multiple_of(g * SUBLANES, SUBLANES)

            def step(k, carry):
                hr, hi = carry
                row = pl.multiple_of(k * n_seq + r0, SUBLANES)
                br = bure[pl.ds(row, SUBLANES), cs]
                bi = buim[pl.ds(row, SUBLANES), cs]
                nr = ar * hr - ai * hi + br
                ni = ar * hi + ai * hr + bi
                if store:
                    bure[pl.ds(row, SUBLANES), cs] = nr
                    buim[pl.ds(row, SUBLANES), cs] = ni
                return nr, ni

            hr, hi = lax.fori_loop(0, n_steps, step, init_fn(r0, cs), unroll=min(n_steps, 4))
            final_fn(r0, cs, hr, hi)
            return 0

        if n_seq == SUBLANES:
            group(0, 0)
        else:
            lax.fori_loop(0, n_seq // SUBLANES, group, 0)


def _c_project_glu(x, u, bure, buim, cre_ref, ncim_ref, dskip_ref, wglu_ref, bglu_ref):
    d = x.shape[1]
    n_blk = cre_ref.shape[0]
    kw = cre_ref.shape[1]
    ys = []
    for m in range(n_blk):
        hr = bure[:, m * kw:(m + 1) * kw].astype(BF16)
        hi = buim[:, m * kw:(m + 1) * kw].astype(BF16)
        ys.append(_dot(hr, cre_ref[m]) + _dot(hi, ncim_ref[m]))
    y = jnp.concatenate(ys, axis=1) + dskip_ref[...] * u
    z = jax.nn.gelu(y).astype(BF16)
    gl = jnp.concatenate([_dot(z, wglu_ref[nt]) for nt in range(wglu_ref.shape[0])], axis=1) + bglu_ref[...]
    return x + gl[:, :d] * jax.nn.sigmoid(gl[:, d:])


def _ssm_prompt_kernel(x_ref, gmix_ref, are_ref, aim_ref, apw_re_ref, apw_im_ref, bw_ref, cre_ref, ncim_ref,
                       dskip_ref, wglu_ref, bglu_ref,
                       out_ref, sre_ref, sim_ref,
                       slab, xp, ub, us, bu_a, bu_b, zb, gl, ends, hin, car, *, n_steps, pitch):
    g_step = pl.program_id(1)
    for parity, (bu_next, bu_cur) in enumerate(((bu_a, bu_b), (bu_b, bu_a))):
        pl.when(g_step % 2 == parity)(functools.partial(
            _ssm_prompt_step, parity, g_step, x_ref, gmix_ref, are_ref, aim_ref, apw_re_ref, apw_im_ref, bw_ref,
            cre_ref, ncim_ref, dskip_ref, wglu_ref, bglu_ref, out_ref, sre_ref, sim_ref,
            slab, xp, ub, us, bu_next, bu_cur, zb, gl, ends, hin, car, n_steps, pitch))


def _ssm_prompt_step(nxt, g_step, x_ref, gmix_ref, are_ref, aim_ref, apw_re_ref, apw_im_ref, bw_ref, cre_ref,
                     ncim_ref, dskip_ref, wglu_ref, bglu_ref, out_ref, sre_ref, sim_ref,
                     slab, xp, ub, us, bu_next, bu_cur, zb, gl, ends, hin, car, n_steps, pitch):
    cur = 1 - nxt
    n_slab = slab.shape[0]
    d = xp.shape[2]
    n_piece = bu_cur.shape[1]
    blk_pieces = SCAN_COLS // MXU_DIM
    n_blk = n_piece // blk_pieces
    trips = 4
    spt = n_steps // trips
    n_cblk = cre_ref.shape[0]
    n_gtile = wglu_ref.shape[0]
    assert 2 * n_piece == n_blk * trips * 2
    assert 2 * n_gtile == n_blk * trips
    assert bw_ref.shape[2] == MXU_DIM and 2 * n_cblk == n_piece

    def lanes(q):
        return slice(q * MXU_DIM, (q + 1) * MXU_DIM)

    for j in range(SUBLANES):
        for c in range(n_slab):
            slab[c, j * pitch:j * pitch + n_steps, :] = x_ref[j * n_steps:(j + 1) * n_steps,
                                                             c * LANES:(c + 1) * LANES]

    def gather(k, _):
        r0 = pl.multiple_of(k * SUBLANES, SUBLANES)
        for c in range(n_slab):
            xp[nxt, pl.ds(r0, SUBLANES), c * LANES:(c + 1) * LANES] = slab[c, pl.ds(k, SUBLANES, stride=pitch), :]
        return 0

    lax.fori_loop(0, n_steps, gather, 0, unroll=4)
    u = _rms(xp[nxt], gmix_ref[...])
    ub[nxt] = u.astype(BF16)
    for c in range(n_slab):
        us[nxt, c] = u[:, c * LANES:(c + 1) * LANES]

    def run_pass(bu, store, init_fn, end_fn, work_fn):
        for blk in range(n_blk):
            coef = [(jnp.broadcast_to(are_ref[:, lanes(blk * blk_pieces + nt)], (SUBLANES, MXU_DIM)),
                     jnp.broadcast_to(aim_ref[:, lanes(blk * blk_pieces + nt)], (SUBLANES, MXU_DIM)))
                    for nt in range(blk_pieces)]

            def trip(i, state, blk=blk, coef=coef):
                if work_fn is not None:
                    work_fn(blk, i)
                state = list(state)
                for s in range(spt):
                    row = (i * spt + s) * SUBLANES
                    for nt in range(blk_pieces):
                        q = blk * blk_pieces + nt
                        ar, ai = coef[nt]
                        hr, hi = state[2 * nt], state[2 * nt + 1]
                        nr = ar * hr - ai * hi + bu[0, q, pl.ds(row, SUBLANES), :]
                        ni = ar * hi + ai * hr + bu[1, q, pl.ds(row, SUBLANES), :]
                        if store:
                            bu[0, q, pl.ds(row, SUBLANES), :] = nr
                            bu[1, q, pl.ds(row, SUBLANES), :] = ni
                        state[2 * nt], state[2 * nt + 1] = nr, ni
                return tuple(state)

            state = init_fn(blk)
            for i in range(trips):
                state = trip(i, state)
            end_fn(blk, state)

    def zero_init(blk):
        return tuple(jnp.zeros((SUBLANES, MXU_DIM), F32) for _ in range(2 * blk_pieces))

    def keep_ends(blk, state):
        for nt in range(blk_pieces):
            ends[0, :, lanes(blk * blk_pieces + nt)] = state[2 * nt]
            ends[1, :, lanes(blk * blk_pieces + nt)] = state[2 * nt + 1]

    def true_init(blk):
        return tuple(hin[ri, :, lanes(blk * blk_pieces + nt)] for nt in range(blk_pieces) for ri in range(2))

    tiles_per_kt = 2 * blk_pieces

    def b_project_tile(t):
        kt, j = divmod(t, tiles_per_kt)
        ri, nt = divmod(j, blk_pieces)
        q = kt * blk_pieces + nt
        bu_next[ri, q] = _dot(ub[nxt, :, lanes(kt)], bw_ref[ri, q])

    def b_project_slice(blk, i):
        b_project_tile(2 * (blk * trips + i))
        b_project_tile(2 * (blk * trips + i) + 1)

    def c_project_block(m):
        y = dskip_ref[m] * us[cur, m]
        for ri, c_ref in ((0, cre_ref), (1, ncim_ref)):
            for w in range(2):
                y = y + _dot(bu_cur[ri, 2 * m + w].astype(BF16), c_ref[m, w * MXU_DIM:(w + 1) * MXU_DIM, :])
        zb[m] = jax.nn.gelu(y).astype(BF16)

    def glu_slice(blk, i):
        t = blk * trips + i
        if t % 2 == 0:
            nt = t // 2
            z = jnp.concatenate([zb[c] for c in range(n_cblk)], axis=1)
            gl[nt] = _dot(z, wglu_ref[nt]) + bglu_ref[nt]

    def first_tile():
        car[...] = jnp.zeros_like(car)
        for blk in range(n_blk):
            lhs = ub[nxt, :, lanes(blk)]
            for ri in range(2):
                for nt in range(blk_pieces):
                    bu_next[ri, blk * blk_pieces + nt] = _dot(lhs, bw_ref[ri, blk * blk_pieces + nt])
        run_pass(bu_next, False, zero_init, keep_ends, None)

    if nxt == 0:
        pl.when(g_step == 0)(first_tile)

    @pl.when(g_step > 0)
    def _():
        hr = car[0]
        hi = car[1]
        apr = apw_re_ref[...]
        api = apw_im_ref[...]
        for j in range(SUBLANES):
            er = ends[0, j:j + 1, :]
            ei = ends[1, j:j + 1, :]
            hin[0, j:j + 1, :] = hr
            hin[1, j:j + 1, :] = hi
            hr, hi = apr * hr - api * hi + er, apr * hi + api * hr + ei
        car[0] = hr
        car[1] = hi
        sre_ref[...] = hr
        sim_ref[...] = hi

        run_pass(bu_cur, True, true_init, lambda *_: None, b_project_slice)

    @pl.when(g_step != 0)
    def _():
        for m in range(n_cblk):
            c_project_block(m)

    @pl.when(jnp.logical_and(g_step >= 1, pl.program_id(0) >= 0))
    def _():
        run_pass(bu_next, False, zero_init, keep_ends, glu_slice)

        half = n_gtile // 2
        for c in range(half):
            xp[cur, :, lanes(c)] = xp[cur, :, lanes(c)] + gl[c] * jax.nn.sigmoid(gl[half + c])

        def scatter(k, _):
            r0 = pl.multiple_of(k * SUBLANES, SUBLANES)
            for c in range(n_slab):
                slab[c, pl.ds(k, SUBLANES, stride=pitch), :] = xp[cur, pl.ds(r0, SUBLANES), c * LANES:(c + 1) * LANES]
            return 0

        lax.fori_loop(0, n_steps, scatter, 0, unroll=4)
        for j in range(SUBLANES):
            for c in range(n_slab):
                out_ref[j * n_steps:(j + 1) * n_steps, c * LANES:(c + 1) * LANES] = slab[c, j * pitch:j * pitch + n_steps, :]


def _ssm_sample_kernel(x_ref, h0re_ref, h0im_ref, gmix_ref, are_ref, aim_ref, bw_ref, cre_ref,
                       ncim_ref, dskip_ref, wglu_ref, bglu_ref,
                       out_ref, sre_ref, sim_ref, bure, buim, *, n_seq, n_steps):
    x = x_ref[...]
    u = _rms(x, gmix_ref[...])
    _b_project(u.astype(BF16), bw_ref, bure, buim)

    def init(r0, cs):
        return h0re_ref[pl.ds(r0, SUBLANES), cs], h0im_ref[pl.ds(r0, SUBLANES), cs]

    def final(r0, cs, hr, hi):
        sre_ref[pl.ds(r0, SUBLANES), cs] = hr
        sim_ref[pl.ds(r0, SUBLANES), cs] = hi

    _scan(bure, buim, are_ref, aim_ref, n_seq, n_steps, init, final, store=True)
    out_ref[...] = _c_project_glu(x, u, bure, buim, cre_ref, ncim_ref, dskip_ref, wglu_ref, bglu_ref)


def _ssm_params(lam_re, lam_im, log_dt, b_re, b_im, c_re, c_im, n_pow):
    g, p = lam_re.shape
    lr = lam_re.astype(F32)
    li = lam_im.astype(F32)
    dt = jnp.exp(log_dt.astype(F32))[:, None]
    mag = jnp.exp(lr * dt)
    ang = li * dt
    ab_re = mag * jnp.cos(ang)
    ab_im = mag * jnp.sin(ang)
    den = lr * lr + li * li
    nr = ab_re - 1.0
    f_re = (nr * lr + ab_im * li) / den
    f_im = (ab_im * lr - nr * li) / den
    br = b_re.astype(F32)
    bi = b_im.astype(F32)
    bb_re = f_re[..., None] * br - f_im[..., None] * bi
    bb_im = f_re[..., None] * bi + f_im[..., None] * br

    c = bb_re.shape[2]
    gk = MXU_DIM // c

    def b_tiles(bb):
        rows_kt = jnp.transpose(bb, (0, 2, 1)).reshape(g // gk, gk * c, p)
        per_tile = MXU_DIM // p
        wide = jnp.tile(rows_kt, (1, 1, per_tile))[:, None]
        row_group = jnp.arange(gk * c)[:, None] // c
        col_group = jnp.arange(MXU_DIM)[None, :] // p
        nt = jnp.arange(gk // per_tile)[:, None, None]
        own = (row_group[None] == nt * per_tile + col_group[None]).astype(F32)
        return (wide * own[None]).reshape(g * p // MXU_DIM, gk * c, MXU_DIM).astype(BF16)

    gc = LANES // c
    eye_c = jnp.eye(gc, dtype=F32)

    def c_blocks(cc):
        ct = jnp.transpose(cc.astype(F32), (0, 2, 1)).reshape(g // gc, gc, p, c)
        return jnp.einsum('tgpc,gh->tgphc', ct, eye_c).reshape(g // gc, gc * p, gc * c).astype(BF16)

    pw_re = ab_re.reshape(1, g * p)
    pw_im = ab_im.reshape(1, g * p)
    for _ in range(n_pow.bit_length() - 1):
        pw_re, pw_im = pw_re * pw_re - pw_im * pw_im, 2.0 * pw_re * pw_im
    return dict(a_re=ab_re.reshape(1, g * p), a_im=ab_im.reshape(1, g * p),
                ap_re=pw_re, ap_im=pw_im,
                b_tiles=jnp.stack([b_tiles(bb_re), b_tiles(bb_im)]),
                c_re=c_blocks(c_re), nc_im=c_blocks(-c_im))


def _ssm_prompt(x, sp, gmix, dskip, wglu, bglu, n_steps):
    n, l, d = x.shape
    nst = sp['a_re'].shape[1]
    rows = SUBLANES * n_steps
    pitch = n_steps + SUBLANES
    tiles = l // rows
    n_piece = nst // MXU_DIM
    n_gtile = wglu.shape[0]
    consts = [gmix, sp['a_re'], sp['a_im'], sp['ap_re'], sp['ap_im'], sp['b_tiles'], sp['c_re'], sp['nc_im'],
              dskip.reshape(d // LANES, 1, LANES), wglu, bglu.reshape(n_gtile, 1, MXU_DIM)]
    out, sre, sim = pl.pallas_call(
        functools.partial(_ssm_prompt_kernel, n_steps=n_steps, pitch=pitch),
        grid=(n, tiles + 1),
        in_specs=[pl.BlockSpec((None, rows, d), lambda i, g: (i, jnp.minimum(g, tiles - 1), 0))]
        + [_const_spec(c.shape) for c in consts],
        out_specs=[pl.BlockSpec((None, rows, d), lambda i, g: (i, jnp.maximum(g - 1, 0), 0)),
                   pl.BlockSpec((None, 1, nst), lambda i, g: (i, 0, 0)),
                   pl.BlockSpec((None, 1, nst), lambda i, g: (i, 0, 0))],
        out_shape=[jax.ShapeDtypeStruct((n, l, d), F32),
                   jax.ShapeDtypeStruct((n, 1, nst), F32),
                   jax.ShapeDtypeStruct((n, 1, nst), F32)],
        scratch_shapes=[pltpu.VMEM((d // LANES, SUBLANES * pitch, LANES), F32),
                        pltpu.VMEM((2, rows, d), F32),
                        pltpu.VMEM((2, rows, d), BF16),
                        pltpu.VMEM((2, d // LANES, rows, LANES), F32),
                        pltpu.VMEM((2, n_piece, rows, MXU_DIM), F32),
                        pltpu.VMEM((2, n_piece, rows, MXU_DIM), F32),
                        pltpu.VMEM((d // LANES, rows, LANES), BF16),
                        pltpu.VMEM((n_gtile, rows, MXU_DIM), F32),
                        pltpu.VMEM((2, SUBLANES, nst), F32),
                        pltpu.VMEM((2, SUBLANES, nst), F32),
                        pltpu.VMEM((2, 1, nst), F32)],
        compiler_params=pltpu.CompilerParams(dimension_semantics=("arbitrary", "arbitrary"),
                                             vmem_limit_bytes=VMEM_LIMIT),
        name="ssm_prompt",
    )(x, *consts)
    return out, sre[:, 0], sim[:, 0]


def _ssm_sample(x_tm, h0re, h0im, sp, gmix, dskip, wglu, bglu, n_seq, n_steps):
    rows, d = x_tm.shape
    nst = sp['a_re'].shape[1]
    args = [x_tm, h0re, h0im, gmix, sp['a_re'], sp['a_im'], sp['b_tiles'], sp['c_re'], sp['nc_im'],
            dskip, wglu, bglu]
    return pl.pallas_call(
        functools.partial(_ssm_sample_kernel, n_seq=n_seq, n_steps=n_steps),
        grid=(1,),
        in_specs=[_const_spec(a.shape) for a in args],
        out_specs=[pl.BlockSpec((rows, d), lambda i: (0, 0)), pl.BlockSpec((n_seq, nst), lambda i: (0, 0)),
                   pl.BlockSpec((n_seq, nst), lambda i: (0, 0))],
        out_shape=[jax.ShapeDtypeStruct((rows, d), F32),
                   jax.ShapeDtypeStruct((n_seq, nst), F32),
                   jax.ShapeDtypeStruct((n_seq, nst), F32)],
        scratch_shapes=[pltpu.VMEM((rows, nst), F32), pltpu.VMEM((rows, nst), F32)],
        compiler_params=pltpu.CompilerParams(dimension_semantics=("arbitrary",),
                                             vmem_limit_bytes=VMEM_LIMIT),
        name="ssm_sample",
    )(*args)


def _ffn_kernel(*refs, rows, head_rows, shift, has_attn):
    if has_attn:
        h_ref, o_ref, wo_ref = refs[:3]
        refs = refs[3:]
    else:
        h_ref = refs[0]
        refs = refs[1:]
    (p_ref, head_ref, gffn_ref, wup_ref, cw_ref, cb_ref, wdown_ref, gple_ref, wgate_ref, win_ref,
     out_ref, tail_ref, acc, xn_ref, ext_a, ext_b) = refs
    ff = wdown_ref.shape[0]
    n_chunk = ff // FF_CHUNK
    up0 = head_rows
    up1 = head_rows + rows

    @pl.when(pl.program_id(1) == 0)
    def _():
        tail_ref[...] = head_ref[...]

    x = h_ref[...]
    if has_attn:
        x = x + _dot(o_ref[...], wo_ref[...])
    xn_ref[...] = _rms(x, gffn_ref[...]).astype(BF16)
    acc[...] = jnp.zeros_like(acc)

    n_slab = ext_a.shape[1]

    def slab_cols(c, half, s):
        c0 = half * ff + c * FF_CHUNK + s * LANES
        return slice(c0, c0 + LANES)

    def up_project(c, ext):
        for half in range(2):
            c0 = half * ff + c * FF_CHUNK
            up = _dot(xn_ref[...], wup_ref[:, c0:c0 + FF_CHUNK])
            for s in range(n_slab):
                ext[half, s, 0:up0, :] = tail_ref[:, slab_cols(c, half, s)]
                ext[half, s, up0:up1, :] = up[:, s * LANES:(s + 1) * LANES]

    def conv(ext, half, c):
        parts = []
        for s in range(n_slab):
            ls = slab_cols(c, half, s)
            t2 = ext[half, s, up0 - 2 * shift:up1 - 2 * shift, :]
            t1 = ext[half, s, up0 - shift:up1 - shift, :]
            tail_ref[:, ls] = ext[half, s, rows:up1, :]
            r = cb_ref[:, ls] + t2 * cw_ref[0:1, ls]
            r = r + t1 * cw_ref[1:2, ls]
            parts.append(r + ext[half, s, up0:up1, :] * cw_ref[2:3, ls])
        return jnp.concatenate(parts, axis=1)

    def down_project(c, ext):
        cg = conv(ext, 0, c)
        cv = conv(ext, 1, c)
        act = (cg * jax.nn.sigmoid(cg) * cv).astype(BF16)
        acc[...] += _dot(act, wdown_ref[c * FF_CHUNK:(c + 1) * FF_CHUNK, :])

    bufs = (ext_a, ext_b)
    up_project(0, bufs[0])
    for c in range(n_chunk):
        if c + 1 < n_chunk:
            up_project(c + 1, bufs[(c + 1) % 2])
        down_project(c, bufs[c % 2])

    h2 = x + acc[...]
    gate = jax.nn.sigmoid(_dot(_rms(h2, gple_ref[...]).astype(BF16), wgate_ref[...]))
    pe = _dot(p_ref[...].astype(BF16), win_ref[...])
    out_ref[...] = h2 + pe * gate


def _ffn_weights(g_ffn, w_up, conv_w, conv_b, w_down, g_ple, w_gate, w_in):
    depth, d, ff2 = w_up.shape
    return dict(
        g_ffn=g_ffn.reshape(depth, 1, d),
        w_up=w_up.astype(BF16),
        conv_w=conv_w,
        conv_b=conv_b.reshape(depth, 1, ff2),
        w_down=w_down.astype(BF16),
        g_ple=g_ple.reshape(depth, 1, d),
        w_gate=w_gate.astype(BF16),
        w_in=w_in.astype(BF16),
    )


def _layer_spec(shape, layer):
    nd = len(shape)
    return pl.BlockSpec((None,) + tuple(shape[1:]), lambda *_: (layer,) + (0,) * (nd - 1),
                        pipeline_mode=pl.Buffered(1))


def _ffn(h, p, layer, head, head_off, fw, rows, n_seq_tiles, shift, attn=None):
    total, d = h.shape
    tiles = total // rows // n_seq_tiles
    head_rows, ff2 = head.shape[1], head.shape[2]
    row_map = lambda i, t: (i * tiles + t, 0)
    p_map = lambda i, t: (layer * (total // rows) + i * tiles + t, 0)
    consts = [fw['g_ffn'], fw['w_up'], fw['conv_w'], fw['conv_b'], fw['w_down'], fw['g_ple'], fw['w_gate'],
              fw['w_in']]
    args = [h]
    in_specs = [pl.BlockSpec((rows, d), row_map)]
    if attn is not None:
        o, wo = attn
        args += [o, wo]
        in_specs += [pl.BlockSpec((rows, o.shape[1]), row_map), _const_spec(wo.shape)]
    args += [p, head] + consts
    in_specs += [pl.BlockSpec((rows, p.shape[1]), p_map),
                 pl.BlockSpec((None, head_rows, ff2), lambda i, t: (head_off + i, 0, 0))]
    in_specs += [_layer_spec(c.shape, layer) for c in consts]
    out, tail = pl.pallas_call(
        functools.partial(_ffn_kernel, rows=rows, head_rows=head_rows, shift=shift, has_attn=attn is not None),
        grid=(n_seq_tiles, tiles),
        in_specs=in_specs,
        out_specs=[pl.BlockSpec((rows, d), row_map),
                   pl.BlockSpec((None, head_rows, ff2), lambda i, t: (i, 0, 0))],
        out_shape=[jax.ShapeDtypeStruct((total, d), F32),
                   jax.ShapeDtypeStruct((n_seq_tiles, head_rows, ff2), F32)],
        scratch_shapes=[pltpu.VMEM((rows, d), F32),
                        pltpu.VMEM((rows, d), BF16),
                        pltpu.VMEM((2, FF_CHUNK // LANES, head_rows + rows, LANES), F32),
                        pltpu.VMEM((2, FF_CHUNK // LANES, head_rows + rows, LANES), F32)],
        compiler_params=pltpu.CompilerParams(dimension_semantics=("arbitrary", "arbitrary"),
                                             vmem_limit_bytes=VMEM_LIMIT),
        name="ffn_ple",
    )(*args)
    return out, tail


def _pair_spread(x, fill):
    assert x.shape[1] == 2 * LANES and HEAD_DIM * 2 == LANES
    r = pltpu.roll(x, HEAD_DIM, 1)
    low = lax.broadcasted_iota(jnp.int32, (x.shape[0], LANES), 1) < HEAD_DIM
    x01, x23 = x[:, :LANES], x[:, LANES:]
    r30, r12 = r[:, :LANES], r[:, LANES:]
    first = lambda t: jnp.where(low, t, fill)
    second = lambda t: jnp.where(low, fill, t)
    return jnp.concatenate([first(x01), second(r30), first(r12), second(x01),
                            first(x23), second(r12), first(r30), second(x23)], axis=1)


def _qkv_kernel(*refs, paired):
    (h_ref, gkv_ref, gmix_ref, wk_ref, wv_ref, wq_ref, knorm_ref, qnorm_ref, kpool_ref, kspread_ref,
     qpool_ref, qspread_ref, q_ref, k_ref, v_ref) = refs[:15]
    x = h_ref[...]
    s = _rms(x, gkv_ref[...]).astype(BF16)
    k = _head_rms(_dot(s, wk_ref[...]), kpool_ref[...], kspread_ref[...], knorm_ref[...])
    v = _dot(s, wv_ref[...])
    xn = _rms(x, gmix_ref[...]).astype(BF16)
    q = _head_rms(_dot(xn, wq_ref[...]), qpool_ref[...], qspread_ref[...], qnorm_ref[...])
    if paired:
        kx_ref, vx_ref = refs[15:]
        kx_ref[...] = _pair_spread(k, 0.0).astype(BF16)
        vx_ref[...] = _pair_spread(v, 1.0).astype(BF16)
    k_ref[...] = k
    v_ref[...] = v
    q_ref[...] = (q * (HEAD_DIM ** -0.5)).astype(q_ref.dtype)


def _head_pool_matrices(n):
    member = (jnp.arange(n)[:, None] // HEAD_DIM) == jnp.arange(LANES)[None, :]
    return member.astype(BF16) * (1.0 / HEAD_DIM), member.T.astype(BF16)


def _qkv(h, qw, rows, paired):
    total, d = h.shape
    hk = qw['w_k'].shape[1]
    hq = qw['w_q'].shape[1]
    consts = [qw['g_kv'], qw['g_mix'], qw['w_k'], qw['w_v'], qw['w_q'], qw['k_norm'], qw['q_norm'],
              *qw['k_pool'], *qw['q_pool']]
    row_map = lambda i: (i, 0)
    out_specs = [pl.BlockSpec((rows, hq), row_map), pl.BlockSpec((rows, hk), row_map),
                 pl.BlockSpec((rows, hk), row_map)]
    out_shape = [jax.ShapeDtypeStruct((total, hq), BF16),
                 jax.ShapeDtypeStruct((total, hk), F32),
                 jax.ShapeDtypeStruct((total, hk), F32)]
    if paired:
        out_specs += [pl.BlockSpec((rows, 4 * hk), row_map)] * 2
        out_shape += [jax.ShapeDtypeStruct((total, 4 * hk), BF16)] * 2
    return pl.pallas_call(
        functools.partial(_qkv_kernel, paired=paired),
        grid=(total // rows,),
        in_specs=[pl.BlockSpec((rows, d), row_map)] + [_const_spec(c.shape) for c in consts],
        out_specs=out_specs,
        out_shape=out_shape,
        compiler_params=pltpu.CompilerParams(dimension_semantics=("arbitrary",),
                                             vmem_limit_bytes=VMEM_LIMIT),
        name="qkv",
    )(h, *consts)


def _t5_bucket(dist):
    max_exact = NUM_BUCKETS // 2
    df = jnp.maximum(dist, 1).astype(F32)
    large = max_exact + (jnp.log(df / max_exact) / math.log(MAX_DISTANCE / max_exact)
                         * (NUM_BUCKETS - max_exact)).astype(jnp.int32)
    return jnp.where(dist < max_exact, dist, jnp.minimum(large, NUM_BUCKETS - 1))


def _bias_table(rel_bias, lb, qb, lk):
    dist = lb + jnp.arange(qb)[:, None] - jnp.arange(lk)[None, :]
    per_dist = rel_bias[_t5_bucket(jnp.arange(WINDOW + 1))].astype(F32)
    onehot = (jnp.clip(dist, 0, WINDOW)[..., None] == jnp.arange(WINDOW + 1)).astype(F32)
    bias = jnp.einsum('qkd,dh->hqk', onehot, per_dist, precision=lax.Precision.HIGHEST)
    ok = (dist >= 0) & (dist <= WINDOW)
    return jnp.where(ok[None], bias, -jnp.inf)


def _softmax_pv(s, sink, v):
    m = jnp.maximum(jnp.max(s, axis=-1, keepdims=True), sink)
    pr = jnp.exp(s - m)
    den = jnp.sum(pr, axis=-1, keepdims=True) + jnp.exp(sink - m)
    return _dot(pr.astype(BF16), v) / den


def _attn_prompt_kernel(sink_ref, q_ref, kx_ref, vx_ref, bias_ref, o_ref, kx_prev, vx_prev):
    qb = kx_prev.shape[0]
    n_blocks = q_ref.shape[0] // qb
    n_heads = bias_ref.shape[1]
    rep = n_heads // N_KV_HEADS
    grp = 4 * HEAD_DIM
    step = pl.program_id(1)

    @pl.when(step == 0)
    def _():
        kx_prev[...] = jnp.zeros_like(kx_prev)
        vx_prev[...] = jnp.zeros_like(vx_prev)

    first_table = jnp.minimum(step, 1)
    low = lax.broadcasted_iota(jnp.int32, (qb, LANES), 1) < HEAD_DIM
    for g in range(N_KV_HEADS):
        gs = slice(g * grp, (g + 1) * grp)
        kx_all = jnp.concatenate([kx_prev[:, gs], kx_ref[:, gs]], axis=0)
        vx_all = jnp.concatenate([vx_prev[:, gs], vx_ref[:, gs]], axis=0)
        for blk in range(n_blocks):
            rows = slice(blk * qb, (blk + 1) * qb)
            kx = kx_all[blk * qb:(blk + 2) * qb]
            vx = vx_all[blk * qb:(blk + 2) * qb]
            table = first_table if blk == 0 else 1
            for pair in range(rep // 2):
                h0 = g * rep + 2 * pair
                ls = slice(h0 * HEAD_DIM, h0 * HEAD_DIM + LANES)
                qp = q_ref[rows, ls]
                res = []
                for h, kh in ((h0, kx[:, :LANES]), (h0 + 1, kx[:, LANES:])):
                    s = _dot_nt(qp, kh) + bias_ref[table, h]
                    m = jnp.maximum(jnp.max(s, axis=-1, keepdims=True), sink_ref[h])
                    res.append((_dot(jnp.exp(s - m).astype(BF16), vx), jnp.exp(sink_ref[h] - m)))
                (ra, ea), (rb, eb) = res
                num = jnp.where(low, ra[:, :LANES], rb[:, LANES:])
                den = jnp.where(low, ra[:, LANES:], rb[:, :LANES]) + jnp.where(low, ea, eb)
                o_ref[rows, ls] = (num / den).astype(o_ref.dtype)
    last = slice((n_blocks - 1) * qb, n_blocks * qb)
    kx_prev[...] = kx_ref[last, :]
    vx_prev[...] = vx_ref[last, :]


def _attn_prompt(q, kx, vx, bias, sinks, n, l):
    hq = q.shape[1]
    wx = kx.shape[1]
    qb = WINDOW
    rows = 2 * qb if l % (2 * qb) == 0 else qb
    steps = l // rows
    cur = lambda i, b, *_: (i * steps + b, 0)
    return pl.pallas_call(
        _attn_prompt_kernel,
        grid_spec=pltpu.PrefetchScalarGridSpec(
            num_scalar_prefetch=1,
            grid=(n, steps),
            in_specs=[pl.BlockSpec((rows, hq), cur), pl.BlockSpec((rows, wx), cur), pl.BlockSpec((rows, wx), cur),
                      pl.BlockSpec(bias.shape, lambda i, b, *_: (0, 0, 0, 0), pipeline_mode=pl.Buffered(1))],
            out_specs=pl.BlockSpec((rows, hq), cur),
            scratch_shapes=[pltpu.VMEM((qb, wx), BF16), pltpu.VMEM((qb, wx), BF16)]),
        out_shape=jax.ShapeDtypeStruct((n * l, hq), BF16),
        compiler_params=pltpu.CompilerParams(dimension_semantics=("arbitrary", "arbitrary"),
                                             vmem_limit_bytes=VMEM_LIMIT),
        name="attn_prompt",
    )(sinks, q, kx, vx, bias)


def _attn_sample_kernel(sink_ref, q_ref, kc_ref, vc_ref, kn_ref, vn_ref, bias_ref,
                        o_ref, kwin_ref, vwin_ref, *, n_new):
    lb = kc_ref.shape[1]
    new_rows = kn_ref.shape[1]
    rows = q_ref.shape[1]
    hk = kc_ref.shape[2]
    grp = rows // N_KV_HEADS
    zero_rows = jnp.zeros((bias_ref.shape[1] - lb - new_rows, hk), BF16)
    lane_head = lax.broadcasted_iota(jnp.int32, (grp, hk), 1) // HEAD_DIM
    samples = range(q_ref.shape[0])
    scores = []
    for s_i in samples:
        kk = jnp.concatenate([kc_ref[s_i].astype(BF16), kn_ref[s_i].astype(BF16), zero_rows], axis=0)
        scores.append(_dot_nt(q_ref[s_i], kk) + bias_ref[...])
    probs = []
    for s in scores:
        m = jnp.maximum(jnp.max(s, axis=-1, keepdims=True), sink_ref[...])
        pr = jnp.exp(s - m)
        probs.append((pr.astype(BF16), jnp.sum(pr, axis=-1, keepdims=True) + jnp.exp(sink_ref[...] - m)))
    for s_i, (pr, den) in zip(samples, probs):
        vv = jnp.concatenate([vc_ref[s_i].astype(BF16), vn_ref[s_i].astype(BF16), zero_rows], axis=0)
        pv = _dot(pr, vv) / den
        o = jnp.zeros((grp, hk), F32)
        for g in range(N_KV_HEADS):
            o = o + jnp.where(lane_head == g, pv[g * grp:(g + 1) * grp, :], 0.0)
        o_ref[s_i] = o
    for s_i in samples:
        kwin_ref[s_i, 0:lb - n_new, :] = kc_ref[s_i, n_new:lb, :]
        kwin_ref[s_i, lb - n_new:lb, :] = kn_ref[s_i, 0:n_new, :]
        vwin_ref[s_i, 0:lb - n_new, :] = vc_ref[s_i, n_new:lb, :]
        vwin_ref[s_i, lb - n_new:lb, :] = vn_ref[s_i, 0:n_new, :]


def _attn_sample(q_blk, kc, vc, kn, vn, bias, sink, n_new, block):
    ns, rows, hk = q_blk.shape
    lb = kc.shape[1]
    lkp = bias.shape[1]
    grp = rows // N_KV_HEADS
    per_s = lambda i: (i, 0, 0)
    return pl.pallas_call(
        functools.partial(_attn_sample_kernel, n_new=n_new),
        grid=(ns // block,),
        in_specs=[_const_spec(sink.shape),
                  pl.BlockSpec((block, rows, hk), per_s),
                  pl.BlockSpec((block, lb, hk), per_s), pl.BlockSpec((block, lb, hk), per_s),
                  pl.BlockSpec((block,) + kn.shape[1:], per_s), pl.BlockSpec((block,) + vn.shape[1:], per_s),
                  _const_spec(bias.shape)],
        out_specs=[pl.BlockSpec((block, grp, hk), per_s),
                   pl.BlockSpec((block, lb, hk), per_s), pl.BlockSpec((block, lb, hk), per_s)],
        out_shape=[jax.ShapeDtypeStruct((ns, grp, hk), F32),
                   jax.ShapeDtypeStruct((ns, lb, hk), F32),
                   jax.ShapeDtypeStruct((ns, lb, hk), F32)],
        compiler_params=pltpu.CompilerParams(dimension_semantics=("arbitrary",),
                                             vmem_limit_bytes=VMEM_LIMIT),
        name="attn_sample",
    )(sink, q_blk, kc, vc, kn, vn, bias)


def kernel(x_prompt, x_sample, state_ssm_re, state_ssm_im, state_ffn_conv, cache_k_win, cache_v_win, p_prompt, p_sample, g_mix, g_ffn, g_ple, ssm_lam_re, ssm_lam_im, ssm_log_dt, ssm_b_re, ssm_b_im, ssm_c_re, ssm_c_im, ssm_d, w_glu, b_glu, g_kv, w_k, w_v, k_norm, w_q, q_norm, sinks, w_o, rel_bias, w_up, conv_w, conv_b, w_down, w_ple_in, w_ple_gate):
    n, l, d = x_prompt.shape
    ns, ls, _ = x_sample.shape
    n_groups, n_state = ssm_lam_re.shape[1:]
    nst = n_groups * n_state
    ff2 = w_up.shape[2]
    hk = w_k.shape[1]
    hq = w_q.shape[2]
    n_heads = hq // HEAD_DIM
    rep = n_heads // N_KV_HEADS
    lb = cache_k_win.shape[1]

    ssm_steps = min(32, l // SUBLANES)
    ffn_rows = min(512, l)
    head_rows_p = SUBLANES
    tm = lambda a: jnp.swapaxes(a, 0, 1)

    sp = _ssm_params(ssm_lam_re[0], ssm_lam_im[0], ssm_log_dt[0], ssm_b_re[0], ssm_b_im[0], ssm_c_re[0],
                     ssm_c_im[0], ssm_steps)
    gmix0 = g_mix[0].reshape(1, d)
    dskip = ssm_d[0].reshape(1, d)
    wglu = jnp.transpose(w_glu[0].astype(BF16).reshape(d, 2 * d // MXU_DIM, MXU_DIM), (1, 0, 2))
    bglu = b_glu[0].reshape(1, 2 * d)
    hp, sre_p, sim_p = _ssm_prompt(x_prompt, sp, gmix0, dskip, wglu, bglu, ssm_steps)
    hp = hp.reshape(n * l, d)
    xs_tm = tm(x_sample).reshape(ls * ns, d)
    hs, sre_s, sim_s = _ssm_sample(xs_tm, state_ssm_re[0].reshape(ns, nst), state_ssm_im[0].reshape(ns, nst),
                                   sp, gmix0, dskip, wglu, bglu, ns, ls)

    pp = p_prompt.reshape(p_prompt.shape[0] * n * l, -1)
    ps = jnp.swapaxes(p_sample, 1, 2).reshape(p_sample.shape[0] * ls * ns, -1)
    zero_head = jnp.zeros((n, head_rows_p, ff2), F32)
    conv_s_tm = jnp.swapaxes(state_ffn_conv, 1, 2).reshape(state_ffn_conv.shape[0], (CONV_WIDTH - 1) * ns, ff2)

    fw = _ffn_weights(g_ffn, w_up, conv_w, conv_b, w_down, g_ple, w_ple_gate, w_ple_in)

    def ffn_layer(i, hp, hs, attn_p=None, attn_s=None):
        hp, tail_p = _ffn(hp, pp, i, zero_head, 0, fw, ffn_rows, n, 1, attn_p)
        hs, tail_s = _ffn(hs, ps, i, conv_s_tm, i, fw, ls * ns, 1, ns, attn_s)
        conv_p = tail_p[:, head_rows_p - (CONV_WIDTH - 1):, :]
        return hp, hs, conv_p, tail_s.reshape(CONV_WIDTH - 1, ns, ff2)

    hp, hs, conv_p0, conv_s0 = ffn_layer(0, hp, hs)

    qw = dict(g_kv=g_kv.reshape(1, d), g_mix=g_mix[1].reshape(1, d), w_k=w_k.astype(BF16), w_v=w_v.astype(BF16),
              w_q=w_q[0].astype(BF16), k_norm=jnp.tile(k_norm, hk // HEAD_DIM).reshape(1, hk),
              q_norm=jnp.tile(q_norm[0], n_heads).reshape(1, hq),
              k_pool=_head_pool_matrices(hk), q_pool=_head_pool_matrices(hq))
    q_p, k_p, v_p, kx_p, vx_p = _qkv(hp, qw, ffn_rows, True)
    q_s, k_s, v_s = _qkv(hs, qw, ls * ns, False)

    wo = w_o[0].astype(BF16)
    bias_p = _bias_table(rel_bias, WINDOW, WINDOW, 2 * WINDOW)
    bias_first = jnp.where(jnp.arange(2 * WINDOW) >= WINDOW, bias_p, -jnp.inf)
    o_p = _attn_prompt(q_p, kx_p, vx_p, jnp.stack([bias_first, bias_p]), sinks[0].astype(F32), n, l)

    lkp = 2 * WINDOW
    new_rows = 16
    pad_new = lambda a: jnp.pad(tm(a.reshape(ls, ns, hk)), ((0, 0), (0, new_rows - ls), (0, 0)))
    kc = cache_k_win.reshape(ns, lb, hk)
    vc = cache_v_win.reshape(ns, lb, hk)
    bias_s = _bias_table(rel_bias, lb, ls, lb + ls)
    bias_s = jnp.pad(bias_s, ((0, 0), (0, 0), (0, lkp - lb - ls)), constant_values=-jnp.inf)
    bias_s = bias_s.reshape(n_heads * ls, lkp)
    sink_s = jnp.repeat(sinks[0].astype(F32), ls).reshape(n_heads * ls, 1)
    q5 = jnp.transpose(q_s.reshape(ls, ns, N_KV_HEADS, rep, HEAD_DIM), (1, 2, 3, 0, 4))
    q_blk = jnp.einsum('sgrtd,gh->sgrthd', q5, jnp.eye(N_KV_HEADS, dtype=q5.dtype))
    q_blk = q_blk.reshape(ns, n_heads * ls, hk)
    o4, k_win_s, v_win_s = _attn_sample(q_blk, kc, vc, pad_new(k_s), pad_new(v_s), bias_s, sink_s, ls, 8)
    o_s = jnp.transpose(o4.reshape(ns, rep, ls, N_KV_HEADS, HEAD_DIM), (2, 0, 3, 1, 4)).reshape(ls * ns, hq)
    o_s = o_s.astype(BF16)

    hp, hs, conv_p1, conv_s1 = ffn_layer(1, hp, hs, (o_p, wo), (o_s, wo))

    y_prompt = hp.reshape(n, l, d)
    y_sample = tm(hs.reshape(ls, ns, d))
    ssm_shape = (1, -1, n_groups, n_state)
    kvh_shape = (-1, lb, N_KV_HEADS, HEAD_DIM)
    k_win_p = k_p.reshape(n, l, hk)[:, l - WINDOW:].reshape(n, WINDOW, N_KV_HEADS, HEAD_DIM)
    v_win_p = v_p.reshape(n, l, hk)[:, l - WINDOW:].reshape(n, WINDOW, N_KV_HEADS, HEAD_DIM)
    return (y_prompt, y_sample,
            sre_p.reshape(ssm_shape), sim_p.reshape(ssm_shape),
            sre_s.reshape(ssm_shape), sim_s.reshape(ssm_shape),
            jnp.stack([conv_p0, conv_p1]), jnp.swapaxes(jnp.stack([conv_s0, conv_s1]), 1, 2),
            k_win_p, v_win_p, k_win_s.reshape(kvh_shape), v_win_s.reshape(kvh_shape))
```

```python
import functools
import math

import jax
import jax.numpy as jnp
from jax import lax
from jax.experimental import pallas as pl
from jax.experimental.pallas import tpu as pltpu

F32 = jnp.float32
BF16 = jnp.bfloat16

EPS = 1e-6
SSM_GROUP = 16
SSM_STATE = 64
HEAD_DIM = 64
N_KV_HEADS = 4
WINDOW = 128
NUM_BUCKETS = 32
MAX_DISTANCE = 128
CONV_WIDTH = 3

LANES = 128
SUBLANES = 8
MXU_DIM = 256
VMEM_LIMIT = 56 * 1024 * 1024

SCAN_COLS = 1024
FF_CHUNK = 256


def _dot(a, b):
    return jnp.dot(a, b, preferred_element_type=F32)


def _dot_nt(a, b):
    return lax.dot_general(a, b, (((1,), (1,)), ((), ())), preferred_element_type=F32)


def _rms(x, g):
    ms = jnp.mean(x * x, axis=-1, keepdims=True)
    return x * lax.rsqrt(ms + EPS) * g


def _head_rms(x, pool, spread, g):
    ms = _dot((x * x).astype(BF16), pool)
    scale = lax.rsqrt(ms + EPS)
    hi = scale.astype(BF16)
    lo = (scale - hi.astype(F32)).astype(BF16)
    return x * (_dot(hi, spread) + _dot(lo, spread)) * g


def _const_spec(shape):
    nd = len(shape)
    return pl.BlockSpec(shape, lambda *_: (0,) * nd, pipeline_mode=pl.Buffered(1))


def _b_project(ub, bw_ref, bure, buim):
    n_piece, kw, nw = bw_ref.shape[1:]
    per_kt = n_piece * kw // ub.shape[1]
    for q in range(n_piece):
        lhs = ub[:, (q // per_kt) * kw:(q // per_kt + 1) * kw]
        bure[:, q * nw:(q + 1) * nw] = _dot(lhs, bw_ref[0, q])
        buim[:, q * nw:(q + 1) * nw] = _dot(lhs, bw_ref[1, q])


def _scan(bure, buim, are_ref, aim_ref, n_seq, n_steps, init_fn, final_fn, store):
    nst = bure.shape[1]
    for cb in range(nst // SCAN_COLS):
        cs = slice(cb * SCAN_COLS, (cb + 1) * SCAN_COLS)
        ar = jnp.broadcast_to(are_ref[:, cs], (SUBLANES, SCAN_COLS))
        ai = jnp.broadcast_to(aim_ref[:, cs], (SUBLANES, SCAN_COLS))

        def group(g, _, cs=cs, ar=ar, ai=ai):
            r0 = pl.multiple_of(g * SUBLANES, SUBLANES)

            def step(k, carry):
                hr, hi = carry
                row = pl.multiple_of(k * n_seq + r0, SUBLANES)
                br = bure[pl.ds(row, SUBLANES), cs]
                bi = buim[pl.ds(row, SUBLANES), cs]
                nr = ar * hr - ai * hi + br
                ni = ar * hi + ai * hr + bi
                if store:
                    bure[pl.ds(row, SUBLANES), cs] = nr
                    buim[pl.ds(row, SUBLANES), cs] = ni
                return nr, ni

            hr, hi = lax.fori_loop(0, n_steps, step, init_fn(r0, cs), unroll=min(n_steps, 4))
            final_fn(r0, cs, hr, hi)
            return 0

        if n_seq == SUBLANES:
            group(0, 0)
        else:
            lax.fori_loop(0, n_seq // SUBLANES, group, 0)


def _c_project_glu(x, u, bure, buim, cre_ref, ncim_ref, dskip_ref, wglu_ref, bglu_ref):
    d = x.shape[1]
    n_blk = cre_ref.shape[0]
    kw = cre_ref.shape[1]
    ys = []
    for m in range(n_blk):
        hr = bure[:, m * kw:(m + 1) * kw].astype(BF16)
        hi = buim[:, m * kw:(m + 1) * kw].astype(BF16)
        ys.append(_dot(hr, cre_ref[m]) + _dot(hi, ncim_ref[m]))
    y = jnp.concatenate(ys, axis=1) + dskip_ref[...] * u
    z = jax.nn.gelu(y).astype(BF16)
    gl = jnp.concatenate([_dot(z, wglu_ref[nt]) for nt in range(wglu_ref.shape[0])], axis=1) + bglu_ref[...]
    return x + gl[:, :d] * jax.nn.sigmoid(gl[:, d:])


def _ssm_prompt_kernel(x_ref, gmix_ref, are_ref, aim_ref, apw_re_ref, apw_im_ref, bw_ref, cre_ref, ncim_ref,
                       dskip_ref, wglu_ref, bglu_ref,
                       out_ref, sre_ref, sim_ref,
                       slab, xp, ub, us, bu_a, bu_b, zb, gl, ends, hin, car, *, n_steps, pitch):
    g_step = pl.program_id(1)
    for parity, (bu_next, bu_cur) in enumerate(((bu_a, bu_b), (bu_b, bu_a))):
        pl.when(g_step % 2 == parity)(functools.partial(
            _ssm_prompt_step, parity, g_step, x_ref, gmix_ref, are_ref, aim_ref, apw_re_ref, apw_im_ref, bw_ref,
            cre_ref, ncim_ref, dskip_ref, wglu_ref, bglu_ref, out_ref, sre_ref, sim_ref,
            slab, xp, ub, us, bu_next, bu_cur, zb, gl, ends, hin, car, n_steps, pitch))


def _ssm_prompt_step(nxt, g_step, x_ref, gmix_ref, are_ref, aim_ref, apw_re_ref, apw_im_ref, bw_ref, cre_ref,
                     ncim_ref, dskip_ref, wglu_ref, bglu_ref, out_ref, sre_ref, sim_ref,
                     slab, xp, ub, us, bu_next, bu_cur, zb, gl, ends, hin, car, n_steps, pitch):
    cur = 1 - nxt
    n_slab = slab.shape[0]
    d = xp.shape[2]
    n_piece = bu_cur.shape[1]
    blk_pieces = SCAN_COLS // MXU_DIM
    n_blk = n_piece // blk_pieces
    trips = 4
    spt = n_steps // trips
    n_cblk = cre_ref.shape[0]
    n_gtile = wglu_ref.shape[0]
    assert 2 * n_piece == n_blk * trips * 2
    assert 2 * n_gtile == n_blk * trips
    assert bw_ref.shape[2] == MXU_DIM and 2 * n_cblk == n_piece

    def lanes(q):
        return slice(q * MXU_DIM, (q + 1) * MXU_DIM)

    for j in range(SUBLANES):
        for c in range(n_slab):
            slab[c, j * pitch:j * pitch + n_steps, :] = x_ref[j * n_steps:(j + 1) * n_steps,
                                                             c * LANES:(c + 1) * LANES]

    def gather(k, _):
        r0 = pl.multiple_of(k * SUBLANES, SUBLANES)
        for c in range(n_slab):
            xp[nxt, pl.ds(r0, SUBLANES), c * LANES:(c + 1) * LANES] = slab[c, pl.ds(k, SUBLANES, stride=pitch), :]
        return 0

    lax.fori_loop(0, n_steps, gather, 0, unroll=4)
    u = _rms(xp[nxt], gmix_ref[...])
    ub[nxt] = u.astype(BF16)
    for c in range(n_slab):
        us[nxt, c] = u[:, c * LANES:(c + 1) * LANES]

    def run_pass(bu, store, init_fn, end_fn, work_fn):
        for blk in range(n_blk):
            coef = [(jnp.broadcast_to(are_ref[:, lanes(blk * blk_pieces + nt)], (SUBLANES, MXU_DIM)),
                     jnp.broadcast_to(aim_ref[:, lanes(blk * blk_pieces + nt)], (SUBLANES, MXU_DIM)))
                    for nt in range(blk_pieces)]

            def trip(i, state, blk=blk, coef=coef):
                if work_fn is not None:
                    work_fn(blk, i)
                state = list(state)
                for s in range(spt):
                    row = (i * spt + s) * SUBLANES
                    for nt in range(blk_pieces):
                        q = blk * blk_pieces + nt
                        ar, ai = coef[nt]
                        hr, hi = state[2 * nt], state[2 * nt + 1]
                        nr = ar * hr - ai * hi + bu[0, q, pl.ds(row, SUBLANES), :]
                        ni = ar * hi + ai * hr + bu[1, q, pl.ds(row, SUBLANES), :]
                        if store:
                            bu[0, q, pl.ds(row, SUBLANES), :] = nr
                            bu[1, q, pl.ds(row, SUBLANES), :] = ni
                        state[2 * nt], state[2 * nt + 1] = nr, ni
                return tuple(state)

            state = init_fn(blk)
            for i in range(trips):
                state = trip(i, state)
            end_fn(blk, state)

    def zero_init(blk):
        return tuple(jnp.zeros((SUBLANES, MXU_DIM), F32) for _ in range(2 * blk_pieces))

    def keep_ends(blk, state):
        for nt in range(blk_pieces):
            ends[0, :, lanes(blk * blk_pieces + nt)] = state[2 * nt]
            ends[1, :, lanes(blk * blk_pieces + nt)] = state[2 * nt + 1]

    def true_init(blk):
        return tuple(hin[ri, :, lanes(blk * blk_pieces + nt)] for nt in range(blk_pieces) for ri in range(2))

    tiles_per_kt = 2 * blk_pieces

    def b_project_tile(t):
        kt, j = divmod(t, tiles_per_kt)
        ri, nt = divmod(j, blk_pieces)
        q = kt * blk_pieces + nt
        bu_next[ri, q] = _dot(ub[nxt, :, lanes(kt)], bw_ref[ri, q])

    def b_project_slice(blk, i):
        b_project_tile(2 * (blk * trips + i))
        b_project_tile(2 * (blk * trips + i) + 1)

    def c_project_block(m):
        y = dskip_ref[m] * us[cur, m]
        for ri, c_ref in ((0, cre_ref), (1, ncim_ref)):
            for w in range(2):
                y = y + _dot(bu_cur[ri, 2 * m + w].astype(BF16), c_ref[m, w * MXU_DIM:(w + 1) * MXU_DIM, :])
        zb[m] = jax.nn.gelu(y).astype(BF16)

    def glu_slice(blk, i):
        t = blk * trips + i
        if t % 2 == 0:
            nt = t // 2
            z = jnp.concatenate([zb[c] for c in range(n_cblk)], axis=1)
            gl[nt] = _dot(z, wglu_ref[nt]) + bglu_ref[nt]

    def first_tile():
        car[...] = jnp.zeros_like(car)
        for blk in range(n_blk):
            lhs = ub[nxt, :, lanes(blk)]
            for ri in range(2):
                for nt in range(blk_pieces):
                    bu_next[ri, blk * blk_pieces + nt] = _dot(lhs, bw_ref[ri, blk * blk_pieces + nt])
        run_pass(bu_next, False, zero_init, keep_ends, None)

    if nxt == 0:
        pl.when(g_step == 0)(first_tile)

    @pl.when(g_step > 0)
    def _():
        hr = car[0]
        hi = car[1]
        apr = apw_re_ref[...]
        api = apw_im_ref[...]
        for j in range(SUBLANES):
            er = ends[0, j:j + 1, :]
            ei = ends[1, j:j + 1, :]
            hin[0, j:j + 1, :] = hr
            hin[1, j:j + 1, :] = hi
            hr, hi = apr * hr - api * hi + er, apr * hi + api * hr + ei
        car[0] = hr
        car[1] = hi
        sre_ref[...] = hr
        sim_ref[...] = hi

        run_pass(bu_cur, True, true_init, lambda *_: None, b_project_slice)

    @pl.when(g_step != 0)
    def _():
        for m in range(n_cblk):
            c_project_block(m)

    @pl.when(jnp.logical_and(g_step >= 1, pl.program_id(0) >= 0))
    def _():
        run_pass(bu_next, False, zero_init, keep_ends, glu_slice)

        half = n_gtile // 2
        for c in range(half):
            xp[cur, :, lanes(c)] = xp[cur, :, lanes(c)] + gl[c] * jax.nn.sigmoid(gl[half + c])

        def scatter(k, _):
            r0 = pl.multiple_of(k * SUBLANES, SUBLANES)
            for c in range(n_slab):
                slab[c, pl.ds(k, SUBLANES, stride=pitch), :] = xp[cur, pl.ds(r0, SUBLANES), c * LANES:(c + 1) * LANES]
            return 0

        lax.fori_loop(0, n_steps, scatter, 0, unroll=4)
        for j in range(SUBLANES):
            for c in range(n_slab):
                out_ref[j * n_steps:(j + 1) * n_steps, c * LANES:(c + 1) * LANES] = slab[c, j * pitch:j * pitch + n_steps, :]


def _ssm_sample_kernel(x_ref, h0re_ref, h0im_ref, gmix_ref, are_ref, aim_ref, bw_ref, cre_ref,
                       ncim_ref, dskip_ref, wglu_ref, bglu_ref,
                       out_ref, sre_ref, sim_ref, bure, buim, *, n_seq, n_steps):
    x = x_ref[...]
    u = _rms(x, gmix_ref[...])
    _b_project(u.astype(BF16), bw_ref, bure, buim)

    def init(r0, cs):
        return h0re_ref[pl.ds(r0, SUBLANES), cs], h0im_ref[pl.ds(r0, SUBLANES), cs]

    def final(r0, cs, hr, hi):
        sre_ref[pl.ds(r0, SUBLANES), cs] = hr
        sim_ref[pl.ds(r0, SUBLANES), cs] = hi

    _scan(bure, buim, are_ref, aim_ref, n_seq, n_steps, init, final, store=True)
    out_ref[...] = _c_project_glu(x, u, bure, buim, cre_ref, ncim_ref, dskip_ref, wglu_ref, bglu_ref)


def _ssm_params(lam_re, lam_im, log_dt, b_re, b_im, c_re, c_im, n_pow):
    g, p = lam_re.shape
    lr = lam_re.astype(F32)
    li = lam_im.astype(F32)
    dt = jnp.exp(log_dt.astype(F32))[:, None]
    mag = jnp.exp(lr * dt)
    ang = li * dt
    ab_re = mag * jnp.cos(ang)
    ab_im = mag * jnp.sin(ang)
    den = lr * lr + li * li
    nr = ab_re - 1.0
    f_re = (nr * lr + ab_im * li) / den
    f_im = (ab_im * lr - nr * li) / den
    br = b_re.astype(F32)
    bi = b_im.astype(F32)
    bb_re = f_re[..., None] * br - f_im[..., None] * bi
    bb_im = f_re[..., None] * bi + f_im[..., None] * br

    c = bb_re.shape[2]
    gk = MXU_DIM // c

    def b_tiles(bb):
        rows_kt = jnp.transpose(bb, (0, 2, 1)).reshape(g // gk, gk * c, p)
        per_tile = MXU_DIM // p
        wide = jnp.tile(rows_kt, (1, 1, per_tile))[:, None]
        row_group = jnp.arange(gk * c)[:, None] // c
        col_group = jnp.arange(MXU_DIM)[None, :] // p
        nt = jnp.arange(gk // per_tile)[:, None, None]
        own = (row_group[None] == nt * per_tile + col_group[None]).astype(F32)
        return (wide * own[None]).reshape(g * p // MXU_DIM, gk * c, MXU_DIM).astype(BF16)

    gc = LANES // c
    eye_c = jnp.eye(gc, dtype=F32)

    def c_blocks(cc):
        ct = jnp.transpose(cc.astype(F32), (0, 2, 1)).reshape(g // gc, gc, p, c)
        return jnp.einsum('tgpc,gh->tgphc', ct, eye_c).reshape(g // gc, gc * p, gc * c).astype(BF16)

    pw_re = ab_re.reshape(1, g * p)
    pw_im = ab_im.reshape(1, g * p)
    for _ in range(n_pow.bit_length() - 1):
        pw_re, pw_im = pw_re * pw_re - pw_im * pw_im, 2.0 * pw_re * pw_im
    return dict(a_re=ab_re.reshape(1, g * p), a_im=ab_im.reshape(1, g * p),
                ap_re=pw_re, ap_im=pw_im,
                b_tiles=jnp.stack([b_tiles(bb_re), b_tiles(bb_im)]),
                c_re=c_blocks(c_re), nc_im=c_blocks(-c_im))


def _ssm_prompt(x, sp, gmix, dskip, wglu, bglu, n_steps):
    n, l, d = x.shape
    nst = sp['a_re'].shape[1]
    rows = SUBLANES * n_steps
    pitch = n_steps + SUBLANES
    tiles = l // rows
    n_piece = nst // MXU_DIM
    n_gtile = wglu.shape[0]
    consts = [gmix, sp['a_re'], sp['a_im'], sp['ap_re'], sp['ap_im'], sp['b_tiles'], sp['c_re'], sp['nc_im'],
              dskip.reshape(d // LANES, 1, LANES), wglu, bglu.reshape(n_gtile, 1, MXU_DIM)]
    out, sre, sim = pl.pallas_call(
        functools.partial(_ssm_prompt_kernel, n_steps=n_steps, pitch=pitch),
        grid=(n, tiles + 1),
        in_specs=[pl.BlockSpec((None, rows, d), lambda i, g: (i, jnp.minimum(g, tiles - 1), 0))]
        + [_const_spec(c.shape) for c in consts],
        out_specs=[pl.BlockSpec((None, rows, d), lambda i, g: (i, jnp.maximum(g - 1, 0), 0)),
                   pl.BlockSpec((None, 1, nst), lambda i, g: (i, 0, 0)),
                   pl.BlockSpec((None, 1, nst), lambda i, g: (i, 0, 0))],
        out_shape=[jax.ShapeDtypeStruct((n, l, d), F32),
                   jax.ShapeDtypeStruct((n, 1, nst), F32),
                   jax.ShapeDtypeStruct((n, 1, nst), F32)],
        scratch_shapes=[pltpu.VMEM((d // LANES, SUBLANES * pitch, LANES), F32),
                        pltpu.VMEM((2, rows, d), F32),
                        pltpu.VMEM((2, rows, d), BF16),
                        pltpu.VMEM((2, d // LANES, rows, LANES), F32),
                        pltpu.VMEM((2, n_piece, rows, MXU_DIM), F32),
                        pltpu.VMEM((2, n_piece, rows, MXU_DIM), F32),
                        pltpu.VMEM((d // LANES, rows, LANES), BF16),
                        pltpu.VMEM((n_gtile, rows, MXU_DIM), F32),
                        pltpu.VMEM((2, SUBLANES, nst), F32),
                        pltpu.VMEM((2, SUBLANES, nst), F32),
                        pltpu.VMEM((2, 1, nst), F32)],
        compiler_params=pltpu.CompilerParams(dimension_semantics=("arbitrary", "arbitrary"),
                                             vmem_limit_bytes=VMEM_LIMIT),
        name="ssm_prompt",
    )(x, *consts)
    return out, sre[:, 0], sim[:, 0]


def _ssm_sample(x_tm, h0re, h0im, sp, gmix, dskip, wglu, bglu, n_seq, n_steps):
    rows, d = x_tm.shape
    nst = sp['a_re'].shape[1]
    args = [x_tm, h0re, h0im, gmix, sp['a_re'], sp['a_im'], sp['b_tiles'], sp['c_re'], sp['nc_im'],
            dskip, wglu, bglu]
    return pl.pallas_call(
        functools.partial(_ssm_sample_kernel, n_seq=n_seq, n_steps=n_steps),
        grid=(1,),
        in_specs=[_const_spec(a.shape) for a in args],
        out_specs=[pl.BlockSpec((rows, d), lambda i: (0, 0)), pl.BlockSpec((n_seq, nst), lambda i: (0, 0)),
                   pl.BlockSpec((n_seq, nst), lambda i: (0, 0))],
        out_shape=[jax.ShapeDtypeStruct((rows, d), F32),
                   jax.ShapeDtypeStruct((n_seq, nst), F32),
                   jax.ShapeDtypeStruct((n_seq, nst), F32)],
        scratch_shapes=[pltpu.VMEM((rows, nst), F32), pltpu.VMEM((rows, nst), F32)],
        compiler_params=pltpu.CompilerParams(dimension_semantics=("arbitrary",),
                                             vmem_limit_bytes=VMEM_LIMIT),
        name="ssm_sample",
    )(*args)


def _ffn_kernel(*refs, rows, head_rows, shift, has_attn):
    if has_attn:
        h_ref, o_ref, wo_ref = refs[:3]
        refs = refs[3:]
    else:
        h_ref = refs[0]
        refs = refs[1:]
    (p_ref, head_ref, gffn_ref, wup_ref, cw_ref, cb_ref, wdown_ref, gple_ref, wgate_ref, win_ref,
     out_ref, tail_ref, acc, xn_ref, ext_a, ext_b) = refs
    ff = wdown_ref.shape[0]
    n_chunk = ff // FF_CHUNK
    up0 = head_rows
    up1 = head_rows + rows

    @pl.when(pl.program_id(1) == 0)
    def _():
        tail_ref[...] = head_ref[...]

    x = h_ref[...]
    if has_attn:
        x = x + _dot(o_ref[...], wo_ref[...])
    xn_ref[...] = _rms(x, gffn_ref[...]).astype(BF16)
    acc[...] = jnp.zeros_like(acc)

    n_slab = ext_a.shape[1]

    def slab_cols(c, half, s):
        c0 = half * ff + c * FF_CHUNK + s * LANES
        return slice(c0, c0 + LANES)

    def up_project(c, ext):
        for half in range(2):
            c0 = half * ff + c * FF_CHUNK
            up = _dot(xn_ref[...], wup_ref[:, c0:c0 + FF_CHUNK])
            for s in range(n_slab):
                ext[half, s, 0:up0, :] = tail_ref[:, slab_cols(c, half, s)]
                ext[half, s, up0:up1, :] = up[:, s * LANES:(s + 1) * LANES]

    def conv(ext, half, c):
        parts = []
        for s in range(n_slab):
            ls = slab_cols(c, half, s)
            t2 = ext[half, s, up0 - 2 * shift:up1 - 2 * shift, :]
            t1 = ext[half, s, up0 - shift:up1 - shift, :]
            tail_ref[:, ls] = ext[half, s, rows:up1, :]
            r = cb_ref[:, ls] + t2 * cw_ref[0:1, ls]
            r = r + t1 * cw_ref[1:2, ls]
            parts.append(r + ext[half, s, up0:up1, :] * cw_ref[2:3, ls])
        return jnp.concatenate(parts, axis=1)

    def down_project(c, ext):
        cg = conv(ext, 0, c)
        cv = conv(ext, 1, c)
        act = (cg * jax.nn.sigmoid(cg) * cv).astype(BF16)
        acc[...] += _dot(act, wdown_ref[c * FF_CHUNK:(c + 1) * FF_CHUNK, :])

    bufs = (ext_a, ext_b)
    up_project(0, bufs[0])
    for c in range(n_chunk):
        if c + 1 < n_chunk:
            up_project(c + 1, bufs[(c + 1) % 2])
        down_project(c, bufs[c % 2])

    h2 = x + acc[...]
    gate = jax.nn.sigmoid(_dot(_rms(h2, gple_ref[...]).astype(BF16), wgate_ref[...]))
    pe = _dot(p_ref[...].astype(BF16), win_ref[...])
    out_ref[...] = h2 + pe * gate


def _ffn_weights(g_ffn, w_up, conv_w, conv_b, w_down, g_ple, w_gate, w_in):
    depth, d, ff2 = w_up.shape
    return dict(
        g_ffn=g_ffn.reshape(depth, 1, d),
        w_up=w_up.astype(BF16),
        conv_w=conv_w,
        conv_b=conv_b.reshape(depth, 1, ff2),
        w_down=w_down.astype(BF16),
        g_ple=g_ple.reshape(depth, 1, d),
        w_gate=w_gate.astype(BF16),
        w_in=w_in.astype(BF16),
    )


def _layer_spec(shape, layer):
    nd = len(shape)
    return pl.BlockSpec((None,) + tuple(shape[1:]), lambda *_: (layer,) + (0,) * (nd - 1),
                        pipeline_mode=pl.Buffered(1))


def _ffn(h, p, layer, head, head_off, fw, rows, n_seq_tiles, shift, attn=None):
    total, d = h.shape
    tiles = total // rows // n_seq_tiles
    head_rows, ff2 = head.shape[1], head.shape[2]
    row_map = lambda i, t: (i * tiles + t, 0)
    p_map = lambda i, t: (layer * (total // rows) + i * tiles + t, 0)
    consts = [fw['g_ffn'], fw['w_up'], fw['conv_w'], fw['conv_b'], fw['w_down'], fw['g_ple'], fw['w_gate'],
              fw['w_in']]
    args = [h]
    in_specs = [pl.BlockSpec((rows, d), row_map)]
    if attn is not None:
        o, wo = attn
        args += [o, wo]
        in_specs += [pl.BlockSpec((rows, o.shape[1]), row_map), _const_spec(wo.shape)]
    args += [p, head] + consts
    in_specs += [pl.BlockSpec((rows, p.shape[1]), p_map),
                 pl.BlockSpec((None, head_rows, ff2), lambda i, t: (head_off + i, 0, 0))]
    in_specs += [_layer_spec(c.shape, layer) for c in consts]
    out, tail = pl.pallas_call(
        functools.partial(_ffn_kernel, rows=rows, head_rows=head_rows, shift=shift, has_attn=attn is not None),
        grid=(n_seq_tiles, tiles),
        in_specs=in_specs,
        out_specs=[pl.BlockSpec((rows, d), row_map),
                   pl.BlockSpec((None, head_rows, ff2), lambda i, t: (i, 0, 0))],
        out_shape=[jax.ShapeDtypeStruct((total, d), F32),
                   jax.ShapeDtypeStruct((n_seq_tiles, head_rows, ff2), F32)],
        scratch_shapes=[pltpu.VMEM((rows, d), F32),
                        pltpu.VMEM((rows, d), BF16),
                        pltpu.VMEM((2, FF_CHUNK // LANES, head_rows + rows, LANES), F32),
                        pltpu.VMEM((2, FF_CHUNK // LANES, head_rows + rows, LANES), F32)],
        compiler_params=pltpu.CompilerParams(dimension_semantics=("arbitrary", "arbitrary"),
                                             vmem_limit_bytes=VMEM_LIMIT),
        name="ffn_ple",
    )(*args)
    return out, tail


def _pair_spread(x, fill):
    assert x.shape[1] == 2 * LANES and HEAD_DIM * 2 == LANES
    r = pltpu.roll(x, HEAD_DIM, 1)
    low = lax.broadcasted_iota(jnp.int32, (x.shape[0], LANES), 1) < HEAD_DIM
    x01, x23 = x[:, :LANES], x[:, LANES:]
    r30, r12 = r[:, :LANES], r[:, LANES:]
    first = lambda t: jnp.where(low, t, fill)
    second = lambda t: jnp.where(low, fill, t)
    return jnp.concatenate([first(x01), second(r30), first(r12), second(x01),
                            first(x23), second(r12), first(r30), second(x23)], axis=1)


def _qkv_kernel(*refs, paired):
    (h_ref, gkv_ref, gmix_ref, wk_ref, wv_ref, wq_ref, knorm_ref, qnorm_ref, kpool_ref, kspread_ref,
     qpool_ref, qspread_ref, q_ref, k_ref, v_ref) = refs[:15]
    x = h_ref[...]
    s = _rms(x, gkv_ref[...]).astype(BF16)
    k = _head_rms(_dot(s, wk_ref[...]), kpool_ref[...], kspread_ref[...], knorm_ref[...])
    v = _dot(s, wv_ref[...])
    xn = _rms(x, gmix_ref[...]).astype(BF16)
    q = _head_rms(_dot(xn, wq_ref[...]), qpool_ref[...], qspread_ref[...], qnorm_ref[...])
    if paired:
        kx_ref, vx_ref = refs[15:]
        kx_ref[...] = _pair_spread(k, 0.0).astype(BF16)
        vx_ref[...] = _pair_spread(v, 1.0).astype(BF16)
    k_ref[...] = k
    v_ref[...] = v
    q_ref[...] = (q * (HEAD_DIM ** -0.5)).astype(q_ref.dtype)


def _head_pool_matrices(n):
    member = (jnp.arange(n)[:, None] // HEAD_DIM) == jnp.arange(LANES)[None, :]
    return member.astype(BF16) * (1.0 / HEAD_DIM), member.T.astype(BF16)


def _qkv(h, qw, rows, paired):
    total, d = h.shape
    hk = qw['w_k'].shape[1]
    hq = qw['w_q'].shape[1]
    consts = [qw['g_kv'], qw['g_mix'], qw['w_k'], qw['w_v'], qw['w_q'], qw['k_norm'], qw['q_norm'],
              *qw['k_pool'], *qw['q_pool']]
    row_map = lambda i: (i, 0)
    out_specs = [pl.BlockSpec((rows, hq), row_map), pl.BlockSpec((rows, hk), row_map),
                 pl.BlockSpec((rows, hk), row_map)]
    out_shape = [jax.ShapeDtypeStruct((total, hq), BF16),
                 jax.ShapeDtypeStruct((total, hk), F32),
                 jax.ShapeDtypeStruct((total, hk), F32)]
    if paired:
        out_specs += [pl.BlockSpec((rows, 4 * hk), row_map)] * 2
        out_shape += [jax.ShapeDtypeStruct((total, 4 * hk), BF16)] * 2
    return pl.pallas_call(
        functools.partial(_qkv_kernel, paired=paired),
        grid=(total // rows,),
        in_specs=[pl.BlockSpec((rows, d), row_map)] + [_const_spec(c.shape) for c in consts],
        out_specs=out_specs,
        out_shape=out_shape,
        compiler_params=pltpu.CompilerParams(dimension_semantics=("arbitrary",),
                                             vmem_limit_bytes=VMEM_LIMIT),
        name="qkv",
    )(h, *consts)


def _t5_bucket(dist):
    max_exact = NUM_BUCKETS // 2
    df = jnp.maximum(dist, 1).astype(F32)
    large = max_exact + (jnp.log(df / max_exact) / math.log(MAX_DISTANCE / max_exact)
                         * (NUM_BUCKETS - max_exact)).astype(jnp.int32)
    return jnp.where(dist < max_exact, dist, jnp.minimum(large, NUM_BUCKETS - 1))


def _bias_table(rel_bias, lb, qb, lk):
    dist = lb + jnp.arange(qb)[:, None] - jnp.arange(lk)[None, :]
    per_dist = rel_bias[_t5_bucket(jnp.arange(WINDOW + 1))].astype(F32)
    onehot = (jnp.clip(dist, 0, WINDOW)[..., None] == jnp.arange(WINDOW + 1)).astype(F32)
    bias = jnp.einsum('qkd,dh->hqk', onehot, per_dist, precision=lax.Precision.HIGHEST)
    ok = (dist >= 0) & (dist <= WINDOW)
    return jnp.where(ok[None], bias, -jnp.inf)


def _softmax_pv(s, sink, v):
    m = jnp.maximum(jnp.max(s, axis=-1, keepdims=True), sink)
    pr = jnp.exp(s - m)
    den = jnp.sum(pr, axis=-1, keepdims=True) + jnp.exp(sink - m)
    return _dot(pr.astype(BF16), v) / den


def _attn_prompt_kernel(sink_ref, q_ref, kx_ref, vx_ref, bias_ref, o_ref, kx_prev, vx_prev):
    qb = kx_prev.shape[0]
    n_blocks = q_ref.shape[0] // qb
    n_heads = bias_ref.shape[1]
    rep = n_heads // N_KV_HEADS
    grp = 4 * HEAD_DIM
    step = pl.program_id(1)

    @pl.when(step == 0)
    def _():
        kx_prev[...] = jnp.zeros_like(kx_prev)
        vx_prev[...] = jnp.zeros_like(vx_prev)

    first_table = jnp.minimum(step, 1)
    low = lax.broadcasted_iota(jnp.int32, (qb, LANES), 1) < HEAD_DIM
    for g in range(N_KV_HEADS):
        gs = slice(g * grp, (g + 1) * grp)
        kx_all = jnp.concatenate([kx_prev[:, gs], kx_ref[:, gs]], axis=0)
        vx_all = jnp.concatenate([vx_prev[:, gs], vx_ref[:, gs]], axis=0)
        for blk in range(n_blocks):
            rows = slice(blk * qb, (blk + 1) * qb)
            kx = kx_all[blk * qb:(blk + 2) * qb]
            vx = vx_all[blk * qb:(blk + 2) * qb]
            table = first_table if blk == 0 else 1
            for pair in range(rep // 2):
                h0 = g * rep + 2 * pair
                ls = slice(h0 * HEAD_DIM, h0 * HEAD_DIM + LANES)
                qp = q_ref[rows, ls]
                res = []
                for h, kh in ((h0, kx[:, :LANES]), (h0 + 1, kx[:, LANES:])):
                    s = _dot_nt(qp, kh) + bias_ref[table, h]
                    m = jnp.maximum(jnp.max(s, axis=-1, keepdims=True), sink_ref[h])
                    res.append((_dot(jnp.exp(s - m).astype(BF16), vx), jnp.exp(sink_ref[h] - m)))
                (ra, ea), (rb, eb) = res
                num = jnp.where(low, ra[:, :LANES], rb[:, LANES:])
                den = jnp.where(low, ra[:, LANES:], rb[:, :LANES]) + jnp.where(low, ea, eb)
                o_ref[rows, ls] = (num / den).astype(o_ref.dtype)
    last = slice((n_blocks - 1) * qb, n_blocks * qb)
    kx_prev[...] = kx_ref[last, :]
    vx_prev[...] = vx_ref[last, :]


def _attn_prompt(q, kx, vx, bias, sinks, n, l):
    hq = q.shape[1]
    wx = kx.shape[1]
    qb = WINDOW
    rows = next(m * qb for m in (4, 2, 1) if l % (m * qb) == 0)
    steps = l // rows
    cur = lambda i, b, *_: (i * steps + b, 0)
    return pl.pallas_call(
        _attn_prompt_kernel,
        grid_spec=pltpu.PrefetchScalarGridSpec(
            num_scalar_prefetch=1,
            grid=(n, steps),
            in_specs=[pl.BlockSpec((rows, hq), cur), pl.BlockSpec((rows, wx), cur), pl.BlockSpec((rows, wx), cur),
                      pl.BlockSpec(bias.shape, lambda i, b, *_: (0, 0, 0, 0), pipeline_mode=pl.Buffered(1))],
            out_specs=pl.BlockSpec((rows, hq), cur),
            scratch_shapes=[pltpu.VMEM((qb, wx), BF16), pltpu.VMEM((qb, wx), BF16)]),
        out_shape=jax.ShapeDtypeStruct((n * l, hq), BF16),
        compiler_params=pltpu.CompilerParams(dimension_semantics=("arbitrary", "arbitrary"),
                                             vmem_limit_bytes=VMEM_LIMIT),
        name="attn_prompt",
    )(sinks, q, kx, vx, bias)


def _attn_sample_kernel(sink_ref, q_ref, kc_ref, vc_ref, kn_ref, vn_ref, bias_ref,
                        o_ref, kwin_ref, vwin_ref, *, n_new):
    lb = kc_ref.shape[1]
    new_rows = kn_ref.shape[1]
    rows = q_ref.shape[1]
    hk = kc_ref.shape[2]
    grp = rows // N_KV_HEADS
    zero_rows = jnp.zeros((bias_ref.shape[1] - lb - new_rows, hk), BF16)
    lane_head = lax.broadcasted_iota(jnp.int32, (grp, hk), 1) // HEAD_DIM
    samples = range(q_ref.shape[0])
    scores = []
    for s_i in samples:
        kk = jnp.concatenate([kc_ref[s_i].astype(BF16), kn_ref[s_i].astype(BF16), zero_rows], axis=0)
        scores.append(_dot_nt(q_ref[s_i], kk) + bias_ref[...])
    probs = []
    for s in scores:
        m = jnp.maximum(jnp.max(s, axis=-1, keepdims=True), sink_ref[...])
        pr = jnp.exp(s - m)
        probs.append((pr.astype(BF16), jnp.sum(pr, axis=-1, keepdims=True) + jnp.exp(sink_ref[...] - m)))
    for s_i, (pr, den) in zip(samples, probs):
        vv = jnp.concatenate([vc_ref[s_i].astype(BF16), vn_ref[s_i].astype(BF16), zero_rows], axis=0)
        pv = _dot(pr, vv) / den
        o = jnp.zeros((grp, hk), F32)
        for g in range(N_KV_HEADS):
            o = o + jnp.where(lane_head == g, pv[g * grp:(g + 1) * grp, :], 0.0)
        o_ref[s_i] = o
    for s_i in samples:
        kwin_ref[s_i, 0:lb - n_new, :] = kc_ref[s_i, n_new:lb, :]
        kwin_ref[s_i, lb - n_new:lb, :] = kn_ref[s_i, 0:n_new, :]
        vwin_ref[s_i, 0:lb - n_new, :] = vc_ref[s_i, n_new:lb, :]
        vwin_ref[s_i, lb - n_new:lb, :] = vn_ref[s_i, 0:n_new, :]


def _attn_sample(q_blk, kc, vc, kn, vn, bias, sink, n_new, block):
    ns, rows, hk = q_blk.shape
    lb = kc.shape[1]
    lkp = bias.shape[1]
    grp = rows // N_KV_HEADS
    per_s = lambda i: (i, 0, 0)
    return pl.pallas_call(
        functools.partial(_attn_sample_kernel, n_new=n_new),
        grid=(ns // block,),
        in_specs=[_const_spec(sink.shape),
                  pl.BlockSpec((block, rows, hk), per_s),
                  pl.BlockSpec((block, lb, hk), per_s), pl.BlockSpec((block, lb, hk), per_s),
                  pl.BlockSpec((block,) + kn.shape[1:], per_s), pl.BlockSpec((block,) + vn.shape[1:], per_s),
                  _const_spec(bias.shape)],
        out_specs=[pl.BlockSpec((block, grp, hk), per_s),
                   pl.BlockSpec((block, lb, hk), per_s), pl.BlockSpec((block, lb, hk), per_s)],
        out_shape=[jax.ShapeDtypeStruct((ns, grp, hk), F32),
                   jax.ShapeDtypeStruct((ns, lb, hk), F32),
                   jax.ShapeDtypeStruct((ns, lb, hk), F32)],
        compiler_params=pltpu.CompilerParams(dimension_semantics=("arbitrary",),
                                             vmem_limit_bytes=VMEM_LIMIT),
        name="attn_sample",
    )(sink, q_blk, kc, vc, kn, vn, bias)


def kernel(x_prompt, x_sample, state_ssm_re, state_ssm_im, state_ffn_conv, cache_k_win, cache_v_win, p_prompt, p_sample, g_mix, g_ffn, g_ple, ssm_lam_re, ssm_lam_im, ssm_log_dt, ssm_b_re, ssm_b_im, ssm_c_re, ssm_c_im, ssm_d, w_glu, b_glu, g_kv, w_k, w_v, k_norm, w_q, q_norm, sinks, w_o, rel_bias, w_up, conv_w, conv_b, w_down, w_ple_in, w_ple_gate):
    n, l, d = x_prompt.shape
    ns, ls, _ = x_sample.shape
    n_groups, n_state = ssm_lam_re.shape[1:]
    nst = n_groups * n_state
    ff2 = w_up.shape[2]
    hk = w_k.shape[1]
    hq = w_q.shape[2]
    n_heads = hq // HEAD_DIM
    rep = n_heads // N_KV_HEADS
    lb = cache_k_win.shape[1]

    ssm_steps = min(32, l // SUBLANES)
    ffn_rows = min(512, l)
    head_rows_p = SUBLANES
    tm = lambda a: jnp.swapaxes(a, 0, 1)

    sp = _ssm_params(ssm_lam_re[0], ssm_lam_im[0], ssm_log_dt[0], ssm_b_re[0], ssm_b_im[0], ssm_c_re[0],
                     ssm_c_im[0], ssm_steps)
    gmix0 = g_mix[0].reshape(1, d)
    dskip = ssm_d[0].reshape(1, d)
    wglu = jnp.transpose(w_glu[0].astype(BF16).reshape(d, 2 * d // MXU_DIM, MXU_DIM), (1, 0, 2))
    bglu = b_glu[0].reshape(1, 2 * d)
    hp, sre_p, sim_p = _ssm_prompt(x_prompt, sp, gmix0, dskip, wglu, bglu, ssm_steps)
    hp = hp.reshape(n * l, d)
    xs_tm = tm(x_sample).reshape(ls * ns, d)
    hs, sre_s, sim_s = _ssm_sample(xs_tm, state_ssm_re[0].reshape(ns, nst), state_ssm_im[0].reshape(ns, nst),
                                   sp, gmix0, dskip, wglu, bglu, ns, ls)

    pp = p_prompt.reshape(p_prompt.shape[0] * n * l, -1)
    ps = jnp.swapaxes(p_sample, 1, 2).reshape(p_sample.shape[0] * ls * ns, -1)
    zero_head = jnp.zeros((n, head_rows_p, ff2), F32)
    conv_s_tm = jnp.swapaxes(state_ffn_conv, 1, 2).reshape(state_ffn_conv.shape[0], (CONV_WIDTH - 1) * ns, ff2)

    fw = _ffn_weights(g_ffn, w_up, conv_w, conv_b, w_down, g_ple, w_ple_gate, w_ple_in)

    def ffn_layer(i, hp, hs, attn_p=None, attn_s=None):
        hp, tail_p = _ffn(hp, pp, i, zero_head, 0, fw, ffn_rows, n, 1, attn_p)
        hs, tail_s = _ffn(hs, ps, i, conv_s_tm, i, fw, ls * ns, 1, ns, attn_s)
        conv_p = tail_p[:, head_rows_p - (CONV_WIDTH - 1):, :]
        return hp, hs, conv_p, tail_s.reshape(CONV_WIDTH - 1, ns, ff2)

    hp, hs, conv_p0, conv_s0 = ffn_layer(0, hp, hs)

    qw = dict(g_kv=g_kv.reshape(1, d), g_mix=g_mix[1].reshape(1, d), w_k=w_k.astype(BF16), w_v=w_v.astype(BF16),
              w_q=w_q[0].astype(BF16), k_norm=jnp.tile(k_norm, hk // HEAD_DIM).reshape(1, hk),
              q_norm=jnp.tile(q_norm[0], n_heads).reshape(1, hq),
              k_pool=_head_pool_matrices(hk), q_pool=_head_pool_matrices(hq))
    qkv_rows = 2 * ffn_rows if (n * l) % (2 * ffn_rows) == 0 else ffn_rows
    q_p, k_p, v_p, kx_p, vx_p = _qkv(hp, qw, qkv_rows, True)
    q_s, k_s, v_s = _qkv(hs, qw, ls * ns, False)

    wo = w_o[0].astype(BF16)
    bias_p = _bias_table(rel_bias, WINDOW, WINDOW, 2 * WINDOW)
    bias_first = jnp.where(jnp.arange(2 * WINDOW) >= WINDOW, bias_p, -jnp.inf)
    o_p = _attn_prompt(q_p, kx_p, vx_p, jnp.stack([bias_first, bias_p]), sinks[0].astype(F32), n, l)

    lkp = 2 * WINDOW
    new_rows = 16
    pad_new = lambda a: jnp.pad(tm(a.reshape(ls, ns, hk)), ((0, 0), (0, new_rows - ls), (0, 0)))
    kc = cache_k_win.reshape(ns, lb, hk)
    vc = cache_v_win.reshape(ns, lb, hk)
    bias_s = _bias_table(rel_bias, lb, ls, lb + ls)
    bias_s = jnp.pad(bias_s, ((0, 0), (0, 0), (0, lkp - lb - ls)), constant_values=-jnp.inf)
    bias_s = bias_s.reshape(n_heads * ls, lkp)
    sink_s = jnp.repeat(sinks[0].astype(F32), ls).reshape(n_heads * ls, 1)
    q5 = jnp.transpose(q_s.reshape(ls, ns, N_KV_HEADS, rep, HEAD_DIM), (1, 2, 3, 0, 4))
    q_blk = jnp.einsum('sgrtd,gh->sgrthd', q5, jnp.eye(N_KV_HEADS, dtype=q5.dtype))
    q_blk = q_blk.reshape(ns, n_heads * ls, hk)
    o4, k_win_s, v_win_s = _attn_sample(q_blk, kc, vc, pad_new(k_s), pad_new(v_s), bias_s, sink_s, ls, 8)
    o_s = jnp.transpose(o4.reshape(ns, rep, ls, N_KV_HEADS, HEAD_DIM), (2, 0, 3, 1, 4)).reshape(ls * ns, hq)
    o_s = o_s.astype(BF16)

    hp, hs, conv_p1, conv_s1 = ffn_layer(1, hp, hs, (o_p, wo), (o_s, wo))

    y_prompt = hp.reshape(n, l, d)
    y_sample = tm(hs.reshape(ls, ns, d))
    ssm_shape = (1, -1, n_groups, n_state)
    kvh_shape = (-1, lb, N_KV_HEADS, HEAD_DIM)
    k_win_p = k_p.reshape(n, l, hk)[:, l - WINDOW:].reshape(n, WINDOW, N_KV_HEADS, HEAD_DIM)
    v_win_p = v_p.reshape(n, l, hk)[:, l - WINDOW:].reshape(n, WINDOW, N_KV_HEADS, HEAD_DIM)
    return (y_prompt, y_sample,
            sre_p.reshape(ssm_shape), sim_p.reshape(ssm_shape),
            sre_s.reshape(ssm_shape), sim_s.reshape(ssm_shape),
            jnp.stack([conv_p0, conv_p1]), jnp.swapaxes(jnp.stack([conv_s0, conv_s1]), 1, 2),
            k_win_p, v_win_p, k_win_s.reshape(kvh_shape), v_win_s.reshape(kvh_shape))
```

```python
import functools
import math

import jax
import jax.numpy as jnp
from jax import lax
from jax.experimental import pallas as pl
from jax.experimental.pallas import tpu as pltpu

F32 = jnp.float32
BF16 = jnp.bfloat16

EPS = 1e-6
SSM_GROUP = 16
SSM_STATE = 64
HEAD_DIM = 64
N_KV_HEADS = 4
WINDOW = 128
NUM_BUCKETS = 32
MAX_DISTANCE = 128
CONV_WIDTH = 3

LANES = 128
SUBLANES = 8
MXU_DIM = 256
VMEM_LIMIT = 56 * 1024 * 1024

SCAN_COLS = 1024
FF_CHUNK = 256


def _dot(a, b):
    return jnp.dot(a, b, preferred_element_type=F32)


def _dot_nt(a, b):
    return lax.dot_general(a, b, (((1,), (1,)), ((), ())), preferred_element_type=F32)


def _rms(x, g):
    ms = jnp.mean(x * x, axis=-1, keepdims=True)
    return x * lax.rsqrt(ms + EPS) * g


def _head_rms(x, pool, spread, g):
    ms = _dot((x * x).astype(BF16), pool)
    scale = lax.rsqrt(ms + EPS)
    hi = scale.astype(BF16)
    lo = (scale - hi.astype(F32)).astype(BF16)
    return x * (_dot(hi, spread) + _dot(lo, spread)) * g


def _const_spec(shape):
    nd = len(shape)
    return pl.BlockSpec(shape, lambda *_: (0,) * nd, pipeline_mode=pl.Buffered(1))


def _b_project(ub, bw_ref, bure, buim):
    n_piece, kw, nw = bw_ref.shape[1:]
    per_kt = n_piece * kw // ub.shape[1]
    for q in range(n_piece):
        lhs = ub[:, (q // per_kt) * kw:(q // per_kt + 1) * kw]
        bure[:, q * nw:(q + 1) * nw] = _dot(lhs, bw_ref[0, q])
        buim[:, q * nw:(q + 1) * nw] = _dot(lhs, bw_ref[1, q])


def _scan(bure, buim, are_ref, aim_ref, n_seq, n_steps, init_fn, final_fn, store):
    nst = bure.shape[1]
    for cb in range(nst // SCAN_COLS):
        cs = slice(cb * SCAN_COLS, (cb + 1) * SCAN_COLS)
        ar = jnp.broadcast_to(are_ref[:, cs], (SUBLANES, SCAN_COLS))
        ai = jnp.broadcast_to(aim_ref[:, cs], (SUBLANES, SCAN_COLS))

        def group(g, _, cs=cs, ar=ar, ai=ai):
            r0 = pl.multiple_of(g * SUBLANES, SUBLANES)

            def step(k, carry):
                hr, hi = carry
                row = pl.multiple_of(k * n_seq + r0, SUBLANES)
                br = bure[pl.ds(row, SUBLANES), cs]
                bi = buim[pl.ds(row, SUBLANES), cs]
                nr = ar * hr - ai * hi + br
                ni = ar * hi + ai * hr + bi
                if store:
                    bure[pl.ds(row, SUBLANES), cs] = nr
                    buim[pl.ds(row, SUBLANES), cs] = ni
                return nr, ni

            hr, hi = lax.fori_loop(0, n_steps, step, init_fn(r0, cs), unroll=min(n_steps, 4))
            final_fn(r0, cs, hr, hi)
            return 0

        lax.fori_loop(0, n_seq // SUBLANES, group, 0, unroll=4)


def _c_project_glu(x, u, bure, buim, cre_ref, ncim_ref, dskip_ref, wglu_ref, bglu_ref):
    d = x.shape[1]
    n_blk = cre_ref.shape[0]
    kw = cre_ref.shape[1]
    ys = []
    for m in range(n_blk):
        hr = bure[:, m * kw:(m + 1) * kw].astype(BF16)
        hi = buim[:, m * kw:(m + 1) * kw].astype(BF16)
        ys.append(_dot(hr, cre_ref[m]) + _dot(hi, ncim_ref[m]))
    y = jnp.concatenate(ys, axis=1) + dskip_ref[...] * u
    z = jax.nn.gelu(y).astype(BF16)
    gl = jnp.concatenate([_dot(z, wglu_ref[nt]) for nt in range(wglu_ref.shape[0])], axis=1) + bglu_ref[...]
    return x + gl[:, :d] * jax.nn.sigmoid(gl[:, d:])


def _ssm_prompt_kernel(x_ref, gmix_ref, are_ref, aim_ref, apw_re_ref, apw_im_ref, bw_ref, cre_ref, ncim_ref,
                       dskip_ref, wglu_ref, bglu_ref,
                       out_ref, sre_ref, sim_ref,
                       slab, xp, ub, us, bu_a, bu_b, zb, gl, ends, hin, car, *, n_steps, pitch):
    g_step = pl.program_id(1)
    for parity, (bu_next, bu_cur) in enumerate(((bu_a, bu_b), (bu_b, bu_a))):
        pl.when(g_step % 2 == parity)(functools.partial(
            _ssm_prompt_step, parity, g_step, x_ref, gmix_ref, are_ref, aim_ref, apw_re_ref, apw_im_ref, bw_ref,
            cre_ref, ncim_ref, dskip_ref, wglu_ref, bglu_ref, out_ref, sre_ref, sim_ref,
            slab, xp, ub, us, bu_next, bu_cur, zb, gl, ends, hin, car, n_steps, pitch))


def _ssm_prompt_step(nxt, g_step, x_ref, gmix_ref, are_ref, aim_ref, apw_re_ref, apw_im_ref, bw_ref, cre_ref,
                     ncim_ref, dskip_ref, wglu_ref, bglu_ref, out_ref, sre_ref, sim_ref,
                     slab, xp, ub, us, bu_next, bu_cur, zb, gl, ends, hin, car, n_steps, pitch):
    cur = 1 - nxt
    n_slab = slab.shape[0]
    d = xp.shape[2]
    n_piece = bu_cur.shape[1]
    blk_pieces = SCAN_COLS // MXU_DIM
    n_blk = n_piece // blk_pieces
    trips = 4
    spt = n_steps // trips
    n_cblk = cre_ref.shape[0]
    n_gtile = wglu_ref.shape[0]
    assert 2 * n_piece == n_blk * trips * 2
    assert 2 * n_gtile == n_blk * trips
    assert bw_ref.shape[2] == MXU_DIM and 2 * n_cblk == n_piece

    def lanes(q):
        return slice(q * MXU_DIM, (q + 1) * MXU_DIM)

    for j in range(SUBLANES):
        for c in range(n_slab):
            slab[c, j * pitch:j * pitch + n_steps, :] = x_ref[j * n_steps:(j + 1) * n_steps,
                                                             c * LANES:(c + 1) * LANES]

    def gather(k, _):
        r0 = pl.multiple_of(k * SUBLANES, SUBLANES)
        for c in range(n_slab):
            xp[nxt, pl.ds(r0, SUBLANES), c * LANES:(c + 1) * LANES] = slab[c, pl.ds(k, SUBLANES, stride=pitch), :]
        return 0

    lax.fori_loop(0, n_steps, gather, 0, unroll=True)
    u = _rms(xp[nxt], gmix_ref[...])
    ub[nxt] = u.astype(BF16)
    for c in range(n_slab):
        us[nxt, c] = u[:, c * LANES:(c + 1) * LANES]

    def run_pass(bu, store, init_fn, end_fn, work_fn):
        for blk in range(n_blk):
            coef = [(jnp.broadcast_to(are_ref[:, lanes(blk * blk_pieces + nt)], (SUBLANES, MXU_DIM)),
                     jnp.broadcast_to(aim_ref[:, lanes(blk * blk_pieces + nt)], (SUBLANES, MXU_DIM)))
                    for nt in range(blk_pieces)]

            def trip(i, state, blk=blk, coef=coef):
                if work_fn is not None:
                    work_fn(blk, i)
                state = list(state)
                for s in range(spt):
                    row = (i * spt + s) * SUBLANES
                    for nt in range(blk_pieces):
                        q = blk * blk_pieces + nt
                        ar, ai = coef[nt]
                        hr, hi = state[2 * nt], state[2 * nt + 1]
                        nr = ar * hr - ai * hi + bu[0, q, pl.ds(row, SUBLANES), :]
                        ni = ar * hi + ai * hr + bu[1, q, pl.ds(row, SUBLANES), :]
                        if store:
                            bu[0, q, pl.ds(row, SUBLANES), :] = nr
                            bu[1, q, pl.ds(row, SUBLANES), :] = ni
                        state[2 * nt], state[2 * nt + 1] = nr, ni
                return tuple(state)

            state = init_fn(blk)
            for i in range(trips):
                state = trip(i, state)
            end_fn(blk, state)

    def zero_init(blk):
        return tuple(jnp.zeros((SUBLANES, MXU_DIM), F32) for _ in range(2 * blk_pieces))

    def keep_ends(blk, state):
        for nt in range(blk_pieces):
            ends[0, :, lanes(blk * blk_pieces + nt)] = state[2 * nt]
            ends[1, :, lanes(blk * blk_pieces + nt)] = state[2 * nt + 1]

    def true_init(blk):
        return tuple(hin[ri, :, lanes(blk * blk_pieces + nt)] for nt in range(blk_pieces) for ri in range(2))

    tiles_per_kt = 2 * blk_pieces

    def b_project_tile(t):
        kt, j = divmod(t, tiles_per_kt)
        ri, nt = divmod(j, blk_pieces)
        q = kt * blk_pieces + nt
        bu_next[ri, q] = _dot(ub[nxt, :, lanes(kt)], bw_ref[ri, q])

    def b_project_slice(blk, i):
        b_project_tile(2 * (blk * trips + i))
        b_project_tile(2 * (blk * trips + i) + 1)

    def c_project_block(m):
        y = dskip_ref[m] * us[cur, m]
        for ri, c_ref in ((0, cre_ref), (1, ncim_ref)):
            for w in range(2):
                y = y + _dot(bu_cur[ri, 2 * m + w].astype(BF16), c_ref[m, w * MXU_DIM:(w + 1) * MXU_DIM, :])
        zb[m] = jax.nn.gelu(y).astype(BF16)

    def glu_slice(blk, i):
        t = blk * trips + i
        if t % 2 == 0:
            nt = t // 2
            z = jnp.concatenate([zb[c] for c in range(n_cblk)], axis=1)
            gl[nt] = _dot(z, wglu_ref[nt]) + bglu_ref[nt]

    def first_tile():
        car[...] = jnp.zeros_like(car)
        for blk in range(n_blk):
            lhs = ub[nxt, :, lanes(blk)]
            for ri in range(2):
                for nt in range(blk_pieces):
                    bu_next[ri, blk * blk_pieces + nt] = _dot(lhs, bw_ref[ri, blk * blk_pieces + nt])
        run_pass(bu_next, False, zero_init, keep_ends, None)

    if nxt == 0:
        pl.when(g_step == 0)(first_tile)

    @pl.when(g_step > 0)
    def _():
        hr = car[0]
        hi = car[1]
        apr = apw_re_ref[...]
        api = apw_im_ref[...]
        for j in range(SUBLANES):
            er = ends[0, j:j + 1, :]
            ei = ends[1, j:j + 1, :]
            hin[0, j:j + 1, :] = hr
            hin[1, j:j + 1, :] = hi
            hr, hi = apr * hr - api * hi + er, apr * hi + api * hr + ei
        car[0] = hr
        car[1] = hi
        sre_ref[...] = hr
        sim_ref[...] = hi

        run_pass(bu_cur, True, true_init, lambda *_: None, b_project_slice)

    @pl.when(g_step != 0)
    def _():
        for m in range(n_cblk):
            c_project_block(m)

    @pl.when(jnp.logical_and(g_step >= 1, pl.program_id(0) >= 0))
    def _():
        run_pass(bu_next, False, zero_init, keep_ends, glu_slice)

        half = n_gtile // 2
        for c in range(half):
            xp[cur, :, lanes(c)] = xp[cur, :, lanes(c)] + gl[c] * jax.nn.sigmoid(gl[half + c])

        def scatter(k, _):
            r0 = pl.multiple_of(k * SUBLANES, SUBLANES)
            for c in range(n_slab):
                slab[c, pl.ds(k, SUBLANES, stride=pitch), :] = xp[cur, pl.ds(r0, SUBLANES), c * LANES:(c + 1) * LANES]
            return 0

        lax.fori_loop(0, n_steps, scatter, 0, unroll=True)
        for j in range(SUBLANES):
            for c in range(n_slab):
                out_ref[j * n_steps:(j + 1) * n_steps, c * LANES:(c + 1) * LANES] = slab[c, j * pitch:j * pitch + n_steps, :]


def _ssm_sample_kernel(x_ref, h0re_ref, h0im_ref, gmix_ref, are_ref, aim_ref, bw_ref, cre_ref,
                       ncim_ref, dskip_ref, wglu_ref, bglu_ref,
                       out_ref, sre_ref, sim_ref, bure, buim, *, n_seq, n_steps):
    x = x_ref[...]
    u = _rms(x, gmix_ref[...])
    _b_project(u.astype(BF16), bw_ref, bure, buim)

    def init(r0, cs):
        return h0re_ref[pl.ds(r0, SUBLANES), cs], h0im_ref[pl.ds(r0, SUBLANES), cs]

    def final(r0, cs, hr, hi):
        sre_ref[pl.ds(r0, SUBLANES), cs] = hr
        sim_ref[pl.ds(r0, SUBLANES), cs] = hi

    _scan(bure, buim, are_ref, aim_ref, n_seq, n_steps, init, final, store=True)
    out_ref[...] = _c_project_glu(x, u, bure, buim, cre_ref, ncim_ref, dskip_ref, wglu_ref, bglu_ref)


def _ssm_params(lam_re, lam_im, log_dt, b_re, b_im, c_re, c_im, n_pow):
    g, p = lam_re.shape
    lr = lam_re.astype(F32)
    li = lam_im.astype(F32)
    dt = jnp.exp(log_dt.astype(F32))[:, None]
    mag = jnp.exp(lr * dt)
    ang = li * dt
    ab_re = mag * jnp.cos(ang)
    ab_im = mag * jnp.sin(ang)
    den = lr * lr + li * li
    nr = ab_re - 1.0
    f_re = (nr * lr + ab_im * li) / den
    f_im = (ab_im * lr - nr * li) / den
    br = b_re.astype(F32)
    bi = b_im.astype(F32)
    bb_re = f_re[..., None] * br - f_im[..., None] * bi
    bb_im = f_re[..., None] * bi + f_im[..., None] * br

    c = bb_re.shape[2]
    gk = MXU_DIM // c

    def b_tiles(bb):
        rows_kt = jnp.transpose(bb, (0, 2, 1)).reshape(g // gk, gk * c, p)
        per_tile = MXU_DIM // p
        wide = jnp.tile(rows_kt, (1, 1, per_tile))[:, None]
        row_group = jnp.arange(gk * c)[:, None] // c
        col_group = jnp.arange(MXU_DIM)[None, :] // p
        nt = jnp.arange(gk // per_tile)[:, None, None]
        own = (row_group[None] == nt * per_tile + col_group[None]).astype(F32)
        return (wide * own[None]).reshape(g * p // MXU_DIM, gk * c, MXU_DIM).astype(BF16)

    gc = LANES // c
    eye_c = jnp.eye(gc, dtype=F32)

    def c_blocks(cc):
        ct = jnp.transpose(cc.astype(F32), (0, 2, 1)).reshape(g // gc, gc, p, c)
        return jnp.einsum('tgpc,gh->tgphc', ct, eye_c).reshape(g // gc, gc * p, gc * c).astype(BF16)

    pw_re = ab_re.reshape(1, g * p)
    pw_im = ab_im.reshape(1, g * p)
    for _ in range(n_pow.bit_length() - 1):
        pw_re, pw_im = pw_re * pw_re - pw_im * pw_im, 2.0 * pw_re * pw_im
    return dict(a_re=ab_re.reshape(1, g * p), a_im=ab_im.reshape(1, g * p),
                ap_re=pw_re, ap_im=pw_im,
                b_tiles=jnp.stack([b_tiles(bb_re), b_tiles(bb_im)]),
                c_re=c_blocks(c_re), nc_im=c_blocks(-c_im))


def _ssm_prompt(x, sp, gmix, dskip, wglu, bglu, n_steps):
    n, l, d = x.shape
    nst = sp['a_re'].shape[1]
    rows = SUBLANES * n_steps
    pitch = n_steps + SUBLANES
    tiles = l // rows
    n_piece = nst // MXU_DIM
    n_gtile = wglu.shape[0]
    consts = [gmix, sp['a_re'], sp['a_im'], sp['ap_re'], sp['ap_im'], sp['b_tiles'], sp['c_re'], sp['nc_im'],
              dskip.reshape(d // LANES, 1, LANES), wglu, bglu.reshape(n_gtile, 1, MXU_DIM)]
    out, sre, sim = pl.pallas_call(
        functools.partial(_ssm_prompt_kernel, n_steps=n_steps, pitch=pitch),
        grid=(n, tiles + 1),
        in_specs=[pl.BlockSpec((None, rows, d), lambda i, g: (i, jnp.minimum(g, tiles - 1), 0))]
        + [_const_spec(c.shape) for c in consts],
        out_specs=[pl.BlockSpec((None, rows, d), lambda i, g: (i, jnp.maximum(g - 1, 0), 0)),
                   pl.BlockSpec((None, 1, nst), lambda i, g: (i, 0, 0)),
                   pl.BlockSpec((None, 1, nst), lambda i, g: (i, 0, 0))],
        out_shape=[jax.ShapeDtypeStruct((n, l, d), F32),
                   jax.ShapeDtypeStruct((n, 1, nst), F32),
                   jax.ShapeDtypeStruct((n, 1, nst), F32)],
        scratch_shapes=[pltpu.VMEM((d // LANES, SUBLANES * pitch, LANES), F32),
                        pltpu.VMEM((2, rows, d), F32),
                        pltpu.VMEM((2, rows, d), BF16),
                        pltpu.VMEM((2, d // LANES, rows, LANES), F32),
                        pltpu.VMEM((2, n_piece, rows, MXU_DIM), F32),
                        pltpu.VMEM((2, n_piece, rows, MXU_DIM), F32),
                        pltpu.VMEM((d // LANES, rows, LANES), BF16),
                        pltpu.VMEM((n_gtile, rows, MXU_DIM), F32),
                        pltpu.VMEM((2, SUBLANES, nst), F32),
                        pltpu.VMEM((2, SUBLANES, nst), F32),
                        pltpu.VMEM((2, 1, nst), F32)],
        compiler_params=pltpu.CompilerParams(dimension_semantics=("arbitrary", "arbitrary"),
                                             vmem_limit_bytes=VMEM_LIMIT),
        name="ssm_prompt",
    )(x, *consts)
    return out, sre[:, 0], sim[:, 0]


def _ssm_sample(x_tm, h0re, h0im, sp, gmix, dskip, wglu, bglu, n_seq, n_steps):
    rows, d = x_tm.shape
    nst = sp['a_re'].shape[1]
    args = [x_tm, h0re, h0im, gmix, sp['a_re'], sp['a_im'], sp['b_tiles'], sp['c_re'], sp['nc_im'],
            dskip, wglu, bglu]
    return pl.pallas_call(
        functools.partial(_ssm_sample_kernel, n_seq=n_seq, n_steps=n_steps),
        grid=(1,),
        in_specs=[_const_spec(a.shape) for a in args],
        out_specs=[pl.BlockSpec((rows, d), lambda i: (0, 0)), pl.BlockSpec((n_seq, nst), lambda i: (0, 0)),
                   pl.BlockSpec((n_seq, nst), lambda i: (0, 0))],
        out_shape=[jax.ShapeDtypeStruct((rows, d), F32),
                   jax.ShapeDtypeStruct((n_seq, nst), F32),
                   jax.ShapeDtypeStruct((n_seq, nst), F32)],
        scratch_shapes=[pltpu.VMEM((rows, nst), F32), pltpu.VMEM((rows, nst), F32)],
        compiler_params=pltpu.CompilerParams(dimension_semantics=("arbitrary",),
                                             vmem_limit_bytes=VMEM_LIMIT),
        name="ssm_sample",
    )(*args)


def _ffn_kernel(*refs, rows, head_rows, shift, has_attn):
    if has_attn:
        h_ref, o_ref, wo_ref = refs[:3]
        refs = refs[3:]
    else:
        h_ref = refs[0]
        refs = refs[1:]
    (p_ref, head_ref, gffn_ref, wup_ref, cw_ref, cb_ref, wdown_ref, gple_ref, wgate_ref, win_ref,
     out_ref, tail_ref, acc, xn_ref, ext_a, ext_b) = refs
    ff = wdown_ref.shape[0]
    n_chunk = ff // FF_CHUNK
    up0 = head_rows
    up1 = head_rows + rows

    @pl.when(pl.program_id(1) == 0)
    def _():
        tail_ref[...] = head_ref[...]

    x = h_ref[...]
    if has_attn:
        x = x + _dot(o_ref[...], wo_ref[...])
    xn_ref[...] = _rms(x, gffn_ref[...]).astype(BF16)
    acc[...] = jnp.zeros_like(acc)

    n_slab = ext_a.shape[1]

    def slab_cols(c, half, s):
        c0 = half * ff + c * FF_CHUNK + s * LANES
        return slice(c0, c0 + LANES)

    def up_project(c, ext):
        for half in range(2):
            c0 = half * ff + c * FF_CHUNK
            up = _dot(xn_ref[...], wup_ref[:, c0:c0 + FF_CHUNK])
            for s in range(n_slab):
                ext[half, s, 0:up0, :] = tail_ref[:, slab_cols(c, half, s)]
                ext[half, s, up0:up1, :] = up[:, s * LANES:(s + 1) * LANES]

    def conv(ext, half, c):
        parts = []
        for s in range(n_slab):
            ls = slab_cols(c, half, s)
            t2 = ext[half, s, up0 - 2 * shift:up1 - 2 * shift, :]
            t1 = ext[half, s, up0 - shift:up1 - shift, :]
            tail_ref[:, ls] = ext[half, s, rows:up1, :]
            r = cb_ref[:, ls] + t2 * cw_ref[0:1, ls]
            r = r + t1 * cw_ref[1:2, ls]
            parts.append(r + ext[half, s, up0:up1, :] * cw_ref[2:3, ls])
        return jnp.concatenate(parts, axis=1)

    def down_project(c, ext):
        cg = conv(ext, 0, c)
        cv = conv(ext, 1, c)
        act = (cg * jax.nn.sigmoid(cg) * cv).astype(BF16)
        acc[...] += _dot(act, wdown_ref[c * FF_CHUNK:(c + 1) * FF_CHUNK, :])

    bufs = (ext_a, ext_b)
    up_project(0, bufs[0])
    for c in range(n_chunk):
        if c + 1 < n_chunk:
            up_project(c + 1, bufs[(c + 1) % 2])
        down_project(c, bufs[c % 2])

    h2 = x + acc[...]
    gate = jax.nn.sigmoid(_dot(_rms(h2, gple_ref[...]).astype(BF16), wgate_ref[...]))
    pe = _dot(p_ref[...].astype(BF16), win_ref[...])
    out_ref[...] = h2 + pe * gate


def _ffn_weights(g_ffn, w_up, conv_w, conv_b, w_down, g_ple, w_gate, w_in):
    depth, d, ff2 = w_up.shape
    return dict(
        g_ffn=g_ffn.reshape(depth, 1, d),
        w_up=w_up.astype(BF16),
        conv_w=conv_w,
        conv_b=conv_b.reshape(depth, 1, ff2),
        w_down=w_down.astype(BF16),
        g_ple=g_ple.reshape(depth, 1, d),
        w_gate=w_gate.astype(BF16),
        w_in=w_in.astype(BF16),
    )


def _layer_spec(shape, layer):
    nd = len(shape)
    return pl.BlockSpec((None,) + tuple(shape[1:]), lambda *_: (layer,) + (0,) * (nd - 1),
                        pipeline_mode=pl.Buffered(1))


def _ffn(h, p, layer, head, head_off, fw, rows, n_seq_tiles, shift, attn=None):
    total, d = h.shape
    tiles = total // rows // n_seq_tiles
    head_rows, ff2 = head.shape[1], head.shape[2]
    row_map = lambda i, t: (i * tiles + t, 0)
    p_map = lambda i, t: (layer * (total // rows) + i * tiles + t, 0)
    consts = [fw['g_ffn'], fw['w_up'], fw['conv_w'], fw['conv_b'], fw['w_down'], fw['g_ple'], fw['w_gate'],
              fw['w_in']]
    args = [h]
    in_specs = [pl.BlockSpec((rows, d), row_map)]
    if attn is not None:
        o, wo = attn
        args += [o, wo]
        in_specs += [pl.BlockSpec((rows, o.shape[1]), row_map), _const_spec(wo.shape)]
    args += [p, head] + consts
    in_specs += [pl.BlockSpec((rows, p.shape[1]), p_map),
                 pl.BlockSpec((None, head_rows, ff2), lambda i, t: (head_off + i, 0, 0))]
    in_specs += [_layer_spec(c.shape, layer) for c in consts]
    out, tail = pl.pallas_call(
        functools.partial(_ffn_kernel, rows=rows, head_rows=head_rows, shift=shift, has_attn=attn is not None),
        grid=(n_seq_tiles, tiles),
        in_specs=in_specs,
        out_specs=[pl.BlockSpec((rows, d), row_map),
                   pl.BlockSpec((None, head_rows, ff2), lambda i, t: (i, 0, 0))],
        out_shape=[jax.ShapeDtypeStruct((total, d), F32),
                   jax.ShapeDtypeStruct((n_seq_tiles, head_rows, ff2), F32)],
        scratch_shapes=[pltpu.VMEM((rows, d), F32),
                        pltpu.VMEM((rows, d), BF16),
                        pltpu.VMEM((2, FF_CHUNK // LANES, head_rows + rows, LANES), F32),
                        pltpu.VMEM((2, FF_CHUNK // LANES, head_rows + rows, LANES), F32)],
        compiler_params=pltpu.CompilerParams(dimension_semantics=("arbitrary", "arbitrary"),
                                             vmem_limit_bytes=VMEM_LIMIT),
        name="ffn_ple",
    )(*args)
    return out, tail


def _pair_spread(x, fill):
    assert x.shape[1] == 2 * LANES and HEAD_DIM * 2 == LANES
    r = pltpu.roll(x, HEAD_DIM, 1)
    low = lax.broadcasted_iota(jnp.int32, (x.shape[0], LANES), 1) < HEAD_DIM
    x01, x23 = x[:, :LANES], x[:, LANES:]
    r30, r12 = r[:, :LANES], r[:, LANES:]
    first = lambda t: jnp.where(low, t, fill)
    second = lambda t: jnp.where(low, fill, t)
    return jnp.concatenate([first(x01), second(r30), first(r12), second(x01),
                            first(x23), second(r12), first(r30), second(x23)], axis=1)


def _qkv_kernel(*refs, paired):
    (h_ref, gkv_ref, gmix_ref, wk_ref, wv_ref, wq_ref, knorm_ref, qnorm_ref, kpool_ref, kspread_ref,
     qpool_ref, qspread_ref, q_ref, k_ref, v_ref) = refs[:15]
    x = h_ref[...]
    s = _rms(x, gkv_ref[...]).astype(BF16)
    k = _head_rms(_dot(s, wk_ref[...]), kpool_ref[...], kspread_ref[...], knorm_ref[...])
    v = _dot(s, wv_ref[...])
    xn = _rms(x, gmix_ref[...]).astype(BF16)
    q = _head_rms(_dot(xn, wq_ref[...]), qpool_ref[...], qspread_ref[...], qnorm_ref[...])
    if paired:
        kx_ref, vx_ref = refs[15:]
        kx_ref[...] = _pair_spread(k, 0.0).astype(BF16)
        vx_ref[...] = _pair_spread(v, 1.0).astype(BF16)
    k_ref[...] = k
    v_ref[...] = v
    q_ref[...] = (q * (HEAD_DIM ** -0.5)).astype(q_ref.dtype)


def _head_pool_matrices(n):
    member = (jnp.arange(n)[:, None] // HEAD_DIM) == jnp.arange(LANES)[None, :]
    return member.astype(BF16) * (1.0 / HEAD_DIM), member.T.astype(BF16)


def _qkv(h, qw, rows, paired):
    total, d = h.shape
    hk = qw['w_k'].shape[1]
    hq = qw['w_q'].shape[1]
    consts = [qw['g_kv'], qw['g_mix'], qw['w_k'], qw['w_v'], qw['w_q'], qw['k_norm'], qw['q_norm'],
              *qw['k_pool'], *qw['q_pool']]
    row_map = lambda i: (i, 0)
    out_specs = [pl.BlockSpec((rows, hq), row_map), pl.BlockSpec((rows, hk), row_map),
                 pl.BlockSpec((rows, hk), row_map)]
    out_shape = [jax.ShapeDtypeStruct((total, hq), BF16),
                 jax.ShapeDtypeStruct((total, hk), F32),
                 jax.ShapeDtypeStruct((total, hk), F32)]
    if paired:
        out_specs += [pl.BlockSpec((rows, 4 * hk), row_map)] * 2
        out_shape += [jax.ShapeDtypeStruct((total, 4 * hk), BF16)] * 2
    return pl.pallas_call(
        functools.partial(_qkv_kernel, paired=paired),
        grid=(total // rows,),
        in_specs=[pl.BlockSpec((rows, d), row_map)] + [_const_spec(c.shape) for c in consts],
        out_specs=out_specs,
        out_shape=out_shape,
        compiler_params=pltpu.CompilerParams(dimension_semantics=("arbitrary",),
                                             vmem_limit_bytes=VMEM_LIMIT),
        name="qkv",
    )(h, *consts)


def _t5_bucket(dist):
    max_exact = NUM_BUCKETS // 2
    df = jnp.maximum(dist, 1).astype(F32)
    large = max_exact + (jnp.log(df / max_exact) / math.log(MAX_DISTANCE / max_exact)
                         * (NUM_BUCKETS - max_exact)).astype(jnp.int32)
    return jnp.where(dist < max_exact, dist, jnp.minimum(large, NUM_BUCKETS - 1))


def _bias_table(rel_bias, lb, qb, lk):
    dist = lb + jnp.arange(qb)[:, None] - jnp.arange(lk)[None, :]
    per_dist = rel_bias[_t5_bucket(jnp.arange(WINDOW + 1))].astype(F32)
    onehot = (jnp.clip(dist, 0, WINDOW)[..., None] == jnp.arange(WINDOW + 1)).astype(F32)
    bias = jnp.einsum('qkd,dh->hqk', onehot, per_dist, precision=lax.Precision.HIGHEST)
    ok = (dist >= 0) & (dist <= WINDOW)
    return jnp.where(ok[None], bias, -jnp.inf)


def _softmax_pv(s, sink, v):
    m = jnp.maximum(jnp.max(s, axis=-1, keepdims=True), sink)
    pr = jnp.exp(s - m)
    den = jnp.sum(pr, axis=-1, keepdims=True) + jnp.exp(sink - m)
    return _dot(pr.astype(BF16), v) / den


def _attn_prompt_kernel(sink_ref, q_ref, kx_ref, vx_ref, bias_ref, o_ref, kx_prev, vx_prev):
    qb = kx_prev.shape[0]
    n_blocks = q_ref.shape[0] // qb
    n_heads = bias_ref.shape[1]
    rep = n_heads // N_KV_HEADS
    grp = 4 * HEAD_DIM
    step = pl.program_id(1)

    @pl.when(step == 0)
    def _():
        kx_prev[...] = jnp.zeros_like(kx_prev)
        vx_prev[...] = jnp.zeros_like(vx_prev)

    first_table = jnp.minimum(step, 1)
    low = lax.broadcasted_iota(jnp.int32, (qb, LANES), 1) < HEAD_DIM
    for g in range(N_KV_HEADS):
        gs = slice(g * grp, (g + 1) * grp)
        kx_all = jnp.concatenate([kx_prev[:, gs], kx_ref[:, gs]], axis=0)
        vx_all = jnp.concatenate([vx_prev[:, gs], vx_ref[:, gs]], axis=0)
        for blk in range(n_blocks):
            rows = slice(blk * qb, (blk + 1) * qb)
            kx = kx_all[blk * qb:(blk + 2) * qb]
            vx = vx_all[blk * qb:(blk + 2) * qb]
            table = first_table if blk == 0 else 1
            for pair in range(rep // 2):
                h0 = g * rep + 2 * pair
                ls = slice(h0 * HEAD_DIM, h0 * HEAD_DIM + LANES)
                qp = q_ref[rows, ls]
                res = []
                for h, kh in ((h0, kx[:, :LANES]), (h0 + 1, kx[:, LANES:])):
                    s = _dot_nt(qp, kh) + bias_ref[table, h]
                    m = jnp.maximum(jnp.max(s, axis=-1, keepdims=True), sink_ref[h])
                    res.append((_dot(jnp.exp(s - m).astype(BF16), vx), jnp.exp(sink_ref[h] - m)))
                (ra, ea), (rb, eb) = res
                num = jnp.where(low, ra[:, :LANES], rb[:, LANES:])
                den = jnp.where(low, ra[:, LANES:], rb[:, :LANES]) + jnp.where(low, ea, eb)
                o_ref[rows, ls] = (num / den).astype(o_ref.dtype)
    last = slice((n_blocks - 1) * qb, n_blocks * qb)
    kx_prev[...] = kx_ref[last, :]
    vx_prev[...] = vx_ref[last, :]


def _attn_prompt(q, kx, vx, bias, sinks, n, l):
    hq = q.shape[1]
    wx = kx.shape[1]
    qb = WINDOW
    rows = next(m * qb for m in (4, 2, 1) if l % (m * qb) == 0)
    steps = l // rows
    cur = lambda i, b, *_: (i * steps + b, 0)
    return pl.pallas_call(
        _attn_prompt_kernel,
        grid_spec=pltpu.PrefetchScalarGridSpec(
            num_scalar_prefetch=1,
            grid=(n, steps),
            in_specs=[pl.BlockSpec((rows, hq), cur), pl.BlockSpec((rows, wx), cur), pl.BlockSpec((rows, wx), cur),
                      pl.BlockSpec(bias.shape, lambda i, b, *_: (0, 0, 0, 0), pipeline_mode=pl.Buffered(1))],
            out_specs=pl.BlockSpec((rows, hq), cur),
            scratch_shapes=[pltpu.VMEM((qb, wx), BF16), pltpu.VMEM((qb, wx), BF16)]),
        out_shape=jax.ShapeDtypeStruct((n * l, hq), BF16),
        compiler_params=pltpu.CompilerParams(dimension_semantics=("arbitrary", "arbitrary"),
                                             vmem_limit_bytes=VMEM_LIMIT),
        name="attn_prompt",
    )(sinks, q, kx, vx, bias)


def _attn_sample_kernel(sink_ref, q_ref, kc_ref, vc_ref, kn_ref, vn_ref, bias_ref,
                        o_ref, kwin_ref, vwin_ref, *, n_new):
    lb = kc_ref.shape[1]
    new_rows = kn_ref.shape[1]
    rows = q_ref.shape[1]
    hk = kc_ref.shape[2]
    grp = rows // N_KV_HEADS
    zero_rows = jnp.zeros((bias_ref.shape[1] - lb - new_rows, hk), BF16)
    lane_head = lax.broadcasted_iota(jnp.int32, (grp, hk), 1) // HEAD_DIM
    samples = range(q_ref.shape[0])
    scores = []
    for s_i in samples:
        kk = jnp.concatenate([kc_ref[s_i].astype(BF16), kn_ref[s_i].astype(BF16), zero_rows], axis=0)
        scores.append(_dot_nt(q_ref[s_i], kk) + bias_ref[...])
    probs = []
    for s in scores:
        m = jnp.maximum(jnp.max(s, axis=-1, keepdims=True), sink_ref[...])
        pr = jnp.exp(s - m)
        probs.append((pr.astype(BF16), jnp.sum(pr, axis=-1, keepdims=True) + jnp.exp(sink_ref[...] - m)))
    for s_i, (pr, den) in zip(samples, probs):
        vv = jnp.concatenate([vc_ref[s_i].astype(BF16), vn_ref[s_i].astype(BF16), zero_rows], axis=0)
        pv = _dot(pr, vv) / den
        o = jnp.zeros((grp, hk), F32)
        for g in range(N_KV_HEADS):
            o = o + jnp.where(lane_head == g, pv[g * grp:(g + 1) * grp, :], 0.0)
        o_ref[s_i] = o
    for s_i in samples:
        kwin_ref[s_i, 0:lb - n_new, :] = kc_ref[s_i, n_new:lb, :]
        kwin_ref[s_i, lb - n_new:lb, :] = kn_ref[s_i, 0:n_new, :]
        vwin_ref[s_i, 0:lb - n_new, :] = vc_ref[s_i, n_new:lb, :]
        vwin_ref[s_i, lb - n_new:lb, :] = vn_ref[s_i, 0:n_new, :]


def _attn_sample(q_blk, kc, vc, kn, vn, bias, sink, n_new, block):
    ns, rows, hk = q_blk.shape
    lb = kc.shape[1]
    lkp = bias.shape[1]
    grp = rows // N_KV_HEADS
    per_s = lambda i: (i, 0, 0)
    return pl.pallas_call(
        functools.partial(_attn_sample_kernel, n_new=n_new),
        grid=(ns // block,),
        in_specs=[_const_spec(sink.shape),
                  pl.BlockSpec((block, rows, hk), per_s),
                  pl.BlockSpec((block, lb, hk), per_s), pl.BlockSpec((block, lb, hk), per_s),
                  pl.BlockSpec((block,) + kn.shape[1:], per_s), pl.BlockSpec((block,) + vn.shape[1:], per_s),
                  _const_spec(bias.shape)],
        out_specs=[pl.BlockSpec((block, grp, hk), per_s),
                   pl.BlockSpec((block, lb, hk), per_s), pl.BlockSpec((block, lb, hk), per_s)],
        out_shape=[jax.ShapeDtypeStruct((ns, grp, hk), F32),
                   jax.ShapeDtypeStruct((ns, lb, hk), F32),
                   jax.ShapeDtypeStruct((ns, lb, hk), F32)],
        compiler_params=pltpu.CompilerParams(dimension_semantics=("arbitrary",),
                                             vmem_limit_bytes=VMEM_LIMIT),
        name="attn_sample",
    )(sink, q_blk, kc, vc, kn, vn, bias)


def kernel(x_prompt, x_sample, state_ssm_re, state_ssm_im, state_ffn_conv, cache_k_win, cache_v_win, p_prompt, p_sample, g_mix, g_ffn, g_ple, ssm_lam_re, ssm_lam_im, ssm_log_dt, ssm_b_re, ssm_b_im, ssm_c_re, ssm_c_im, ssm_d, w_glu, b_glu, g_kv, w_k, w_v, k_norm, w_q, q_norm, sinks, w_o, rel_bias, w_up, conv_w, conv_b, w_down, w_ple_in, w_ple_gate):
    n, l, d = x_prompt.shape
    ns, ls, _ = x_sample.shape
    n_groups, n_state = ssm_lam_re.shape[1:]
    nst = n_groups * n_state
    ff2 = w_up.shape[2]
    hk = w_k.shape[1]
    hq = w_q.shape[2]
    n_heads = hq // HEAD_DIM
    rep = n_heads // N_KV_HEADS
    lb = cache_k_win.shape[1]

    ssm_steps = min(32, l // SUBLANES)
    ffn_rows = min(512, l)
    head_rows_p = SUBLANES
    tm = lambda a: jnp.swapaxes(a, 0, 1)

    sp = _ssm_params(ssm_lam_re[0], ssm_lam_im[0], ssm_log_dt[0], ssm_b_re[0], ssm_b_im[0], ssm_c_re[0],
                     ssm_c_im[0], ssm_steps)
    gmix0 = g_mix[0].reshape(1, d)
    dskip = ssm_d[0].reshape(1, d)
    wglu = jnp.transpose(w_glu[0].astype(BF16).reshape(d, 2 * d // MXU_DIM, MXU_DIM), (1, 0, 2))
    bglu = b_glu[0].reshape(1, 2 * d)
    hp, sre_p, sim_p = _ssm_prompt(x_prompt, sp, gmix0, dskip, wglu, bglu, ssm_steps)
    hp = hp.reshape(n * l, d)
    xs_tm = tm(x_sample).reshape(ls * ns, d)
    hs, sre_s, sim_s = _ssm_sample(xs_tm, state_ssm_re[0].reshape(ns, nst), state_ssm_im[0].reshape(ns, nst),
                                   sp, gmix0, dskip, wglu, bglu, ns, ls)

    pp = p_prompt.reshape(p_prompt.shape[0] * n * l, -1)
    ps = jnp.swapaxes(p_sample, 1, 2).reshape(p_sample.shape[0] * ls * ns, -1)
    zero_head = jnp.zeros((n, head_rows_p, ff2), F32)
    conv_s_tm = jnp.swapaxes(state_ffn_conv, 1, 2).reshape(state_ffn_conv.shape[0], (CONV_WIDTH - 1) * ns, ff2)

    fw = _ffn_weights(g_ffn, w_up, conv_w, conv_b, w_down, g_ple, w_ple_gate, w_ple_in)

    def ffn_layer(i, hp, hs, attn_p=None, attn_s=None):
        hp, tail_p = _ffn(hp, pp, i, zero_head, 0, fw, ffn_rows, n, 1, attn_p)
        hs, tail_s = _ffn(hs, ps, i, conv_s_tm, i, fw, ls * ns, 1, ns, attn_s)
        conv_p = tail_p[:, head_rows_p - (CONV_WIDTH - 1):, :]
        return hp, hs, conv_p, tail_s.reshape(CONV_WIDTH - 1, ns, ff2)

    hp, hs, conv_p0, conv_s0 = ffn_layer(0, hp, hs)

    qw = dict(g_kv=g_kv.reshape(1, d), g_mix=g_mix[1].reshape(1, d), w_k=w_k.astype(BF16), w_v=w_v.astype(BF16),
              w_q=w_q[0].astype(BF16), k_norm=jnp.tile(k_norm, hk // HEAD_DIM).reshape(1, hk),
              q_norm=jnp.tile(q_norm[0], n_heads).reshape(1, hq),
              k_pool=_head_pool_matrices(hk), q_pool=_head_pool_matrices(hq))
    qkv_rows = 2 * ffn_rows if (n * l) % (2 * ffn_rows) == 0 else ffn_rows
    q_p, k_p, v_p, kx_p, vx_p = _qkv(hp, qw, qkv_rows, True)
    q_s, k_s, v_s = _qkv(hs, qw, ls * ns, False)

    wo = w_o[0].astype(BF16)
    bias_p = _bias_table(rel_bias, WINDOW, WINDOW, 2 * WINDOW)
    bias_first = jnp.where(jnp.arange(2 * WINDOW) >= WINDOW, bias_p, -jnp.inf)
    o_p = _attn_prompt(q_p, kx_p, vx_p, jnp.stack([bias_first, bias_p]), sinks[0].astype(F32), n, l)

    lkp = 2 * WINDOW
    new_rows = 16
    pad_new = lambda a: jnp.pad(tm(a.reshape(ls, ns, hk)), ((0, 0), (0, new_rows - ls), (0, 0)))
    kc = cache_k_win.reshape(ns, lb, hk)
    vc = cache_v_win.reshape(ns, lb, hk)
    bias_s = _bias_table(rel_bias, lb, ls, lb + ls)
    bias_s = jnp.pad(bias_s, ((0, 0), (0, 0), (0, lkp - lb - ls)), constant_values=-jnp.inf)
    bias_s = bias_s.reshape(n_heads * ls, lkp)
    sink_s = jnp.repeat(sinks[0].astype(F32), ls).reshape(n_heads * ls, 1)
    q5 = jnp.transpose(q_s.reshape(ls, ns, N_KV_HEADS, rep, HEAD_DIM), (1, 2, 3, 0, 4))
    q_blk = jnp.einsum('sgrtd,gh->sgrthd', q5, jnp.eye(N_KV_HEADS, dtype=q5.dtype))
    q_blk = q_blk.reshape(ns, n_heads * ls, hk)
    o4, k_win_s, v_win_s = _attn_sample(q_blk, kc, vc, pad_new(k_s), pad_new(v_s), bias_s, sink_s, ls, 8)
    o_s = jnp.transpose(o4.reshape(ns, rep, ls, N_KV_HEADS, HEAD_DIM), (2, 0, 3, 1, 4)).reshape(ls * ns, hq)
    o_s = o_s.astype(BF16)

    hp, hs, conv_p1, conv_s1 = ffn_layer(1, hp, hs, (o_p, wo), (o_s, wo))

    y_prompt = hp.reshape(n, l, d)
    y_sample = tm(hs.reshape(ls, ns, d))
    ssm_shape = (1, -1, n_groups, n_state)
    kvh_shape = (-1, lb, N_KV_HEADS, HEAD_DIM)
    k_win_p = k_p.reshape(n, l, hk)[:, l - WINDOW:].reshape(n, WINDOW, N_KV_HEADS, HEAD_DIM)
    v_win_p = v_p.reshape(n, l, hk)[:, l - WINDOW:].reshape(n, WINDOW, N_KV_HEADS, HEAD_DIM)
    return (y_prompt, y_sample,
            sre_p.reshape(ssm_shape), sim_p.reshape(ssm_shape),
            sre_s.reshape(ssm_shape), sim_s.reshape(ssm_shape),
            jnp.stack([conv_p0, conv_p1]), jnp.swapaxes(jnp.stack([conv_s0, conv_s1]), 1, 2),
            k_win_p, v_win_p, k_win_s.reshape(kvh_shape), v_win_s.reshape(kvh_shape))
```

```python
import functools
import math

import jax
import jax.numpy as jnp
from jax import lax
from jax.experimental import pallas as pl
from jax.experimental.pallas import tpu as pltpu

F32 = jnp.float32
BF16 = jnp.bfloat16

EPS = 1e-6
HEAD_DIM = 64
N_KV_HEADS = 4
WINDOW = 128
NUM_BUCKETS = 32
MAX_DISTANCE = 128
CONV_WIDTH = 3

LANES = 128
SUBLANES = 8
MXU_DIM = 256
VMEM_LIMIT = 56 * 1024 * 1024

SCAN_COLS = 1024
FF_CHUNK = 256


def _dot(a, b):
    return jnp.dot(a, b, preferred_element_type=F32)


def _dot_nt(a, b):
    return lax.dot_general(a, b, (((1,), (1,)), ((), ())), preferred_element_type=F32)


def _rms(x, g):
    ms = jnp.mean(x * x, axis=-1, keepdims=True)
    return x * lax.rsqrt(ms + EPS) * g


def _head_rms(x, pool, spread, g):
    ms = _dot((x * x).astype(BF16), pool)
    scale = lax.rsqrt(ms + EPS)
    hi = scale.astype(BF16)
    lo = (scale - hi.astype(F32)).astype(BF16)
    return x * (_dot(hi, spread) + _dot(lo, spread)) * g


def _const_spec(shape):
    nd = len(shape)
    return pl.BlockSpec(shape, lambda *_: (0,) * nd, pipeline_mode=pl.Buffered(1))


def _b_project(ub, bw_ref, bure, buim):
    n_piece, kw, nw = bw_ref.shape[1:]
    per_kt = n_piece * kw // ub.shape[1]
    for q in range(n_piece):
        lhs = ub[:, (q // per_kt) * kw:(q // per_kt + 1) * kw]
        bure[:, q * nw:(q + 1) * nw] = _dot(lhs, bw_ref[0, q])
        buim[:, q * nw:(q + 1) * nw] = _dot(lhs, bw_ref[1, q])


def _scan(bure, buim, are_ref, aim_ref, n_seq, n_steps, init_fn, final_fn):
    nst = bure.shape[1]
    for cb in range(nst // SCAN_COLS):
        cs = slice(cb * SCAN_COLS, (cb + 1) * SCAN_COLS)
        ar = jnp.broadcast_to(are_ref[:, cs], (SUBLANES, SCAN_COLS))
        ai = jnp.broadcast_to(aim_ref[:, cs], (SUBLANES, SCAN_COLS))

        def group(g, _, cs=cs, ar=ar, ai=ai):
            r0 = pl.multiple_of(g * SUBLANES, SUBLANES)

            def step(k, carry):
                hr, hi = carry
                row = pl.multiple_of(k * n_seq + r0, SUBLANES)
                br = bure[pl.ds(row, SUBLANES), cs]
                bi = buim[pl.ds(row, SUBLANES), cs]
                nr = ar * hr - ai * hi + br
                ni = ar * hi + ai * hr + bi
                bure[pl.ds(row, SUBLANES), cs] = nr
                buim[pl.ds(row, SUBLANES), cs] = ni
                return nr, ni

            hr, hi = lax.fori_loop(0, n_steps, step, init_fn(r0, cs), unroll=min(n_steps, 4))
            final_fn(r0, cs, hr, hi)
            return 0

        lax.fori_loop(0, n_seq // SUBLANES, group, 0, unroll=4)


def _c_project_glu(x, u, bure, buim, cre_ref, ncim_ref, dskip_ref, wglu_ref, bglu_ref):
    d = x.shape[1]
    n_blk = cre_ref.shape[0]
    kw = cre_ref.shape[1]
    ys = []
    for m in range(n_blk):
        hr = bure[:, m * kw:(m + 1) * kw].astype(BF16)
        hi = buim[:, m * kw:(m + 1) * kw].astype(BF16)
        ys.append(_dot(hr, cre_ref[m]) + _dot(hi, ncim_ref[m]))
    y = jnp.concatenate(ys, axis=1) + dskip_ref[...] * u
    z = jax.nn.gelu(y).astype(BF16)
    gl = jnp.concatenate([_dot(z, wglu_ref[nt]) for nt in range(wglu_ref.shape[0])], axis=1) + bglu_ref[...]
    return x + gl[:, :d] * jax.nn.sigmoid(gl[:, d:])


def _ssm_prompt_kernel(x_ref, gmix_ref, are_ref, aim_ref, apw_re_ref, apw_im_ref, bw_ref, cre_ref, ncim_ref,
                       dskip_ref, wglu_ref, bglu_ref,
                       out_ref, sre_ref, sim_ref,
                       slab, xp, ub, us, bu_a, bu_b, zb, gl, ends, hin, car, *, n_steps, pitch, seq_tiles):
    g_step = pl.program_id(0)
    for parity, (bu_next, bu_cur) in enumerate(((bu_a, bu_b), (bu_b, bu_a))):
        pl.when(g_step % 2 == parity)(functools.partial(
            _ssm_prompt_step, parity, g_step, x_ref, gmix_ref, are_ref, aim_ref, apw_re_ref, apw_im_ref, bw_ref,
            cre_ref, ncim_ref, dskip_ref, wglu_ref, bglu_ref, out_ref, sre_ref, sim_ref,
            slab, xp, ub, us, bu_next, bu_cur, zb, gl, ends, hin, car, n_steps, pitch, seq_tiles))


def _ssm_prompt_step(nxt, g_step, x_ref, gmix_ref, are_ref, aim_ref, apw_re_ref, apw_im_ref, bw_ref, cre_ref,
                     ncim_ref, dskip_ref, wglu_ref, bglu_ref, out_ref, sre_ref, sim_ref,
                     slab, xp, ub, us, bu_next, bu_cur, zb, gl, ends, hin, car, n_steps, pitch, seq_tiles):
    cur = 1 - nxt
    n_slab = slab.shape[0]
    d = xp.shape[2]
    n_piece = bu_cur.shape[1]
    blk_pieces = SCAN_COLS // MXU_DIM
    n_blk = n_piece // blk_pieces
    trips = 4
    spt = n_steps // trips
    n_cblk = cre_ref.shape[0]
    n_gtile = wglu_ref.shape[0]
    assert 2 * n_piece == n_blk * trips * 2
    assert 2 * n_gtile == n_blk * trips
    assert bw_ref.shape[2] == MXU_DIM and 2 * n_cblk == n_piece

    def lanes(q):
        return slice(q * MXU_DIM, (q + 1) * MXU_DIM)

    for j in range(SUBLANES):
        for c in range(n_slab):
            slab[c, j * pitch:j * pitch + n_steps, :] = x_ref[j * n_steps:(j + 1) * n_steps,
                                                             c * LANES:(c + 1) * LANES]

    def gather(k, _):
        r0 = pl.multiple_of(k * SUBLANES, SUBLANES)
        for c in range(n_slab):
            xp[nxt, pl.ds(r0, SUBLANES), c * LANES:(c + 1) * LANES] = slab[c, pl.ds(k, SUBLANES, stride=pitch), :]
        return 0

    lax.fori_loop(0, n_steps, gather, 0, unroll=True)
    u = _rms(xp[nxt], gmix_ref[...])
    ub[nxt] = u.astype(BF16)
    for c in range(n_slab):
        us[nxt, c] = u[:, c * LANES:(c + 1) * LANES]

    def run_pass(bu, store, init_fn, end_fn, work_fn):
        for blk in range(n_blk):
            coef = [(jnp.broadcast_to(are_ref[:, lanes(blk * blk_pieces + nt)], (SUBLANES, MXU_DIM)),
                     jnp.broadcast_to(aim_ref[:, lanes(blk * blk_pieces + nt)], (SUBLANES, MXU_DIM)))
                    for nt in range(blk_pieces)]

            def trip(i, state, blk=blk, coef=coef):
                if work_fn is not None:
                    work_fn(blk, i)
                state = list(state)
                for s in range(spt):
                    row = (i * spt + s) * SUBLANES
                    for nt in range(blk_pieces):
                        q = blk * blk_pieces + nt
                        ar, ai = coef[nt]
                        hr, hi = state[2 * nt], state[2 * nt + 1]
                        nr = ar * hr - ai * hi + bu[0, q, pl.ds(row, SUBLANES), :]
                        ni = ar * hi + ai * hr + bu[1, q, pl.ds(row, SUBLANES), :]
                        if store:
                            bu[0, q, pl.ds(row, SUBLANES), :] = nr
                            bu[1, q, pl.ds(row, SUBLANES), :] = ni
                        state[2 * nt], state[2 * nt + 1] = nr, ni
                return tuple(state)

            state = init_fn(blk)
            for i in range(trips):
                state = trip(i, state)
            end_fn(blk, state)

    def zero_init(blk):
        return tuple(jnp.zeros((SUBLANES, MXU_DIM), F32) for _ in range(2 * blk_pieces))

    def keep_ends(blk, state):
        for nt in range(blk_pieces):
            ends[0, :, lanes(blk * blk_pieces + nt)] = state[2 * nt]
            ends[1, :, lanes(blk * blk_pieces + nt)] = state[2 * nt + 1]

    def true_init(blk):
        return tuple(hin[ri, :, lanes(blk * blk_pieces + nt)] for nt in range(blk_pieces) for ri in range(2))

    tiles_per_kt = 2 * blk_pieces

    def b_project_tile(t):
        kt, j = divmod(t, tiles_per_kt)
        ri, nt = divmod(j, blk_pieces)
        q = kt * blk_pieces + nt
        bu_next[ri, q] = _dot(ub[nxt, :, lanes(kt)], bw_ref[ri, q])

    def b_project_slice(blk, i):
        b_project_tile(2 * (blk * trips + i))
        b_project_tile(2 * (blk * trips + i) + 1)

    def c_project_block(m):
        y = dskip_ref[m] * us[cur, m]
        for ri, c_ref in ((0, cre_ref), (1, ncim_ref)):
            for w in range(2):
                y = y + _dot(bu_cur[ri, 2 * m + w].astype(BF16), c_ref[m, w * MXU_DIM:(w + 1) * MXU_DIM, :])
        zb[m] = jax.nn.gelu(y).astype(BF16)

    def glu_slice(blk, i):
        t = blk * trips + i
        if t % 2 == 0:
            nt = t // 2
            z = jnp.concatenate([zb[c] for c in range(n_cblk)], axis=1)
            gl[nt] = _dot(z, wglu_ref[nt]) + bglu_ref[nt]

    def first_tile():
        car[...] = jnp.zeros_like(car)
        for blk in range(n_blk):
            lhs = ub[nxt, :, lanes(blk)]
            for ri in range(2):
                for nt in range(blk_pieces):
                    bu_next[ri, blk * blk_pieces + nt] = _dot(lhs, bw_ref[ri, blk * blk_pieces + nt])
        run_pass(bu_next, False, zero_init, keep_ends, None)

    if nxt == 0:
        pl.when(g_step == 0)(first_tile)

    @pl.when(g_step > 0)
    def _():
        fresh = (g_step - 1) % seq_tiles == 0
        hr = jnp.where(fresh, 0.0, car[0])
        hi = jnp.where(fresh, 0.0, car[1])
        apr = apw_re_ref[...]
        api = apw_im_ref[...]
        for j in range(SUBLANES):
            er = ends[0, j:j + 1, :]
            ei = ends[1, j:j + 1, :]
            hin[0, j:j + 1, :] = hr
            hin[1, j:j + 1, :] = hi
            hr, hi = apr * hr - api * hi + er, apr * hi + api * hr + ei
        car[0] = hr
        car[1] = hi
        sre_ref[...] = hr
        sim_ref[...] = hi

        run_pass(bu_cur, True, true_init, lambda *_: None, b_project_slice)

    @pl.when(g_step != 0)
    def _():
        for m in range(n_cblk):
            c_project_block(m)

    @pl.when(jnp.logical_and(g_step >= 1, pl.program_id(0) >= 0))
    def _():
        run_pass(bu_next, False, zero_init, keep_ends, glu_slice)

        half = n_gtile // 2
        for c in range(half):
            xp[cur, :, lanes(c)] = xp[cur, :, lanes(c)] + gl[c] * jax.nn.sigmoid(gl[half + c])

        def scatter(k, _):
            r0 = pl.multiple_of(k * SUBLANES, SUBLANES)
            for c in range(n_slab):
                slab[c, pl.ds(k, SUBLANES, stride=pitch), :] = xp[cur, pl.ds(r0, SUBLANES), c * LANES:(c + 1) * LANES]
            return 0

        lax.fori_loop(0, n_steps, scatter, 0, unroll=True)
        for j in range(SUBLANES):
            for c in range(n_slab):
                out_ref[j * n_steps:(j + 1) * n_steps, c * LANES:(c + 1) * LANES] = slab[c, j * pitch:j * pitch + n_steps, :]


def _ssm_sample_kernel(x_ref, h0re_ref, h0im_ref, gmix_ref, are_ref, aim_ref, bw_ref, cre_ref,
                       ncim_ref, dskip_ref, wglu_ref, bglu_ref,
                       out_ref, sre_ref, sim_ref, bure, buim, *, n_seq, n_steps):
    x = x_ref[...]
    u = _rms(x, gmix_ref[...])
    _b_project(u.astype(BF16), bw_ref, bure, buim)

    def init(r0, cs):
        return h0re_ref[pl.ds(r0, SUBLANES), cs], h0im_ref[pl.ds(r0, SUBLANES), cs]

    def final(r0, cs, hr, hi):
        sre_ref[pl.ds(r0, SUBLANES), cs] = hr
        sim_ref[pl.ds(r0, SUBLANES), cs] = hi

    _scan(bure, buim, are_ref, aim_ref, n_seq, n_steps, init, final)
    out_ref[...] = _c_project_glu(x, u, bure, buim, cre_ref, ncim_ref, dskip_ref, wglu_ref, bglu_ref)


def _ssm_params(lam_re, lam_im, log_dt, b_re, b_im, c_re, c_im, n_pow):
    g, p = lam_re.shape
    lr = lam_re.astype(F32)
    li = lam_im.astype(F32)
    dt = jnp.exp(log_dt.astype(F32))[:, None]
    mag = jnp.exp(lr * dt)
    ang = li * dt
    ab_re = mag * jnp.cos(ang)
    ab_im = mag * jnp.sin(ang)
    den = lr * lr + li * li
    nr = ab_re - 1.0
    f_re = (nr * lr + ab_im * li) / den
    f_im = (ab_im * lr - nr * li) / den
    br = b_re.astype(F32)
    bi = b_im.astype(F32)
    bb_re = f_re[..., None] * br - f_im[..., None] * bi
    bb_im = f_re[..., None] * bi + f_im[..., None] * br

    c = bb_re.shape[2]
    gk = MXU_DIM // c

    def b_tiles(bb):
        rows_kt = jnp.transpose(bb, (0, 2, 1)).reshape(g // gk, gk * c, p)
        per_tile = MXU_DIM // p
        wide = jnp.tile(rows_kt, (1, 1, per_tile))[:, None]
        row_group = jnp.arange(gk * c)[:, None] // c
        col_group = jnp.arange(MXU_DIM)[None, :] // p
        nt = jnp.arange(gk // per_tile)[:, None, None]
        own = (row_group[None] == nt * per_tile + col_group[None]).astype(F32)
        return (wide * own[None]).reshape(g * p // MXU_DIM, gk * c, MXU_DIM).astype(BF16)

    gc = LANES // c
    eye_c = jnp.eye(gc, dtype=F32)

    def c_blocks(cc):
        ct = jnp.transpose(cc.astype(F32), (0, 2, 1)).reshape(g // gc, gc, p, c)
        return jnp.einsum('tgpc,gh->tgphc', ct, eye_c).reshape(g // gc, gc * p, gc * c).astype(BF16)

    pw_re = ab_re.reshape(1, g * p)
    pw_im = ab_im.reshape(1, g * p)
    for _ in range(n_pow.bit_length() - 1):
        pw_re, pw_im = pw_re * pw_re - pw_im * pw_im, 2.0 * pw_re * pw_im
    return dict(a_re=ab_re.reshape(1, g * p), a_im=ab_im.reshape(1, g * p),
                ap_re=pw_re, ap_im=pw_im,
                b_tiles=jnp.stack([b_tiles(bb_re), b_tiles(bb_im)]),
                c_re=c_blocks(c_re), nc_im=c_blocks(-c_im))


def _ssm_prompt(x, sp, gmix, dskip, wglu, bglu, n_steps):
    n, l, d = x.shape
    nst = sp['a_re'].shape[1]
    rows = SUBLANES * n_steps
    pitch = n_steps + SUBLANES
    tiles = l // rows
    n_piece = nst // MXU_DIM
    n_gtile = wglu.shape[0]
    consts = [gmix, sp['a_re'], sp['a_im'], sp['ap_re'], sp['ap_im'], sp['b_tiles'], sp['c_re'], sp['nc_im'],
              dskip.reshape(d // LANES, 1, LANES), wglu, bglu.reshape(n_gtile, 1, MXU_DIM)]
    last = n * tiles - 1
    cur_seq = lambda g: (jnp.maximum(g - 1, 0) // tiles, 0, 0)
    out, sre, sim = pl.pallas_call(
        functools.partial(_ssm_prompt_kernel, n_steps=n_steps, pitch=pitch, seq_tiles=tiles),
        grid=(n * tiles + 1,),
        in_specs=[pl.BlockSpec((rows, d), lambda g: (jnp.minimum(g, last), 0))]
        + [_const_spec(c.shape) for c in consts],
        out_specs=[pl.BlockSpec((rows, d), lambda g: (jnp.maximum(g - 1, 0), 0)),
                   pl.BlockSpec((None, 1, nst), cur_seq),
                   pl.BlockSpec((None, 1, nst), cur_seq)],
        out_shape=[jax.ShapeDtypeStruct((n * l, d), F32),
                   jax.ShapeDtypeStruct((n, 1, nst), F32),
                   jax.ShapeDtypeStruct((n, 1, nst), F32)],
        scratch_shapes=[pltpu.VMEM((d // LANES, SUBLANES * pitch, LANES), F32),
                        pltpu.VMEM((2, rows, d), F32),
                        pltpu.VMEM((2, rows, d), BF16),
                        pltpu.VMEM((2, d // LANES, rows, LANES), F32),
                        pltpu.VMEM((2, n_piece, rows, MXU_DIM), F32),
                        pltpu.VMEM((2, n_piece, rows, MXU_DIM), F32),
                        pltpu.VMEM((d // LANES, rows, LANES), BF16),
                        pltpu.VMEM((n_gtile, rows, MXU_DIM), F32),
                        pltpu.VMEM((2, SUBLANES, nst), F32),
                        pltpu.VMEM((2, SUBLANES, nst), F32),
                        pltpu.VMEM((2, 1, nst), F32)],
        compiler_params=pltpu.CompilerParams(dimension_semantics=("arbitrary",),
                                             vmem_limit_bytes=VMEM_LIMIT),
        name="ssm_prompt",
    )(x.reshape(n * l, d), *consts)
    return out, sre[:, 0], sim[:, 0]


def _ssm_sample(x_tm, h0re, h0im, sp, gmix, dskip, wglu, bglu, n_seq, n_steps):
    rows, d = x_tm.shape
    nst = sp['a_re'].shape[1]
    args = [x_tm, h0re, h0im, gmix, sp['a_re'], sp['a_im'], sp['b_tiles'], sp['c_re'], sp['nc_im'],
            dskip, wglu, bglu]
    return pl.pallas_call(
        functools.partial(_ssm_sample_kernel, n_seq=n_seq, n_steps=n_steps),
        grid=(1,),
        in_specs=[_const_spec(a.shape) for a in args],
        out_specs=[pl.BlockSpec((rows, d), lambda i: (0, 0)), pl.BlockSpec((n_seq, nst), lambda i: (0, 0)),
                   pl.BlockSpec((n_seq, nst), lambda i: (0, 0))],
        out_shape=[jax.ShapeDtypeStruct((rows, d), F32),
                   jax.ShapeDtypeStruct((n_seq, nst), F32),
                   jax.ShapeDtypeStruct((n_seq, nst), F32)],
        scratch_shapes=[pltpu.VMEM((rows, nst), F32), pltpu.VMEM((rows, nst), F32)],
        compiler_params=pltpu.CompilerParams(dimension_semantics=("arbitrary",),
                                             vmem_limit_bytes=VMEM_LIMIT),
        name="ssm_sample",
    )(*args)


def _ffn_kernel(*refs, rows, head_rows, shift, has_attn):
    if has_attn:
        h_ref, o_ref, wo_ref = refs[:3]
        refs = refs[3:]
    else:
        h_ref = refs[0]
        refs = refs[1:]
    (p_ref, head_ref, gffn_ref, wup_ref, cw_ref, cb_ref, wdown_ref, gple_ref, wgate_ref, win_ref,
     out_ref, tail_ref, acc, xn_ref, ext_a, ext_b) = refs
    ff = wdown_ref.shape[0]
    n_chunk = ff // FF_CHUNK
    up0 = head_rows
    up1 = head_rows + rows

    @pl.when(pl.program_id(1) == 0)
    def _():
        tail_ref[...] = head_ref[...]

    x = h_ref[...]
    if has_attn:
        x = x + _dot(o_ref[...], wo_ref[...])
    xn_ref[...] = _rms(x, gffn_ref[...]).astype(BF16)
    acc[...] = jnp.zeros_like(acc)

    n_slab = ext_a.shape[1]

    def slab_cols(c, half, s):
        c0 = half * ff + c * FF_CHUNK + s * LANES
        return slice(c0, c0 + LANES)

    def up_project(c, ext):
        for half in range(2):
            c0 = half * ff + c * FF_CHUNK
            up = _dot(xn_ref[...], wup_ref[:, c0:c0 + FF_CHUNK])
            for s in range(n_slab):
                ext[half, s, 0:up0, :] = tail_ref[:, slab_cols(c, half, s)]
                ext[half, s, up0:up1, :] = up[:, s * LANES:(s + 1) * LANES]

    def conv(ext, half, c):
        parts = []
        for s in range(n_slab):
            ls = slab_cols(c, half, s)
            t2 = ext[half, s, up0 - 2 * shift:up1 - 2 * shift, :]
            t1 = ext[half, s, up0 - shift:up1 - shift, :]
            tail_ref[:, ls] = ext[half, s, rows:up1, :]
            r = cb_ref[:, ls] + t2 * cw_ref[0:1, ls]
            r = r + t1 * cw_ref[1:2, ls]
            parts.append(r + ext[half, s, up0:up1, :] * cw_ref[2:3, ls])
        return jnp.concatenate(parts, axis=1)

    def down_project(c, ext):
        cg = conv(ext, 0, c)
        cv = conv(ext, 1, c)
        act = (cg * jax.nn.sigmoid(cg) * cv).astype(BF16)
        acc[...] += _dot(act, wdown_ref[c * FF_CHUNK:(c + 1) * FF_CHUNK, :])

    bufs = (ext_a, ext_b)
    up_project(0, bufs[0])
    for c in range(n_chunk):
        if c + 1 < n_chunk:
            up_project(c + 1, bufs[(c + 1) % 2])
        down_project(c, bufs[c % 2])

    h2 = x + acc[...]
    gate = jax.nn.sigmoid(_dot(_rms(h2, gple_ref[...]).astype(BF16), wgate_ref[...]))
    pe = _dot(p_ref[...].astype(BF16), win_ref[...])
    out_ref[...] = h2 + pe * gate


def _ffn_weights(g_ffn, w_up, conv_w, conv_b, w_down, g_ple, w_gate, w_in):
    depth, d, ff2 = w_up.shape
    return dict(
        g_ffn=g_ffn.reshape(depth, 1, d),
        w_up=w_up.astype(BF16),
        conv_w=conv_w,
        conv_b=conv_b.reshape(depth, 1, ff2),
        w_down=w_down.astype(BF16),
        g_ple=g_ple.reshape(depth, 1, d),
        w_gate=w_gate.astype(BF16),
        w_in=w_in.astype(BF16),
    )


def _layer_spec(shape, layer):
    nd = len(shape)
    return pl.BlockSpec((None,) + tuple(shape[1:]), lambda *_: (layer,) + (0,) * (nd - 1),
                        pipeline_mode=pl.Buffered(1))


def _ffn(h, p, layer, head, head_off, fw, rows, n_seq_tiles, shift, attn=None):
    total, d = h.shape
    tiles = total // rows // n_seq_tiles
    head_rows, ff2 = head.shape[1], head.shape[2]
    row_map = lambda i, t: (i * tiles + t, 0)
    p_map = lambda i, t: (layer * (total // rows) + i * tiles + t, 0)
    consts = [fw['g_ffn'], fw['w_up'], fw['conv_w'], fw['conv_b'], fw['w_down'], fw['g_ple'], fw['w_gate'],
              fw['w_in']]
    args = [h]
    in_specs = [pl.BlockSpec((rows, d), row_map)]
    if attn is not None:
        o, wo = attn
        args += [o, wo]
        in_specs += [pl.BlockSpec((rows, o.shape[1]), row_map), _const_spec(wo.shape)]
    args += [p, head] + consts
    in_specs += [pl.BlockSpec((rows, p.shape[1]), p_map),
                 pl.BlockSpec((None, head_rows, ff2), lambda i, t: (head_off + i, 0, 0))]
    in_specs += [_layer_spec(c.shape, layer) for c in consts]
    out, tail = pl.pallas_call(
        functools.partial(_ffn_kernel, rows=rows, head_rows=head_rows, shift=shift, has_attn=attn is not None),
        grid=(n_seq_tiles, tiles),
        in_specs=in_specs,
        out_specs=[pl.BlockSpec((rows, d), row_map),
                   pl.BlockSpec((None, head_rows, ff2), lambda i, t: (i, 0, 0))],
        out_shape=[jax.ShapeDtypeStruct((total, d), F32),
                   jax.ShapeDtypeStruct((n_seq_tiles, head_rows, ff2), F32)],
        scratch_shapes=[pltpu.VMEM((rows, d), F32),
                        pltpu.VMEM((rows, d), BF16),
                        pltpu.VMEM((2, FF_CHUNK // LANES, head_rows + rows, LANES), F32),
                        pltpu.VMEM((2, FF_CHUNK // LANES, head_rows + rows, LANES), F32)],
        compiler_params=pltpu.CompilerParams(dimension_semantics=("arbitrary", "arbitrary"),
                                             vmem_limit_bytes=VMEM_LIMIT),
        name="ffn_ple",
    )(*args)
    return out, tail


def _pair_spread(x, fill):
    assert x.shape[1] == 2 * LANES and HEAD_DIM * 2 == LANES
    r = pltpu.roll(x, HEAD_DIM, 1)
    low = lax.broadcasted_iota(jnp.int32, (x.shape[0], LANES), 1) < HEAD_DIM
    x01, x23 = x[:, :LANES], x[:, LANES:]
    r30, r12 = r[:, :LANES], r[:, LANES:]
    first = lambda t: jnp.where(low, t, fill)
    second = lambda t: jnp.where(low, fill, t)
    return jnp.concatenate([first(x01), second(r30), first(r12), second(x01),
                            first(x23), second(r12), first(r30), second(x23)], axis=1)


def _qkv_kernel(*refs, paired):
    (h_ref, gkv_ref, gmix_ref, wk_ref, wv_ref, wq_ref, knorm_ref, qnorm_ref, kpool_ref, kspread_ref,
     qpool_ref, qspread_ref, q_ref, k_ref, v_ref) = refs[:15]
    x = h_ref[...]
    s = _rms(x, gkv_ref[...]).astype(BF16)
    k = _head_rms(_dot(s, wk_ref[...]), kpool_ref[...], kspread_ref[...], knorm_ref[...])
    v = _dot(s, wv_ref[...])
    xn = _rms(x, gmix_ref[...]).astype(BF16)
    q = _head_rms(_dot(xn, wq_ref[...]), qpool_ref[...], qspread_ref[...], qnorm_ref[...])
    if paired:
        kx_ref, vx_ref = refs[15:]
        kx_ref[...] = _pair_spread(k, 0.0).astype(BF16)
        vx_ref[...] = _pair_spread(v, 1.0).astype(BF16)
    k_ref[...] = k
    v_ref[...] = v
    q_ref[...] = (q * (HEAD_DIM ** -0.5)).astype(q_ref.dtype)


def _head_pool_matrices(n):
    member = (jnp.arange(n)[:, None] // HEAD_DIM) == jnp.arange(LANES)[None, :]
    return member.astype(BF16) * (1.0 / HEAD_DIM), member.T.astype(BF16)


def _qkv(h, qw, rows, paired):
    total, d = h.shape
    hk = qw['w_k'].shape[1]
    hq = qw['w_q'].shape[1]
    consts = [qw['g_kv'], qw['g_mix'], qw['w_k'], qw['w_v'], qw['w_q'], qw['k_norm'], qw['q_norm'],
              *qw['k_pool'], *qw['q_pool']]
    row_map = lambda i: (i, 0)
    out_specs = [pl.BlockSpec((rows, hq), row_map), pl.BlockSpec((rows, hk), row_map),
                 pl.BlockSpec((rows, hk), row_map)]
    out_shape = [jax.ShapeDtypeStruct((total, hq), BF16),
                 jax.ShapeDtypeStruct((total, hk), F32),
                 jax.ShapeDtypeStruct((total, hk), F32)]
    if paired:
        out_specs += [pl.BlockSpec((rows, 4 * hk), row_map)] * 2
        out_shape += [jax.ShapeDtypeStruct((total, 4 * hk), BF16)] * 2
    return pl.pallas_call(
        functools.partial(_qkv_kernel, paired=paired),
        grid=(total // rows,),
        in_specs=[pl.BlockSpec((rows, d), row_map)] + [_const_spec(c.shape) for c in consts],
        out_specs=out_specs,
        out_shape=out_shape,
        compiler_params=pltpu.CompilerParams(dimension_semantics=("arbitrary",),
                                             vmem_limit_bytes=VMEM_LIMIT),
        name="qkv",
    )(h, *consts)


def _t5_bucket(dist):
    max_exact = NUM_BUCKETS // 2
    df = jnp.maximum(dist, 1).astype(F32)
    large = max_exact + (jnp.log(df / max_exact) / math.log(MAX_DISTANCE / max_exact)
                         * (NUM_BUCKETS - max_exact)).astype(jnp.int32)
    return jnp.where(dist < max_exact, dist, jnp.minimum(large, NUM_BUCKETS - 1))


def _bias_table(rel_bias, lb, qb, lk):
    dist = lb + jnp.arange(qb)[:, None] - jnp.arange(lk)[None, :]
    per_dist = rel_bias[_t5_bucket(jnp.arange(WINDOW + 1))].astype(F32)
    onehot = (jnp.clip(dist, 0, WINDOW)[..., None] == jnp.arange(WINDOW + 1)).astype(F32)
    bias = jnp.einsum('qkd,dh->hqk', onehot, per_dist, precision=lax.Precision.HIGHEST)
    ok = (dist >= 0) & (dist <= WINDOW)
    return jnp.where(ok[None], bias, -jnp.inf)


def _attn_prompt_kernel(sink_ref, q_ref, kx_ref, vx_ref, bias_ref, o_ref, kx_prev, vx_prev):
    qb = kx_prev.shape[0]
    n_blocks = q_ref.shape[0] // qb
    n_heads = bias_ref.shape[1]
    rep = n_heads // N_KV_HEADS
    grp = 4 * HEAD_DIM
    step = pl.program_id(1)

    @pl.when(step == 0)
    def _():
        kx_prev[...] = jnp.zeros_like(kx_prev)
        vx_prev[...] = jnp.zeros_like(vx_prev)

    first_table = jnp.minimum(step, 1)
    low = lax.broadcasted_iota(jnp.int32, (qb, LANES), 1) < HEAD_DIM
    for g in range(N_KV_HEADS):
        gs = slice(g * grp, (g + 1) * grp)
        kx_all = jnp.concatenate([kx_prev[:, gs], kx_ref[:, gs]], axis=0)
        vx_all = jnp.concatenate([vx_prev[:, gs], vx_ref[:, gs]], axis=0)
        for blk in range(n_blocks):
            rows = slice(blk * qb, (blk + 1) * qb)
            kx = kx_all[blk * qb:(blk + 2) * qb]
            vx = vx_all[blk * qb:(blk + 2) * qb]
            table = first_table if blk == 0 else 1
            for pair in range(rep // 2):
                h0 = g * rep + 2 * pair
                ls = slice(h0 * HEAD_DIM, h0 * HEAD_DIM + LANES)
                qp = q_ref[rows, ls]
                res = []
                for h, kh in ((h0, kx[:, :LANES]), (h0 + 1, kx[:, LANES:])):
                    s = _dot_nt(qp, kh) + bias_ref[table, h]
                    m = jnp.maximum(jnp.max(s, axis=-1, keepdims=True), sink_ref[h])
                    res.append((_dot(jnp.exp(s - m).astype(BF16), vx), jnp.exp(sink_ref[h] - m)))
                (ra, ea), (rb, eb) = res
                num = jnp.where(low, ra[:, :LANES], rb[:, LANES:])
                den = jnp.where(low, ra[:, LANES:], rb[:, :LANES]) + jnp.where(low, ea, eb)
                o_ref[rows, ls] = (num / den).astype(o_ref.dtype)
    last = slice((n_blocks - 1) * qb, n_blocks * qb)
    kx_prev[...] = kx_ref[last, :]
    vx_prev[...] = vx_ref[last, :]


def _attn_prompt(q, kx, vx, bias, sinks, n, l):
    hq = q.shape[1]
    wx = kx.shape[1]
    qb = WINDOW
    rows = next(m * qb for m in (4, 2, 1) if l % (m * qb) == 0)
    steps = l // rows
    cur = lambda i, b, *_: (i * steps + b, 0)
    return pl.pallas_call(
        _attn_prompt_kernel,
        grid_spec=pltpu.PrefetchScalarGridSpec(
            num_scalar_prefetch=1,
            grid=(n, steps),
            in_specs=[pl.BlockSpec((rows, hq), cur), pl.BlockSpec((rows, wx), cur), pl.BlockSpec((rows, wx), cur),
                      pl.BlockSpec(bias.shape, lambda i, b, *_: (0, 0, 0, 0), pipeline_mode=pl.Buffered(1))],
            out_specs=pl.BlockSpec((rows, hq), cur),
            scratch_shapes=[pltpu.VMEM((qb, wx), BF16), pltpu.VMEM((qb, wx), BF16)]),
        out_shape=jax.ShapeDtypeStruct((n * l, hq), BF16),
        compiler_params=pltpu.CompilerParams(dimension_semantics=("arbitrary", "arbitrary"),
                                             vmem_limit_bytes=VMEM_LIMIT),
        name="attn_prompt",
    )(sinks, q, kx, vx, bias)


def _attn_sample_kernel(sink_ref, q_ref, kc_ref, vc_ref, kn_ref, vn_ref, bias_ref,
                        o_ref, kwin_ref, vwin_ref, *, n_new):
    lb = kc_ref.shape[1]
    new_rows = kn_ref.shape[1]
    rows = q_ref.shape[1]
    hk = kc_ref.shape[2]
    grp = rows // N_KV_HEADS
    zero_rows = jnp.zeros((bias_ref.shape[1] - lb - new_rows, hk), BF16)
    lane_head = lax.broadcasted_iota(jnp.int32, (grp, hk), 1) // HEAD_DIM
    samples = range(q_ref.shape[0])
    scores = []
    for s_i in samples:
        kk = jnp.concatenate([kc_ref[s_i].astype(BF16), kn_ref[s_i].astype(BF16), zero_rows], axis=0)
        scores.append(_dot_nt(q_ref[s_i], kk) + bias_ref[...])
    probs = []
    for s in scores:
        m = jnp.maximum(jnp.max(s, axis=-1, keepdims=True), sink_ref[...])
        pr = jnp.exp(s - m)
        probs.append((pr.astype(BF16), jnp.sum(pr, axis=-1, keepdims=True) + jnp.exp(sink_ref[...] - m)))
    for s_i, (pr, den) in zip(samples, probs):
        vv = jnp.concatenate([vc_ref[s_i].astype(BF16), vn_ref[s_i].astype(BF16), zero_rows], axis=0)
        pv = _dot(pr, vv) / den
        o = jnp.zeros((grp, hk), F32)
        for g in range(N_KV_HEADS):
            o = o + jnp.where(lane_head == g, pv[g * grp:(g + 1) * grp, :], 0.0)
        o_ref[s_i] = o
    for s_i in samples:
        kwin_ref[s_i, 0:lb - n_new, :] = kc_ref[s_i, n_new:lb, :]
        kwin_ref[s_i, lb - n_new:lb, :] = kn_ref[s_i, 0:n_new, :]
        vwin_ref[s_i, 0:lb - n_new, :] = vc_ref[s_i, n_new:lb, :]
        vwin_ref[s_i, lb - n_new:lb, :] = vn_ref[s_i, 0:n_new, :]


def _attn_sample(q_blk, kc, vc, kn, vn, bias, sink, n_new, block):
    ns, rows, hk = q_blk.shape
    lb = kc.shape[1]
    lkp = bias.shape[1]
    grp = rows // N_KV_HEADS
    per_s = lambda i: (i, 0, 0)
    return pl.pallas_call(
        functools.partial(_attn_sample_kernel, n_new=n_new),
        grid=(ns // block,),
        in_specs=[_const_spec(sink.shape),
                  pl.BlockSpec((block, rows, hk), per_s),
                  pl.BlockSpec((block, lb, hk), per_s), pl.BlockSpec((block, lb, hk), per_s),
                  pl.BlockSpec((block,) + kn.shape[1:], per_s), pl.BlockSpec((block,) + vn.shape[1:], per_s),
                  _const_spec(bias.shape)],
        out_specs=[pl.BlockSpec((block, grp, hk), per_s),
                   pl.BlockSpec((block, lb, hk), per_s), pl.BlockSpec((block, lb, hk), per_s)],
        out_shape=[jax.ShapeDtypeStruct((ns, grp, hk), F32),
                   jax.ShapeDtypeStruct((ns, lb, hk), F32),
                   jax.ShapeDtypeStruct((ns, lb, hk), F32)],
        compiler_params=pltpu.CompilerParams(dimension_semantics=("arbitrary",),
                                             vmem_limit_bytes=VMEM_LIMIT),
        name="attn_sample",
    )(sink, q_blk, kc, vc, kn, vn, bias)


def kernel(x_prompt, x_sample, state_ssm_re, state_ssm_im, state_ffn_conv, cache_k_win, cache_v_win, p_prompt, p_sample, g_mix, g_ffn, g_ple, ssm_lam_re, ssm_lam_im, ssm_log_dt, ssm_b_re, ssm_b_im, ssm_c_re, ssm_c_im, ssm_d, w_glu, b_glu, g_kv, w_k, w_v, k_norm, w_q, q_norm, sinks, w_o, rel_bias, w_up, conv_w, conv_b, w_down, w_ple_in, w_ple_gate):
    n, l, d = x_prompt.shape
    ns, ls, _ = x_sample.shape
    n_groups, n_state = ssm_lam_re.shape[1:]
    nst = n_groups * n_state
    ff2 = w_up.shape[2]
    hk = w_k.shape[1]
    hq = w_q.shape[2]
    n_heads = hq // HEAD_DIM
    rep = n_heads // N_KV_HEADS
    lb = cache_k_win.shape[1]

    ssm_steps = min(32, l // SUBLANES)
    ffn_rows = min(512, l)
    head_rows_p = SUBLANES
    tm = lambda a: jnp.swapaxes(a, 0, 1)

    sp = _ssm_params(ssm_lam_re[0], ssm_lam_im[0], ssm_log_dt[0], ssm_b_re[0], ssm_b_im[0], ssm_c_re[0],
                     ssm_c_im[0], ssm_steps)
    gmix0 = g_mix[0].reshape(1, d)
    dskip = ssm_d[0].reshape(1, d)
    wglu = jnp.transpose(w_glu[0].astype(BF16).reshape(d, 2 * d // MXU_DIM, MXU_DIM), (1, 0, 2))
    bglu = b_glu[0].reshape(1, 2 * d)
    hp, sre_p, sim_p = _ssm_prompt(x_prompt, sp, gmix0, dskip, wglu, bglu, ssm_steps)
    xs_tm = tm(x_sample).reshape(ls * ns, d)
    hs, sre_s, sim_s = _ssm_sample(xs_tm, state_ssm_re[0].reshape(ns, nst), state_ssm_im[0].reshape(ns, nst),
                                   sp, gmix0, dskip, wglu, bglu, ns, ls)

    pp = p_prompt.reshape(p_prompt.shape[0] * n * l, -1)
    ps = jnp.swapaxes(p_sample, 1, 2).reshape(p_sample.shape[0] * ls * ns, -1)
    zero_head = jnp.zeros((n, head_rows_p, ff2), F32)
    conv_s_tm = jnp.swapaxes(state_ffn_conv, 1, 2).reshape(state_ffn_conv.shape[0], (CONV_WIDTH - 1) * ns, ff2)

    fw = _ffn_weights(g_ffn, w_up, conv_w, conv_b, w_down, g_ple, w_ple_gate, w_ple_in)

    def ffn_layer(i, hp, hs, attn_p=None, attn_s=None):
        hp, tail_p = _ffn(hp, pp, i, zero_head, 0, fw, ffn_rows, n, 1, attn_p)
        hs, tail_s = _ffn(hs, ps, i, conv_s_tm, i, fw, ls * ns, 1, ns, attn_s)
        conv_p = tail_p[:, head_rows_p - (CONV_WIDTH - 1):, :]
        return hp, hs, conv_p, tail_s.reshape(CONV_WIDTH - 1, ns, ff2)

    hp, hs, conv_p0, conv_s0 = ffn_layer(0, hp, hs)

    qw = dict(g_kv=g_kv.reshape(1, d), g_mix=g_mix[1].reshape(1, d), w_k=w_k.astype(BF16), w_v=w_v.astype(BF16),
              w_q=w_q[0].astype(BF16), k_norm=jnp.tile(k_norm, hk // HEAD_DIM).reshape(1, hk),
              q_norm=jnp.tile(q_norm[0], n_heads).reshape(1, hq),
              k_pool=_head_pool_matrices(hk), q_pool=_head_pool_matrices(hq))
    qkv_rows = 2 * ffn_rows if (n * l) % (2 * ffn_rows) == 0 else ffn_rows
    q_p, k_p, v_p, kx_p, vx_p = _qkv(hp, qw, qkv_rows, True)
    q_s, k_s, v_s = _qkv(hs, qw, ls * ns, False)

    wo = w_o[0].astype(BF16)
    bias_p = _bias_table(rel_bias, WINDOW, WINDOW, 2 * WINDOW)
    bias_first = jnp.where(jnp.arange(2 * WINDOW) >= WINDOW, bias_p, -jnp.inf)
    o_p = _attn_prompt(q_p, kx_p, vx_p, jnp.stack([bias_first, bias_p]), sinks[0].astype(F32), n, l)

    lkp = 2 * WINDOW
    new_rows = 16
    pad_new = lambda a: jnp.pad(tm(a.reshape(ls, ns, hk)), ((0, 0), (0, new_rows - ls), (0, 0)))
    kc = cache_k_win.reshape(ns, lb, hk)
    vc = cache_v_win.reshape(ns, lb, hk)
    bias_s = _bias_table(rel_bias, lb, ls, lb + ls)
    bias_s = jnp.pad(bias_s, ((0, 0), (0, 0), (0, lkp - lb - ls)), constant_values=-jnp.inf)
    bias_s = bias_s.reshape(n_heads * ls, lkp)
    sink_s = jnp.repeat(sinks[0].astype(F32), ls).reshape(n_heads * ls, 1)
    q5 = jnp.transpose(q_s.reshape(ls, ns, N_KV_HEADS, rep, HEAD_DIM), (1, 2, 3, 0, 4))
    q_blk = jnp.einsum('sgrtd,gh->sgrthd', q5, jnp.eye(N_KV_HEADS, dtype=q5.dtype))
    q_blk = q_blk.reshape(ns, n_heads * ls, hk)
    o4, k_win_s, v_win_s = _attn_sample(q_blk, kc, vc, pad_new(k_s), pad_new(v_s), bias_s, sink_s, ls, 8)
    o_s = jnp.transpose(o4.reshape(ns, rep, ls, N_KV_HEADS, HEAD_DIM), (2, 0, 3, 1, 4)).reshape(ls * ns, hq)
    o_s = o_s.astype(BF16)

    hp, hs, conv_p1, conv_s1 = ffn_layer(1, hp, hs, (o_p, wo), (o_s, wo))

    y_prompt = hp.reshape(n, l, d)
    y_sample = tm(hs.reshape(ls, ns, d))
    ssm_shape = (1, -1, n_groups, n_state)
    kvh_shape = (-1, lb, N_KV_HEADS, HEAD_DIM)
    k_win_p = k_p.reshape(n, l, hk)[:, l - WINDOW:].reshape(n, WINDOW, N_KV_HEADS, HEAD_DIM)
    v_win_p = v_p.reshape(n, l, hk)[:, l - WINDOW:].reshape(n, WINDOW, N_KV_HEADS, HEAD_DIM)
    return (y_prompt, y_sample,
            sre_p.reshape(ssm_shape), sim_p.reshape(ssm_shape),
            sre_s.reshape(ssm_shape), sim_s.reshape(ssm_shape),
            jnp.stack([conv_p0, conv_p1]), jnp.swapaxes(jnp.stack([conv_s0, conv_s1]), 1, 2),
            k_win_p, v_win_p, k_win_s.reshape(kvh_shape), v_win_s.reshape(kvh_shape))
```

```python
import functools
import math

import jax
import jax.numpy as jnp
from jax import lax
from jax.experimental import pallas as pl
from jax.experimental.pallas import tpu as pltpu

F32 = jnp.float32
BF16 = jnp.bfloat16

EPS = 1e-6
HEAD_DIM = 64
N_KV_HEADS = 4
WINDOW = 128
NUM_BUCKETS = 32
MAX_DISTANCE = 128
CONV_WIDTH = 3

LANES = 128
SUBLANES = 8
MXU_DIM = 256
VMEM_LIMIT = 56 * 1024 * 1024

SCAN_COLS = 1024
FF_CHUNK = 256


def _dot(a, b):
    return jnp.dot(a, b, preferred_element_type=F32)


def _dot_nt(a, b):
    return lax.dot_general(a, b, (((1,), (1,)), ((), ())), preferred_element_type=F32)


def _rms(x, g):
    ms = jnp.mean(x * x, axis=-1, keepdims=True)
    return x * lax.rsqrt(ms + EPS) * g


def _head_rms(x, pool, spread, g):
    ms = _dot((x * x).astype(BF16), pool)
    scale = lax.rsqrt(ms + EPS)
    hi = scale.astype(BF16)
    lo = (scale - hi.astype(F32)).astype(BF16)
    return x * (_dot(hi, spread) + _dot(lo, spread)) * g


def _const_spec(shape):
    nd = len(shape)
    return pl.BlockSpec(shape, lambda *_: (0,) * nd, pipeline_mode=pl.Buffered(1))


def _b_project(ub, bw_ref, bure, buim):
    n_piece, kw, nw = bw_ref.shape[1:]
    per_kt = n_piece * kw // ub.shape[1]
    for q in range(n_piece):
        lhs = ub[:, (q // per_kt) * kw:(q // per_kt + 1) * kw]
        bure[:, q * nw:(q + 1) * nw] = _dot(lhs, bw_ref[0, q])
        buim[:, q * nw:(q + 1) * nw] = _dot(lhs, bw_ref[1, q])


def _scan(bure, buim, are_ref, aim_ref, n_seq, n_steps, init_fn, final_fn):
    nst = bure.shape[1]
    for cb in range(nst // SCAN_COLS):
        cs = slice(cb * SCAN_COLS, (cb + 1) * SCAN_COLS)
        ar = jnp.broadcast_to(are_ref[:, cs], (SUBLANES, SCAN_COLS))
        ai = jnp.broadcast_to(aim_ref[:, cs], (SUBLANES, SCAN_COLS))

        def group(g, _, cs=cs, ar=ar, ai=ai):
            r0 = pl.multiple_of(g * SUBLANES, SUBLANES)

            def step(k, carry):
                hr, hi = carry
                row = pl.multiple_of(k * n_seq + r0, SUBLANES)
                br = bure[pl.ds(row, SUBLANES), cs]
                bi = buim[pl.ds(row, SUBLANES), cs]
                nr = ar * hr - ai * hi + br
                ni = ar * hi + ai * hr + bi
                bure[pl.ds(row, SUBLANES), cs] = nr
                buim[pl.ds(row, SUBLANES), cs] = ni
                return nr, ni

            hr, hi = lax.fori_loop(0, n_steps, step, init_fn(r0, cs), unroll=min(n_steps, 4))
            final_fn(r0, cs, hr, hi)
            return 0

        lax.fori_loop(0, n_seq // SUBLANES, group, 0, unroll=4)


def _c_project_glu(x, u, bure, buim, cre_ref, ncim_ref, dskip_ref, wglu_ref, bglu_ref):
    d = x.shape[1]
    n_blk = cre_ref.shape[0]
    kw = cre_ref.shape[1]
    ys = []
    for m in range(n_blk):
        hr = bure[:, m * kw:(m + 1) * kw].astype(BF16)
        hi = buim[:, m * kw:(m + 1) * kw].astype(BF16)
        ys.append(_dot(hr, cre_ref[m]) + _dot(hi, ncim_ref[m]))
    y = jnp.concatenate(ys, axis=1) + dskip_ref[...] * u
    z = jax.nn.gelu(y).astype(BF16)
    gl = jnp.concatenate([_dot(z, wglu_ref[nt]) for nt in range(wglu_ref.shape[0])], axis=1) + bglu_ref[...]
    return x + gl[:, :d] * jax.nn.sigmoid(gl[:, d:])


def _ssm_prompt_kernel(x_ref, gmix_ref, are_ref, aim_ref, apw_re_ref, apw_im_ref, bw_ref, cre_ref, ncim_ref,
                       dskip_ref, wglu_ref, bglu_ref,
                       out_ref, sre_ref, sim_ref,
                       slab, xp, ub, us, bu_a, bu_b, zb, gl, ends, hin, car, *, n_steps, pitch, seq_tiles):
    g_step = pl.program_id(0)
    for parity, (bu_next, bu_cur) in enumerate(((bu_a, bu_b), (bu_b, bu_a))):
        pl.when(g_step % 2 == parity)(functools.partial(
            _ssm_prompt_step, parity, g_step, x_ref, gmix_ref, are_ref, aim_ref, apw_re_ref, apw_im_ref, bw_ref,
            cre_ref, ncim_ref, dskip_ref, wglu_ref, bglu_ref, out_ref, sre_ref, sim_ref,
            slab, xp, ub, us, bu_next, bu_cur, zb, gl, ends, hin, car, n_steps, pitch, seq_tiles))


def _ssm_prompt_step(nxt, g_step, x_ref, gmix_ref, are_ref, aim_ref, apw_re_ref, apw_im_ref, bw_ref, cre_ref,
                     ncim_ref, dskip_ref, wglu_ref, bglu_ref, out_ref, sre_ref, sim_ref,
                     slab, xp, ub, us, bu_next, bu_cur, zb, gl, ends, hin, car, n_steps, pitch, seq_tiles):
    cur = 1 - nxt
    n_slab = slab.shape[0]
    d = xp.shape[2]
    n_piece = bu_cur.shape[1]
    blk_pieces = SCAN_COLS // MXU_DIM
    n_blk = n_piece // blk_pieces
    trips = 4
    spt = n_steps // trips
    n_cblk = cre_ref.shape[0]
    n_gtile = wglu_ref.shape[0]
    assert 2 * n_piece == n_blk * trips * 2
    assert 2 * n_gtile == n_blk * trips
    assert bw_ref.shape[2] == MXU_DIM and 2 * n_cblk == n_piece

    def lanes(q):
        return slice(q * MXU_DIM, (q + 1) * MXU_DIM)

    for j in range(SUBLANES):
        for c in range(n_slab):
            slab[c, j * pitch:j * pitch + n_steps, :] = x_ref[j * n_steps:(j + 1) * n_steps,
                                                             c * LANES:(c + 1) * LANES]

    def gather(k, _):
        r0 = pl.multiple_of(k * SUBLANES, SUBLANES)
        for c in range(n_slab):
            xp[nxt, pl.ds(r0, SUBLANES), c * LANES:(c + 1) * LANES] = slab[c, pl.ds(k, SUBLANES, stride=pitch), :]
        return 0

    lax.fori_loop(0, n_steps, gather, 0, unroll=True)
    u = _rms(xp[nxt], gmix_ref[...])
    ub[nxt] = u.astype(BF16)
    for c in range(n_slab):
        us[nxt, c] = u[:, c * LANES:(c + 1) * LANES]

    def run_pass(bu, store, init_fn, end_fn, work_fn):
        for blk in range(n_blk):
            coef = [(jnp.broadcast_to(are_ref[:, lanes(blk * blk_pieces + nt)], (SUBLANES, MXU_DIM)),
                     jnp.broadcast_to(aim_ref[:, lanes(blk * blk_pieces + nt)], (SUBLANES, MXU_DIM)))
                    for nt in range(blk_pieces)]

            def trip(i, state, blk=blk, coef=coef):
                if work_fn is not None:
                    work_fn(blk, i)
                state = list(state)
                for s in range(spt):
                    row = (i * spt + s) * SUBLANES
                    for nt in range(blk_pieces):
                        q = blk * blk_pieces + nt
                        ar, ai = coef[nt]
                        hr, hi = state[2 * nt], state[2 * nt + 1]
                        nr = ar * hr - ai * hi + bu[0, q, pl.ds(row, SUBLANES), :]
                        ni = ar * hi + ai * hr + bu[1, q, pl.ds(row, SUBLANES), :]
                        if store:
                            bu[0, q, pl.ds(row, SUBLANES), :] = nr
                            bu[1, q, pl.ds(row, SUBLANES), :] = ni
                        state[2 * nt], state[2 * nt + 1] = nr, ni
                return tuple(state)

            state = init_fn(blk)
            for i in range(trips):
                state = trip(i, state)
            end_fn(blk, state)

    def zero_init(blk):
        return tuple(jnp.zeros((SUBLANES, MXU_DIM), F32) for _ in range(2 * blk_pieces))

    def keep_ends(blk, state):
        for nt in range(blk_pieces):
            ends[0, :, lanes(blk * blk_pieces + nt)] = state[2 * nt]
            ends[1, :, lanes(blk * blk_pieces + nt)] = state[2 * nt + 1]

    def true_init(blk):
        return tuple(hin[ri, :, lanes(blk * blk_pieces + nt)] for nt in range(blk_pieces) for ri in range(2))

    tiles_per_kt = 2 * blk_pieces

    def b_project_tile(t):
        kt, j = divmod(t, tiles_per_kt)
        ri, nt = divmod(j, blk_pieces)
        q = kt * blk_pieces + nt
        bu_next[ri, q] = _dot(ub[nxt, :, lanes(kt)], bw_ref[ri, q])

    def b_project_slice(blk, i):
        b_project_tile(2 * (blk * trips + i))
        b_project_tile(2 * (blk * trips + i) + 1)

    def c_project_block(m):
        y = dskip_ref[m] * us[cur, m]
        for ri, c_ref in ((0, cre_ref), (1, ncim_ref)):
            for w in range(2):
                y = y + _dot(bu_cur[ri, 2 * m + w].astype(BF16), c_ref[m, w * MXU_DIM:(w + 1) * MXU_DIM, :])
        zb[m] = jax.nn.gelu(y).astype(BF16)

    def glu_slice(blk, i):
        t = blk * trips + i
        if t % 2 == 0:
            nt = t // 2
            z = jnp.concatenate([zb[c] for c in range(n_cblk)], axis=1)
            gl[nt] = _dot(z, wglu_ref[nt]) + bglu_ref[nt]

    def first_tile():
        car[...] = jnp.zeros_like(car)
        for blk in range(n_blk):
            lhs = ub[nxt, :, lanes(blk)]
            for ri in range(2):
                for nt in range(blk_pieces):
                    bu_next[ri, blk * blk_pieces + nt] = _dot(lhs, bw_ref[ri, blk * blk_pieces + nt])
        run_pass(bu_next, False, zero_init, keep_ends, None)

    if nxt == 0:
        pl.when(g_step == 0)(first_tile)

    @pl.when(g_step > 0)
    def _():
        fresh = (g_step - 1) % seq_tiles == 0
        hr = jnp.where(fresh, 0.0, car[0])
        hi = jnp.where(fresh, 0.0, car[1])
        apr = apw_re_ref[...]
        api = apw_im_ref[...]
        for j in range(SUBLANES):
            er = ends[0, j:j + 1, :]
            ei = ends[1, j:j + 1, :]
            hin[0, j:j + 1, :] = hr
            hin[1, j:j + 1, :] = hi
            hr, hi = apr * hr - api * hi + er, apr * hi + api * hr + ei
        car[0] = hr
        car[1] = hi
        sre_ref[...] = hr
        sim_ref[...] = hi

        run_pass(bu_cur, True, true_init, lambda *_: None, b_project_slice)

    @pl.when(g_step != 0)
    def _():
        for m in range(n_cblk):
            c_project_block(m)

    @pl.when(jnp.logical_and(g_step >= 1, pl.program_id(0) >= 0))
    def _():
        run_pass(bu_next, False, zero_init, keep_ends, glu_slice)

        half = n_gtile // 2
        for c in range(half):
            xp[cur, :, lanes(c)] = xp[cur, :, lanes(c)] + gl[c] * jax.nn.sigmoid(gl[half + c])

        def scatter(k, _):
            r0 = pl.multiple_of(k * SUBLANES, SUBLANES)
            for c in range(n_slab):
                slab[c, pl.ds(k, SUBLANES, stride=pitch), :] = xp[cur, pl.ds(r0, SUBLANES), c * LANES:(c + 1) * LANES]
            return 0

        lax.fori_loop(0, n_steps, scatter, 0, unroll=True)
        for j in range(SUBLANES):
            for c in range(n_slab):
                out_ref[j * n_steps:(j + 1) * n_steps, c * LANES:(c + 1) * LANES] = slab[c, j * pitch:j * pitch + n_steps, :]


def _ssm_sample_kernel(x_ref, h0re_ref, h0im_ref, gmix_ref, are_ref, aim_ref, bw_ref, cre_ref,
                       ncim_ref, dskip_ref, wglu_ref, bglu_ref,
                       out_ref, sre_ref, sim_ref, bure, buim, *, n_seq, n_steps):
    x = x_ref[...]
    u = _rms(x, gmix_ref[...])
    _b_project(u.astype(BF16), bw_ref, bure, buim)

    def init(r0, cs):
        return h0re_ref[pl.ds(r0, SUBLANES), cs], h0im_ref[pl.ds(r0, SUBLANES), cs]

    def final(r0, cs, hr, hi):
        sre_ref[pl.ds(r0, SUBLANES), cs] = hr
        sim_ref[pl.ds(r0, SUBLANES), cs] = hi

    _scan(bure, buim, are_ref, aim_ref, n_seq, n_steps, init, final)
    out_ref[...] = _c_project_glu(x, u, bure, buim, cre_ref, ncim_ref, dskip_ref, wglu_ref, bglu_ref)


def _ssm_params(lam_re, lam_im, log_dt, b_re, b_im, c_re, c_im, n_pow):
    g, p = lam_re.shape
    lr = lam_re.astype(F32)
    li = lam_im.astype(F32)
    dt = jnp.exp(log_dt.astype(F32))[:, None]
    mag = jnp.exp(lr * dt)
    ang = li * dt
    ab_re = mag * jnp.cos(ang)
    ab_im = mag * jnp.sin(ang)
    den = lr * lr + li * li
    nr = ab_re - 1.0
    f_re = (nr * lr + ab_im * li) / den
    f_im = (ab_im * lr - nr * li) / den
    br = b_re.astype(F32)
    bi = b_im.astype(F32)
    bb_re = f_re[..., None] * br - f_im[..., None] * bi
    bb_im = f_re[..., None] * bi + f_im[..., None] * br

    c = bb_re.shape[2]
    gk = MXU_DIM // c

    def b_tiles(bb):
        rows_kt = jnp.transpose(bb, (0, 2, 1)).reshape(g // gk, gk * c, p)
        per_tile = MXU_DIM // p
        wide = jnp.tile(rows_kt, (1, 1, per_tile))[:, None]
        row_group = jnp.arange(gk * c)[:, None] // c
        col_group = jnp.arange(MXU_DIM)[None, :] // p
        nt = jnp.arange(gk // per_tile)[:, None, None]
        own = (row_group[None] == nt * per_tile + col_group[None]).astype(F32)
        return (wide * own[None]).reshape(g * p // MXU_DIM, gk * c, MXU_DIM).astype(BF16)

    gc = LANES // c
    eye_c = jnp.eye(gc, dtype=F32)

    def c_blocks(cc):
        ct = jnp.transpose(cc.astype(F32), (0, 2, 1)).reshape(g // gc, gc, p, c)
        return jnp.einsum('tgpc,gh->tgphc', ct, eye_c).reshape(g // gc, gc * p, gc * c).astype(BF16)

    pw_re = ab_re.reshape(1, g * p)
    pw_im = ab_im.reshape(1, g * p)
    for _ in range(n_pow.bit_length() - 1):
        pw_re, pw_im = pw_re * pw_re - pw_im * pw_im, 2.0 * pw_re * pw_im
    return dict(a_re=ab_re.reshape(1, g * p), a_im=ab_im.reshape(1, g * p),
                ap_re=pw_re, ap_im=pw_im,
                b_tiles=jnp.stack([b_tiles(bb_re), b_tiles(bb_im)]),
                c_re=c_blocks(c_re), nc_im=c_blocks(-c_im))


def _ssm_prompt(x, sp, gmix, dskip, wglu, bglu, n_steps):
    n, l, d = x.shape
    nst = sp['a_re'].shape[1]
    rows = SUBLANES * n_steps
    pitch = n_steps + SUBLANES
    tiles = l // rows
    n_piece = nst // MXU_DIM
    n_gtile = wglu.shape[0]
    consts = [gmix, sp['a_re'], sp['a_im'], sp['ap_re'], sp['ap_im'], sp['b_tiles'], sp['c_re'], sp['nc_im'],
              dskip.reshape(d // LANES, 1, LANES), wglu, bglu.reshape(n_gtile, 1, MXU_DIM)]
    last = n * tiles - 1
    cur_seq = lambda g: (jnp.maximum(g - 1, 0) // tiles, 0, 0)
    out, sre, sim = pl.pallas_call(
        functools.partial(_ssm_prompt_kernel, n_steps=n_steps, pitch=pitch, seq_tiles=tiles),
        grid=(n * tiles + 1,),
        in_specs=[pl.BlockSpec((rows, d), lambda g: (jnp.minimum(g, last), 0))]
        + [_const_spec(c.shape) for c in consts],
        out_specs=[pl.BlockSpec((rows, d), lambda g: (jnp.maximum(g - 1, 0), 0)),
                   pl.BlockSpec((None, 1, nst), cur_seq),
                   pl.BlockSpec((None, 1, nst), cur_seq)],
        out_shape=[jax.ShapeDtypeStruct((n * l, d), F32),
                   jax.ShapeDtypeStruct((n, 1, nst), F32),
                   jax.ShapeDtypeStruct((n, 1, nst), F32)],
        scratch_shapes=[pltpu.VMEM((d // LANES, SUBLANES * pitch, LANES), F32),
                        pltpu.VMEM((2, rows, d), F32),
                        pltpu.VMEM((2, rows, d), BF16),
                        pltpu.VMEM((2, d // LANES, rows, LANES), F32),
                        pltpu.VMEM((2, n_piece, rows, MXU_DIM), F32),
                        pltpu.VMEM((2, n_piece, rows, MXU_DIM), F32),
                        pltpu.VMEM((d // LANES, rows, LANES), BF16),
                        pltpu.VMEM((n_gtile, rows, MXU_DIM), F32),
                        pltpu.VMEM((2, SUBLANES, nst), F32),
                        pltpu.VMEM((2, SUBLANES, nst), F32),
                        pltpu.VMEM((2, 1, nst), F32)],
        compiler_params=pltpu.CompilerParams(dimension_semantics=("arbitrary",),
                                             vmem_limit_bytes=VMEM_LIMIT),
        name="ssm_prompt",
    )(x.reshape(n * l, d), *consts)
    return out, sre[:, 0], sim[:, 0]


def _ssm_sample(x_tm, h0re, h0im, sp, gmix, dskip, wglu, bglu, n_seq, n_steps):
    rows, d = x_tm.shape
    nst = sp['a_re'].shape[1]
    args = [x_tm, h0re, h0im, gmix, sp['a_re'], sp['a_im'], sp['b_tiles'], sp['c_re'], sp['nc_im'],
            dskip, wglu, bglu]
    return pl.pallas_call(
        functools.partial(_ssm_sample_kernel, n_seq=n_seq, n_steps=n_steps),
        grid=(1,),
        in_specs=[_const_spec(a.shape) for a in args],
        out_specs=[pl.BlockSpec((rows, d), lambda i: (0, 0)), pl.BlockSpec((n_seq, nst), lambda i: (0, 0)),
                   pl.BlockSpec((n_seq, nst), lambda i: (0, 0))],
        out_shape=[jax.ShapeDtypeStruct((rows, d), F32),
                   jax.ShapeDtypeStruct((n_seq, nst), F32),
                   jax.ShapeDtypeStruct((n_seq, nst), F32)],
        scratch_shapes=[pltpu.VMEM((rows, nst), F32), pltpu.VMEM((rows, nst), F32)],
        compiler_params=pltpu.CompilerParams(dimension_semantics=("arbitrary",),
                                             vmem_limit_bytes=VMEM_LIMIT),
        name="ssm_sample",
    )(*args)


def _ffn_kernel(*refs, rows, head_rows, shift, has_attn):
    if has_attn:
        h_ref, o_ref, wo_ref = refs[:3]
        refs = refs[3:]
    else:
        h_ref = refs[0]
        refs = refs[1:]
    (p_ref, head_ref, gffn_ref, wup_ref, cw_ref, cb_ref, wdown_ref, gple_ref, wgate_ref, win_ref,
     out_ref, tail_ref, acc, xn_ref, ext_a, ext_b) = refs
    ff = wdown_ref.shape[0]
    n_chunk = ff // FF_CHUNK
    up0 = head_rows
    up1 = head_rows + rows

    @pl.when(pl.program_id(1) == 0)
    def _():
        tail_ref[...] = head_ref[...]

    x = h_ref[...]
    if has_attn:
        x = x + _dot(o_ref[...], wo_ref[...])
    xn_ref[...] = _rms(x, gffn_ref[...]).astype(BF16)
    acc[...] = jnp.zeros_like(acc)

    n_slab = ext_a.shape[1]

    def slab_cols(c, half, s):
        c0 = half * ff + c * FF_CHUNK + s * LANES
        return slice(c0, c0 + LANES)

    def up_project(c, ext):
        for half in range(2):
            c0 = half * ff + c * FF_CHUNK
            up = _dot(xn_ref[...], wup_ref[:, c0:c0 + FF_CHUNK])
            for s in range(n_slab):
                ext[half, s, 0:up0, :] = tail_ref[:, slab_cols(c, half, s)]
                ext[half, s, up0:up1, :] = up[:, s * LANES:(s + 1) * LANES]

    def conv(ext, half, c):
        parts = []
        for s in range(n_slab):
            ls = slab_cols(c, half, s)
            t2 = ext[half, s, up0 - 2 * shift:up1 - 2 * shift, :]
            t1 = ext[half, s, up0 - shift:up1 - shift, :]
            tail_ref[:, ls] = ext[half, s, rows:up1, :]
            r = cb_ref[:, ls] + t2 * cw_ref[0:1, ls]
            r = r + t1 * cw_ref[1:2, ls]
            parts.append(r + ext[half, s, up0:up1, :] * cw_ref[2:3, ls])
        return jnp.concatenate(parts, axis=1)

    def down_project(c, ext):
        cg = conv(ext, 0, c)
        cv = conv(ext, 1, c)
        act = (cg * jax.nn.sigmoid(cg) * cv).astype(BF16)
        acc[...] += _dot(act, wdown_ref[c * FF_CHUNK:(c + 1) * FF_CHUNK, :])

    bufs = (ext_a, ext_b)
    up_project(0, bufs[0])
    for c in range(n_chunk):
        if c + 1 < n_chunk:
            up_project(c + 1, bufs[(c + 1) % 2])
        down_project(c, bufs[c % 2])

    h2 = x + acc[...]
    gate = jax.nn.sigmoid(_dot(_rms(h2, gple_ref[...]).astype(BF16), wgate_ref[...]))
    pe = _dot(p_ref[...].astype(BF16), win_ref[...])
    out_ref[...] = h2 + pe * gate


def _ffn_weights(g_ffn, w_up, conv_w, conv_b, w_down, g_ple, w_gate, w_in):
    depth, d, ff2 = w_up.shape
    return dict(
        g_ffn=g_ffn.reshape(depth, 1, d),
        w_up=w_up.astype(BF16),
        conv_w=conv_w,
        conv_b=conv_b.reshape(depth, 1, ff2),
        w_down=w_down.astype(BF16),
        g_ple=g_ple.reshape(depth, 1, d),
        w_gate=w_gate.astype(BF16),
        w_in=w_in.astype(BF16),
    )


def _layer_spec(shape, layer):
    nd = len(shape)
    return pl.BlockSpec((None,) + tuple(shape[1:]), lambda *_: (layer,) + (0,) * (nd - 1),
                        pipeline_mode=pl.Buffered(1))


def _ffn(h, p, layer, head, head_off, fw, rows, n_seq_tiles, shift, attn=None):
    total, d = h.shape
    tiles = total // rows // n_seq_tiles
    head_rows, ff2 = head.shape[1], head.shape[2]
    row_map = lambda i, t: (i * tiles + t, 0)
    p_map = lambda i, t: (layer * (total // rows) + i * tiles + t, 0)
    consts = [fw['g_ffn'], fw['w_up'], fw['conv_w'], fw['conv_b'], fw['w_down'], fw['g_ple'], fw['w_gate'],
              fw['w_in']]
    args = [h]
    in_specs = [pl.BlockSpec((rows, d), row_map)]
    if attn is not None:
        o, wo = attn
        args += [o, wo]
        in_specs += [pl.BlockSpec((rows, o.shape[1]), row_map), _const_spec(wo.shape)]
    args += [p, head] + consts
    in_specs += [pl.BlockSpec((rows, p.shape[1]), p_map),
                 pl.BlockSpec((None, head_rows, ff2), lambda i, t: (head_off + i, 0, 0))]
    in_specs += [_layer_spec(c.shape, layer) for c in consts]
    out, tail = pl.pallas_call(
        functools.partial(_ffn_kernel, rows=rows, head_rows=head_rows, shift=shift, has_attn=attn is not None),
        grid=(n_seq_tiles, tiles),
        in_specs=in_specs,
        out_specs=[pl.BlockSpec((rows, d), row_map),
                   pl.BlockSpec((None, head_rows, ff2), lambda i, t: (i, 0, 0))],
        out_shape=[jax.ShapeDtypeStruct((total, d), F32),
                   jax.ShapeDtypeStruct((n_seq_tiles, head_rows, ff2), F32)],
        scratch_shapes=[pltpu.VMEM((rows, d), F32),
                        pltpu.VMEM((rows, d), BF16),
                        pltpu.VMEM((2, FF_CHUNK // LANES, head_rows + rows, LANES), F32),
                        pltpu.VMEM((2, FF_CHUNK // LANES, head_rows + rows, LANES), F32)],
        compiler_params=pltpu.CompilerParams(dimension_semantics=("arbitrary", "arbitrary"),
                                             vmem_limit_bytes=VMEM_LIMIT),
        name="ffn_ple",
    )(*args)
    return out, tail


def _pair_spread(x, fill):
    assert x.shape[1] == 2 * LANES and HEAD_DIM * 2 == LANES
    r = pltpu.roll(x, HEAD_DIM, 1)
    low = lax.broadcasted_iota(jnp.int32, (x.shape[0], LANES), 1) < HEAD_DIM
    x01, x23 = x[:, :LANES], x[:, LANES:]
    r30, r12 = r[:, :LANES], r[:, LANES:]
    first = lambda t: jnp.where(low, t, fill)
    second = lambda t: jnp.where(low, fill, t)
    return jnp.concatenate([first(x01), second(r30), first(r12), second(x01),
                            first(x23), second(r12), first(r30), second(x23)], axis=1)


def _qkv_kernel(*refs, paired):
    (h_ref, gkv_ref, gmix_ref, wk_ref, wv_ref, wq_ref, knorm_ref, qnorm_ref, kpool_ref, kspread_ref,
     qpool_ref, qspread_ref, q_ref, k_ref, v_ref) = refs[:15]
    x = h_ref[...]
    s = _rms(x, gkv_ref[...]).astype(BF16)
    k = _head_rms(_dot(s, wk_ref[...]), kpool_ref[...], kspread_ref[...], knorm_ref[...])
    v = _dot(s, wv_ref[...])
    xn = _rms(x, gmix_ref[...]).astype(BF16)
    q = _head_rms(_dot(xn, wq_ref[...]), qpool_ref[...], qspread_ref[...], qnorm_ref[...])
    if paired:
        kx_ref, vx_ref = refs[15:]
        kx_ref[...] = _pair_spread(k, 0.0).astype(BF16)
        vx_ref[...] = _pair_spread(v, 1.0).astype(BF16)
    k_ref[...] = k
    v_ref[...] = v
    q_ref[...] = (q * (HEAD_DIM ** -0.5)).astype(q_ref.dtype)


def _head_pool_matrices(n):
    member = (jnp.arange(n)[:, None] // HEAD_DIM) == jnp.arange(LANES)[None, :]
    return member.astype(BF16) * (1.0 / HEAD_DIM), member.T.astype(BF16)


def _qkv(h, qw, rows, paired):
    total, d = h.shape
    hk = qw['w_k'].shape[1]
    hq = qw['w_q'].shape[1]
    consts = [qw['g_kv'], qw['g_mix'], qw['w_k'], qw['w_v'], qw['w_q'], qw['k_norm'], qw['q_norm'],
              *qw['k_pool'], *qw['q_pool']]
    row_map = lambda i: (i, 0)
    out_specs = [pl.BlockSpec((rows, hq), row_map), pl.BlockSpec((rows, hk), row_map),
                 pl.BlockSpec((rows, hk), row_map)]
    out_shape = [jax.ShapeDtypeStruct((total, hq), BF16),
                 jax.ShapeDtypeStruct((total, hk), F32),
                 jax.ShapeDtypeStruct((total, hk), F32)]
    if paired:
        out_specs += [pl.BlockSpec((rows, 4 * hk), row_map)] * 2
        out_shape += [jax.ShapeDtypeStruct((total, 4 * hk), BF16)] * 2
    return pl.pallas_call(
        functools.partial(_qkv_kernel, paired=paired),
        grid=(total // rows,),
        in_specs=[pl.BlockSpec((rows, d), row_map)] + [_const_spec(c.shape) for c in consts],
        out_specs=out_specs,
        out_shape=out_shape,
        compiler_params=pltpu.CompilerParams(dimension_semantics=("arbitrary",),
                                             vmem_limit_bytes=VMEM_LIMIT),
        name="qkv",
    )(h, *consts)


def _t5_bucket(dist):
    max_exact = NUM_BUCKETS // 2
    df = jnp.maximum(dist, 1).astype(F32)
    large = max_exact + (jnp.log(df / max_exact) / math.log(MAX_DISTANCE / max_exact)
                         * (NUM_BUCKETS - max_exact)).astype(jnp.int32)
    return jnp.where(dist < max_exact, dist, jnp.minimum(large, NUM_BUCKETS - 1))


def _bias_table(rel_bias, lb, qb, lk):
    dist = lb + jnp.arange(qb)[:, None] - jnp.arange(lk)[None, :]
    per_dist = rel_bias[_t5_bucket(jnp.arange(WINDOW + 1))].astype(F32)
    onehot = (jnp.clip(dist, 0, WINDOW)[..., None] == jnp.arange(WINDOW + 1)).astype(F32)
    bias = jnp.einsum('qkd,dh->hqk', onehot, per_dist, precision=lax.Precision.HIGHEST)
    ok = (dist >= 0) & (dist <= WINDOW)
    return jnp.where(ok[None], bias, -jnp.inf)


def _attn_prompt_kernel(sink_ref, q_ref, kx_ref, vx_ref, bias_ref, o_ref, kx_prev, vx_prev):
    qb = kx_prev.shape[0]
    n_blocks = q_ref.shape[0] // qb
    n_heads = bias_ref.shape[1]
    rep = n_heads // N_KV_HEADS
    grp = 4 * HEAD_DIM
    step = pl.program_id(1)

    @pl.when(step == 0)
    def _():
        kx_prev[...] = jnp.zeros_like(kx_prev)
        vx_prev[...] = jnp.zeros_like(vx_prev)

    first_table = jnp.minimum(step, 1)
    low = lax.broadcasted_iota(jnp.int32, (qb, LANES), 1) < HEAD_DIM
    for g in range(N_KV_HEADS):
        gs = slice(g * grp, (g + 1) * grp)
        kx_all = jnp.concatenate([kx_prev[:, gs], kx_ref[:, gs]], axis=0)
        vx_all = jnp.concatenate([vx_prev[:, gs], vx_ref[:, gs]], axis=0)
        for blk in range(n_blocks):
            rows = slice(blk * qb, (blk + 1) * qb)
            kx = kx_all[blk * qb:(blk + 2) * qb]
            vx = vx_all[blk * qb:(blk + 2) * qb]
            table = first_table if blk == 0 else 1
            for pair in range(rep // 2):
                h0 = g * rep + 2 * pair
                ls = slice(h0 * HEAD_DIM, h0 * HEAD_DIM + LANES)
                qp = q_ref[rows, ls]
                res = []
                for h, kh in ((h0, kx[:, :LANES]), (h0 + 1, kx[:, LANES:])):
                    s = _dot_nt(qp, kh) + bias_ref[table, h]
                    m = jnp.maximum(jnp.max(s, axis=-1, keepdims=True), sink_ref[h])
                    res.append((_dot(jnp.exp(s - m).astype(BF16), vx), jnp.exp(sink_ref[h] - m)))
                (ra, ea), (rb, eb) = res
                num = jnp.where(low, ra[:, :LANES], rb[:, LANES:])
                den = jnp.where(low, ra[:, LANES:], rb[:, :LANES]) + jnp.where(low, ea, eb)
                o_ref[rows, ls] = (num / den).astype(o_ref.dtype)
    last = slice((n_blocks - 1) * qb, n_blocks * qb)
    kx_prev[...] = kx_ref[last, :]
    vx_prev[...] = vx_ref[last, :]


def _attn_prompt(q, kx, vx, bias, sinks, n, l):
    hq = q.shape[1]
    wx = kx.shape[1]
    qb = WINDOW
    rows = next(m * qb for m in (8, 4, 2, 1) if l % (m * qb) == 0)
    steps = l // rows
    cur = lambda i, b, *_: (i * steps + b, 0)
    return pl.pallas_call(
        _attn_prompt_kernel,
        grid_spec=pltpu.PrefetchScalarGridSpec(
            num_scalar_prefetch=1,
            grid=(n, steps),
            in_specs=[pl.BlockSpec((rows, hq), cur), pl.BlockSpec((rows, wx), cur), pl.BlockSpec((rows, wx), cur),
                      pl.BlockSpec(bias.shape, lambda i, b, *_: (0, 0, 0, 0), pipeline_mode=pl.Buffered(1))],
            out_specs=pl.BlockSpec((rows, hq), cur),
            scratch_shapes=[pltpu.VMEM((qb, wx), BF16), pltpu.VMEM((qb, wx), BF16)]),
        out_shape=jax.ShapeDtypeStruct((n * l, hq), BF16),
        compiler_params=pltpu.CompilerParams(dimension_semantics=("arbitrary", "arbitrary"),
                                             vmem_limit_bytes=VMEM_LIMIT),
        name="attn_prompt",
    )(sinks, q, kx, vx, bias)


def _attn_sample_kernel(sink_ref, q_ref, kc_ref, vc_ref, kn_ref, vn_ref, bias_ref,
                        o_ref, kwin_ref, vwin_ref, *, n_new):
    lb = kc_ref.shape[1]
    new_rows = kn_ref.shape[1]
    rows = q_ref.shape[1]
    hk = kc_ref.shape[2]
    grp = rows // N_KV_HEADS
    zero_rows = jnp.zeros((bias_ref.shape[1] - lb - new_rows, hk), BF16)
    lane_head = lax.broadcasted_iota(jnp.int32, (grp, hk), 1) // HEAD_DIM
    samples = range(q_ref.shape[0])
    scores = []
    for s_i in samples:
        kk = jnp.concatenate([kc_ref[s_i].astype(BF16), kn_ref[s_i].astype(BF16), zero_rows], axis=0)
        scores.append(_dot_nt(q_ref[s_i], kk) + bias_ref[...])
    probs = []
    for s in scores:
        m = jnp.maximum(jnp.max(s, axis=-1, keepdims=True), sink_ref[...])
        pr = jnp.exp(s - m)
        probs.append((pr.astype(BF16), jnp.sum(pr, axis=-1, keepdims=True) + jnp.exp(sink_ref[...] - m)))
    for s_i, (pr, den) in zip(samples, probs):
        vv = jnp.concatenate([vc_ref[s_i].astype(BF16), vn_ref[s_i].astype(BF16), zero_rows], axis=0)
        pv = _dot(pr, vv) / den
        o = jnp.zeros((grp, hk), F32)
        for g in range(N_KV_HEADS):
            o = o + jnp.where(lane_head == g, pv[g * grp:(g + 1) * grp, :], 0.0)
        o_ref[s_i] = o
    for s_i in samples:
        kwin_ref[s_i, 0:lb - n_new, :] = kc_ref[s_i, n_new:lb, :]
        kwin_ref[s_i, lb - n_new:lb, :] = kn_ref[s_i, 0:n_new, :]
        vwin_ref[s_i, 0:lb - n_new, :] = vc_ref[s_i, n_new:lb, :]
        vwin_ref[s_i, lb - n_new:lb, :] = vn_ref[s_i, 0:n_new, :]


def _attn_sample(q_blk, kc, vc, kn, vn, bias, sink, n_new, block):
    ns, rows, hk = q_blk.shape
    lb = kc.shape[1]
    lkp = bias.shape[1]
    grp = rows // N_KV_HEADS
    per_s = lambda i: (i, 0, 0)
    return pl.pallas_call(
        functools.partial(_attn_sample_kernel, n_new=n_new),
        grid=(ns // block,),
        in_specs=[_const_spec(sink.shape),
                  pl.BlockSpec((block, rows, hk), per_s),
                  pl.BlockSpec((block, lb, hk), per_s), pl.BlockSpec((block, lb, hk), per_s),
                  pl.BlockSpec((block,) + kn.shape[1:], per_s), pl.BlockSpec((block,) + vn.shape[1:], per_s),
                  _const_spec(bias.shape)],
        out_specs=[pl.BlockSpec((block, grp, hk), per_s),
                   pl.BlockSpec((block, lb, hk), per_s), pl.BlockSpec((block, lb, hk), per_s)],
        out_shape=[jax.ShapeDtypeStruct((ns, grp, hk), F32),
                   jax.ShapeDtypeStruct((ns, lb, hk), F32),
                   jax.ShapeDtypeStruct((ns, lb, hk), F32)],
        compiler_params=pltpu.CompilerParams(dimension_semantics=("arbitrary",),
                                             vmem_limit_bytes=VMEM_LIMIT),
        name="attn_sample",
    )(sink, q_blk, kc, vc, kn, vn, bias)


def kernel(x_prompt, x_sample, state_ssm_re, state_ssm_im, state_ffn_conv, cache_k_win, cache_v_win, p_prompt, p_sample, g_mix, g_ffn, g_ple, ssm_lam_re, ssm_lam_im, ssm_log_dt, ssm_b_re, ssm_b_im, ssm_c_re, ssm_c_im, ssm_d, w_glu, b_glu, g_kv, w_k, w_v, k_norm, w_q, q_norm, sinks, w_o, rel_bias, w_up, conv_w, conv_b, w_down, w_ple_in, w_ple_gate):
    n, l, d = x_prompt.shape
    ns, ls, _ = x_sample.shape
    n_groups, n_state = ssm_lam_re.shape[1:]
    nst = n_groups * n_state
    ff2 = w_up.shape[2]
    hk = w_k.shape[1]
    hq = w_q.shape[2]
    n_heads = hq // HEAD_DIM
    rep = n_heads // N_KV_HEADS
    lb = cache_k_win.shape[1]

    ssm_steps = min(32, l // SUBLANES)
    ffn_rows = min(512, l)
    head_rows_p = SUBLANES
    tm = lambda a: jnp.swapaxes(a, 0, 1)

    sp = _ssm_params(ssm_lam_re[0], ssm_lam_im[0], ssm_log_dt[0], ssm_b_re[0], ssm_b_im[0], ssm_c_re[0],
                     ssm_c_im[0], ssm_steps)
    gmix0 = g_mix[0].reshape(1, d)
    dskip = ssm_d[0].reshape(1, d)
    wglu = jnp.transpose(w_glu[0].astype(BF16).reshape(d, 2 * d // MXU_DIM, MXU_DIM), (1, 0, 2))
    bglu = b_glu[0].reshape(1, 2 * d)
    hp, sre_p, sim_p = _ssm_prompt(x_prompt, sp, gmix0, dskip, wglu, bglu, ssm_steps)
    xs_tm = tm(x_sample).reshape(ls * ns, d)
    hs, sre_s, sim_s = _ssm_sample(xs_tm, state_ssm_re[0].reshape(ns, nst), state_ssm_im[0].reshape(ns, nst),
                                   sp, gmix0, dskip, wglu, bglu, ns, ls)

    pp = p_prompt.reshape(p_prompt.shape[0] * n * l, -1)
    ps = jnp.swapaxes(p_sample, 1, 2).reshape(p_sample.shape[0] * ls * ns, -1)
    zero_head = jnp.zeros((n, head_rows_p, ff2), F32)
    conv_s_tm = jnp.swapaxes(state_ffn_conv, 1, 2).reshape(state_ffn_conv.shape[0], (CONV_WIDTH - 1) * ns, ff2)

    fw = _ffn_weights(g_ffn, w_up, conv_w, conv_b, w_down, g_ple, w_ple_gate, w_ple_in)

    def ffn_layer(i, hp, hs, attn_p=None, attn_s=None):
        hp, tail_p = _ffn(hp, pp, i, zero_head, 0, fw, ffn_rows, n, 1, attn_p)
        hs, tail_s = _ffn(hs, ps, i, conv_s_tm, i, fw, ls * ns, 1, ns, attn_s)
        conv_p = tail_p[:, head_rows_p - (CONV_WIDTH - 1):, :]
        return hp, hs, conv_p, tail_s.reshape(CONV_WIDTH - 1, ns, ff2)

    hp, hs, conv_p0, conv_s0 = ffn_layer(0, hp, hs)

    qw = dict(g_kv=g_kv.reshape(1, d), g_mix=g_mix[1].reshape(1, d), w_k=w_k.astype(BF16), w_v=w_v.astype(BF16),
              w_q=w_q[0].astype(BF16), k_norm=jnp.tile(k_norm, hk // HEAD_DIM).reshape(1, hk),
              q_norm=jnp.tile(q_norm[0], n_heads).reshape(1, hq),
              k_pool=_head_pool_matrices(hk), q_pool=_head_pool_matrices(hq))
    qkv_rows = 2 * ffn_rows if (n * l) % (2 * ffn_rows) == 0 else ffn_rows
    q_p, k_p, v_p, kx_p, vx_p = _qkv(hp, qw, qkv_rows, True)
    q_s, k_s, v_s = _qkv(hs, qw, ls * ns, False)

    wo = w_o[0].astype(BF16)
    bias_p = _bias_table(rel_bias, WINDOW, WINDOW, 2 * WINDOW)
    bias_first = jnp.where(jnp.arange(2 * WINDOW) >= WINDOW, bias_p, -jnp.inf)
    o_p = _attn_prompt(q_p, kx_p, vx_p, jnp.stack([bias_first, bias_p]), sinks[0].astype(F32), n, l)

    lkp = 2 * WINDOW
    new_rows = 16
    pad_new = lambda a: jnp.pad(tm(a.reshape(ls, ns, hk)), ((0, 0), (0, new_rows - ls), (0, 0)))
    kc = cache_k_win.reshape(ns, lb, hk)
    vc = cache_v_win.reshape(ns, lb, hk)
    bias_s = _bias_table(rel_bias, lb, ls, lb + ls)
    bias_s = jnp.pad(bias_s, ((0, 0), (0, 0), (0, lkp - lb - ls)), constant_values=-jnp.inf)
    bias_s = bias_s.reshape(n_heads * ls, lkp)
    sink_s = jnp.repeat(sinks[0].astype(F32), ls).reshape(n_heads * ls, 1)
    q5 = jnp.transpose(q_s.reshape(ls, ns, N_KV_HEADS, rep, HEAD_DIM), (1, 2, 3, 0, 4))
    q_blk = jnp.einsum('sgrtd,gh->sgrthd', q5, jnp.eye(N_KV_HEADS, dtype=q5.dtype))
    q_blk = q_blk.reshape(ns, n_heads * ls, hk)
    o4, k_win_s, v_win_s = _attn_sample(q_blk, kc, vc, pad_new(k_s), pad_new(v_s), bias_s, sink_s, ls, 8)
    o_s = jnp.transpose(o4.reshape(ns, rep, ls, N_KV_HEADS, HEAD_DIM), (2, 0, 3, 1, 4)).reshape(ls * ns, hq)
    o_s = o_s.astype(BF16)

    hp, hs, conv_p1, conv_s1 = ffn_layer(1, hp, hs, (o_p, wo), (o_s, wo))

    y_prompt = hp.reshape(n, l, d)
    y_sample = tm(hs.reshape(ls, ns, d))
    ssm_shape = (1, -1, n_groups, n_state)
    kvh_shape = (-1, lb, N_KV_HEADS, HEAD_DIM)
    k_win_p = k_p.reshape(n, l, hk)[:, l - WINDOW:].reshape(n, WINDOW, N_KV_HEADS, HEAD_DIM)
    v_win_p = v_p.reshape(n, l, hk)[:, l - WINDOW:].reshape(n, WINDOW, N_KV_HEADS, HEAD_DIM)
    return (y_prompt, y_sample,
            sre_p.reshape(ssm_shape), sim_p.reshape(ssm_shape),
            sre_s.reshape(ssm_shape), sim_s.reshape(ssm_shape),
            jnp.stack([conv_p0, conv_p1]), jnp.swapaxes(jnp.stack([conv_s0, conv_s1]), 1, 2),
            k_win_p, v_win_p, k_win_s.reshape(kvh_shape), v_win_s.reshape(kvh_shape))
```
